```python
import math
import jax
import jax.numpy as jnp
from jax import lax
import numpy as np

D_MODEL = 2048
BATCH = 8
SEQ = 2048
DEPTH = 2

CTX_LEN = 256
GRID_W = 64
ROPE_BASE = 10000.0
Q_BLOCK = 128

MLA_HEADS = 4
MLA_Q_LORA = 384
MLA_KV_LORA = 256
MLA_NOPE = 128
MLA_ROPE = 64
MLA_V = 128
DIFF_HEADS = 4
DIFF_HD = 64
LRU_WIDTH = 512
LRU_BLOCKS = 4
LRU_CONV = 4
LRU_C = 8.0
SSD_HEADS = 8
SSD_HD = 64
SSD_WIDTH = SSD_HEADS * SSD_HD
SSD_GROUPS = 2
SSD_STATE = 128
SSD_CONV = 4
SSD_CHUNK = 128
SSD_XBC_W = SSD_WIDTH + 2 * SSD_GROUPS * SSD_STATE
N_BRANCH = 4
BRANCH_WIDTH = 512
N_EXPERTS = 32
TOP_K = 4
D_EXPERT = D_MODEL
SWIGLU_LIMIT = 7.0
SWIGLU_ALPHA = 1.702
MOE_BLOCK = 256
DEEPNORM_ALPHA = (2 * DEPTH) ** 0.25
DEEPNORM_BETA = (8 * DEPTH) ** -0.25

C_MLA_CKV = 0
C_MLA_KR = C_MLA_CKV + MLA_KV_LORA
C_DIFF_K = C_MLA_KR + MLA_ROPE
C_DIFF_V = C_DIFF_K + 2 * DIFF_HEADS * DIFF_HD
C_LRU_X = C_DIFF_V + DIFF_HEADS * 2 * DIFF_HD
C_SSD_XBC = C_LRU_X + LRU_WIDTH
C_SSD_DT = C_SSD_XBC + SSD_XBC_W
N_STATE_COLS = C_SSD_DT + 2 * SSD_HEADS
C_MLA_CQ = N_STATE_COLS
C_DIFF_Q = C_MLA_CQ + MLA_Q_LORA
C_LRU_GATE = C_DIFF_Q + 2 * DIFF_HEADS * DIFF_HD
C_SSD_Z = C_LRU_GATE + LRU_WIDTH
C_GATES = C_SSD_Z + SSD_WIDTH
N_IN_COLS = C_GATES + N_BRANCH * D_MODEL

kernel_name = 'hybrid_mla_diff_rglru_ssd_moe_dit'


def cols(p, start, width):
    return p[..., start:start + width]


def layer_norm(x, g=None, b=None, eps=1e-6):
    xf = x.astype(jnp.float32)
    xc = xf - jnp.mean(xf, axis=-1, keepdims=True)
    y = xc * lax.rsqrt(jnp.mean(xc * xc, axis=-1, keepdims=True) + eps)
    if g is not None:
        y = y * g.astype(jnp.float32) + b.astype(jnp.float32)
    return y.astype(x.dtype)


def rms_norm(x, g, eps=1e-6):
    xf = x.astype(jnp.float32)
    y = xf * lax.rsqrt(jnp.mean(xf * xf, axis=-1, keepdims=True) + eps) * g.astype(jnp.float32)
    return y.astype(x.dtype)


def modulate(x, shift, scale):
    return layer_norm(x) * (1 + scale) + shift


def axial_rope(n_tok, rot_dim):
    rows = n_tok // GRID_W
    row = jnp.repeat(jnp.arange(rows), GRID_W).astype(jnp.float32)
    col = jnp.tile(jnp.arange(GRID_W), rows).astype(jnp.float32)
    n_freq = rot_dim // 4
    inv = jnp.power(ROPE_BASE, -jnp.arange(n_freq, dtype=jnp.float32) / n_freq)
    ang = jnp.concatenate([row[:, None] * inv, col[:, None] * inv], axis=-1)
    return jnp.cos(ang), jnp.sin(ang)


def apply_rope(x, cos, sin):
    half = x.shape[-1] // 2
    x1 = x[..., :half].astype(jnp.float32)
    x2 = x[..., half:].astype(jnp.float32)
    return jnp.concatenate([x1 * cos - x2 * sin, x2 * cos + x1 * sin], axis=-1).astype(x.dtype)


def over_query_blocks(fn, q):
    s, d = q.shape[-2], q.shape[-1]
    nb = s // Q_BLOCK
    qb = jnp.moveaxis(q.reshape(*q.shape[:-2], nb, Q_BLOCK, d), -3, 0)
    out = jnp.moveaxis(lax.map(fn, qb), 0, -3)
    return out.reshape(*out.shape[:-3], s, out.shape[-1])


def softmax_attend(q, k, v, scale):
    s = jnp.einsum('bhqd,bhkd->bhqk', q, k).astype(jnp.float32) * scale
    return jnp.einsum('bhqk,bhkd->bhqd', jax.nn.softmax(s, axis=-1).astype(v.dtype), v)


def merge_heads(y):
    b, h, n, d = y.shape
    return y.transpose(0, 2, 1, 3).reshape(b, n, h * d)


def centred_dwconv(x, w, b):
    k, ch = w.shape
    y = lax.conv_general_dilated(x, w[:, None, :].astype(x.dtype), window_strides=(1,),
                                 padding=[(k // 2, k - 1 - k // 2)],
                                 dimension_numbers=('NWC', 'WIO', 'NWC'), feature_group_count=ch)
    return y + b


def linear_scan(a, b, h0, reverse):
    first = -1 if reverse else 0
    b = b.at[:, first].add(a[:, first] * h0)

    def combine(l, r):
        return l[0] * r[0], r[0] * l[1] + r[1]

    return lax.associative_scan(combine, (a, b), axis=1, reverse=reverse)[1]


def block_diag(x, w):
    nblk, bw, _ = w.shape
    y = jnp.einsum('bnkc,kcd->bnkd', x.reshape(*x.shape[:-1], nblk, bw), w)
    return y.reshape(x.shape)


def rglru_inputs(xc, w_a, b_a, w_x, b_x, lam):
    r = jax.nn.sigmoid((block_diag(xc, w_a) + b_a).astype(jnp.float32))
    i = jax.nn.sigmoid((block_diag(xc, w_x) + b_x).astype(jnp.float32))
    log_a = -LRU_C * r * jax.nn.softplus(-lam.astype(jnp.float32))
    return jnp.exp(log_a), jnp.sqrt(-jnp.expm1(2.0 * log_a)) * (i * xc.astype(jnp.float32))


def ssd_scan(x, da, bm, cm, h0):
    b, n, h, p = x.shape
    g, s = bm.shape[2], bm.shape[3]
    e = h // g
    nc, ln = n // SSD_CHUNK, SSD_CHUNK
    x = x.reshape(b, nc, ln, g, e, p)
    bm = bm.reshape(b, nc, ln, g, s).astype(jnp.float32)
    cm = cm.reshape(b, nc, ln, g, s).astype(jnp.float32)
    cs = jnp.cumsum(da.reshape(b, nc, ln, g, e), axis=2)
    tri = jnp.tril(jnp.ones((ln, ln), dtype=bool))[:, :, None, None]
    decay = jnp.exp(jnp.where(tri, cs[:, :, :, None] - cs[:, :, None, :], -jnp.inf))
    cb = jnp.einsum('bclgs,bcmgs->bclmg', cm, bm)
    y_diag = jnp.einsum('bclmg,bclmge,bcmgep->bclgep', cb, decay, x)
    states = jnp.einsum('bclgs,bclge,bclgep->bcgeps', bm, jnp.exp(cs[:, :, -1:] - cs), x)
    ct = jnp.cumsum(jnp.pad(cs[:, :, -1], ((0, 0), (1, 0), (0, 0), (0, 0))), axis=1)
    tri_c = jnp.tril(jnp.ones((nc + 1, nc + 1), dtype=bool))[:, :, None, None]
    decay_c = jnp.exp(jnp.where(tri_c, ct[:, :, None] - ct[:, None, :], -jnp.inf))
    all_states = jnp.concatenate([h0.reshape(b, 1, g, e, p, s), states], axis=1)
    states = jnp.einsum('bzcge,bcgeps->bzgeps', decay_c, all_states)
    y_off = jnp.einsum('bclgs,bcgeps,bclge->bclgep', cm, states[:, :-1], jnp.exp(cs))
    return (y_diag + y_off).reshape(b, n, h, p), states[:, -1].reshape(b, h, p, s)


def ssd_direction(xs, bm, cm, dt_raw, h0, a_log, dt_bias, reverse):
    dt = jax.nn.softplus(dt_raw.astype(jnp.float32) + dt_bias.astype(jnp.float32))
    da = dt * -jnp.exp(a_log.astype(jnp.float32))
    xdt = xs.astype(jnp.float32) * dt[..., None]
    if reverse:
        xdt, da, bm, cm = (jnp.flip(t, axis=1) for t in (xdt, da, bm, cm))
    y, h_final = ssd_scan(xdt, da, bm, cm, h0)
    return (jnp.flip(y, axis=1) if reverse else y), h_final


def mla_mixer(pc, pl, rope, g_q, g_kv, w_uq, w_ukv, ctx_out):
    scale = (MLA_NOPE + MLA_ROPE) ** -0.5

    def keys_values(p, rot):
        b, n, _ = p.shape
        ckv = rms_norm(cols(p, C_MLA_CKV, MLA_KV_LORA), g_kv)
        kv = (ckv @ w_ukv).reshape(b, n, MLA_HEADS, MLA_NOPE + MLA_V).transpose(0, 2, 1, 3)
        k_rope = cols(p, C_MLA_KR, MLA_ROPE)[:, None]
        if rot is not None:
            k_rope = apply_rope(k_rope, *rot)
        k = jnp.concatenate([kv[..., :MLA_NOPE], jnp.broadcast_to(k_rope, (b, MLA_HEADS, n, MLA_ROPE))], axis=-1)
        return k, kv[..., MLA_NOPE:]

    def queries(p, rot):
        b, n, _ = p.shape
        cq = rms_norm(cols(p, C_MLA_CQ, MLA_Q_LORA), g_q)
        q = (cq @ w_uq).reshape(b, n, MLA_HEADS, MLA_NOPE + MLA_ROPE).transpose(0, 2, 1, 3)
        q_rope = q[..., MLA_NOPE:]
        if rot is not None:
            q_rope = apply_rope(q_rope, *rot)
        return jnp.concatenate([q[..., :MLA_NOPE], q_rope], axis=-1)

    k_c, v_c = keys_values(pc, None)
    k_l, v_l = keys_values(pl, rope)
    k_all = jnp.concatenate([k_c, k_l], axis=2)
    v_all = jnp.concatenate([v_c, v_l], axis=2)
    y_lat = merge_heads(over_query_blocks(lambda qb: softmax_attend(qb, k_all, v_all, scale), queries(pl, rope)))
    y_ctx = merge_heads(softmax_attend(queries(pc, None), k_c, v_c, scale)) if ctx_out else None
    return y_ctx, y_lat


def diff_mixer(pc, pl, rope, lq1, lk1, lq2, lk2, g_sub, lambda_init, ctx_out):
    scale = DIFF_HD ** -0.5
    lam = (jnp.exp(jnp.sum(lq1 * lk1).astype(jnp.float32)) - jnp.exp(jnp.sum(lq2 * lk2).astype(jnp.float32))
           + lambda_init)

    def qk(p, start, rot):
        b, n, _ = p.shape
        t = cols(p, start, 2 * DIFF_HEADS * DIFF_HD).reshape(b, n, DIFF_HEADS, 2, DIFF_HD).transpose(0, 2, 3, 1, 4)
        return t if rot is None else apply_rope(t, *rot)

    def values(p):
        b, n, _ = p.shape
        return cols(p, C_DIFF_V, DIFF_HEADS * 2 * DIFF_HD).reshape(b, n, DIFF_HEADS, 2 * DIFF_HD).transpose(0, 2, 1, 3)

    def attend(q, k, v):
        s = jnp.einsum('bhmqd,bhmkd->bhmqk', q, k).astype(jnp.float32) * scale
        prob = jax.nn.softmax(s, axis=-1)
        a = prob[:, :, 0] - lam * prob[:, :, 1]
        o = jnp.einsum('bhqk,bhkd->bhqd', a.astype(v.dtype), v)
        return rms_norm(o, g_sub) * (1.0 - lambda_init)

    k_c, v_c = qk(pc, C_DIFF_K, None), values(pc)
    k_all = jnp.concatenate([k_c, qk(pl, C_DIFF_K, rope)], axis=3)
    v_all = jnp.concatenate([v_c, values(pl)], axis=2)
    y_lat = merge_heads(over_query_blocks(lambda qb: attend(qb, k_all, v_all), qk(pl, C_DIFF_Q, rope)))
    y_ctx = merge_heads(attend(qk(pc, C_DIFF_Q, None), k_c, v_c)) if ctx_out else None
    return y_ctx, y_lat


def lru_mixer(pc, pl, conv_w, conv_b, w_a, b_a, w_x, b_x, lam, ctx_out):
    xc_c = centred_dwconv(cols(pc, C_LRU_X, LRU_WIDTH), conv_w, conv_b)
    xc_l = centred_dwconv(cols(pl, C_LRU_X, LRU_WIDTH), conv_w, conv_b)
    h0 = jnp.zeros((pc.shape[0], LRU_WIDTH), jnp.float32)
    hs_c, hs_l = [], []
    for d in range(2):
        rev = d == 1
        a_c, in_c = rglru_inputs(xc_c, w_a[d], b_a[d], w_x[d], b_x[d], lam[d])
        h_c = linear_scan(a_c, in_c, h0, rev)
        a_l, in_l = rglru_inputs(xc_l, w_a[d], b_a[d], w_x[d], b_x[d], lam[d])
        hs_l.append(linear_scan(a_l, in_l, h_c[:, 0] if rev else h_c[:, -1], rev))
        hs_c.append(h_c)
    gate_l = jax.nn.gelu(cols(pl, C_LRU_GATE, LRU_WIDTH).astype(jnp.float32))
    y_lat = ((hs_l[0] + hs_l[1]) * gate_l).astype(pl.dtype)
    y_ctx = None
    if ctx_out:
        gate_c = jax.nn.gelu(cols(pc, C_LRU_GATE, LRU_WIDTH).astype(jnp.float32))
        y_ctx = ((hs_c[0] + hs_c[1]) * gate_c).astype(pc.dtype)
    return y_ctx, y_lat


def ssd_mixer(pc, pl, conv_w, conv_b, a_log, dt_bias, d_skip, g_norm, ctx_out):
    gs = SSD_GROUPS * SSD_STATE

    def split_inputs(p):
        xbc = jax.nn.silu(centred_dwconv(cols(p, C_SSD_XBC, SSD_XBC_W), conv_w, conv_b))
        b, n, _ = xbc.shape
        xs = xbc[..., :SSD_WIDTH].reshape(b, n, SSD_HEADS, SSD_HD)
        bm = xbc[..., SSD_WIDTH:SSD_WIDTH + gs].reshape(b, n, SSD_GROUPS, SSD_STATE)
        cm = xbc[..., SSD_WIDTH + gs:].reshape(b, n, SSD_GROUPS, SSD_STATE)
        dt_raw = cols(p, C_SSD_DT, 2 * SSD_HEADS).reshape(b, n, 2, SSD_HEADS)
        return xs, bm, cm, dt_raw

    def gated_out(p, xs, y):
        b, n = xs.shape[:2]
        y = y + xs.astype(jnp.float32) * d_skip.astype(jnp.float32)[:, None]
        yz = y.reshape(b, n, SSD_WIDTH) * jax.nn.silu(cols(p, C_SSD_Z, SSD_WIDTH).astype(jnp.float32))
        yz = rms_norm(yz.reshape(b, n, SSD_GROUPS, SSD_WIDTH // SSD_GROUPS),
                      g_norm.reshape(SSD_GROUPS, SSD_WIDTH // SSD_GROUPS))
        return yz.reshape(b, n, SSD_WIDTH).astype(p.dtype)

    xs_c, bm_c, cm_c, dt_c = split_inputs(pc)
    xs_l, bm_l, cm_l, dt_l = split_inputs(pl)
    h0 = jnp.zeros((pc.shape[0], SSD_HEADS, SSD_HD, SSD_STATE), jnp.float32)
    ys_c, ys_l = [], []
    for d in range(2):
        rev = d == 1
        y_c, h_c = ssd_direction(xs_c, bm_c, cm_c, dt_c[:, :, d], h0, a_log[d], dt_bias[d], rev)
        y_l, _ = ssd_direction(xs_l, bm_l, cm_l, dt_l[:, :, d], h_c, a_log[d], dt_bias[d], rev)
        ys_c.append(y_c)
        ys_l.append(y_l)
    y_lat = gated_out(pl, xs_l, ys_l[0] + ys_l[1])
    y_ctx = gated_out(pc, xs_c, ys_c[0] + ys_c[1]) if ctx_out else None
    return y_ctx, y_lat


def merge_branches(p, ys, w_branch, w_out):
    d = w_out.shape[0]
    merged = 0.0
    for k, y in enumerate(ys):
        merged = merged + jax.nn.sigmoid(cols(p, C_GATES + k * d, d)) * (y @ w_branch[k])
    return merged @ w_out


def token_mixer(u_ctx, u_lat, lp, rope_mla, rope_diff, layer_idx, ctx_out):
    w_in = lp['w_in']
    p_lat = u_lat @ w_in
    p_ctx = u_ctx @ (w_in if ctx_out else w_in[:, :N_STATE_COLS])
    lambda_init = 0.8 - 0.6 * math.exp(-0.3 * layer_idx)
    branches = (
        mla_mixer(p_ctx, p_lat, rope_mla, lp['mla_g_q'], lp['mla_g_kv'], lp['mla_w_uq'], lp['mla_w_ukv'], ctx_out),
        diff_mixer(p_ctx, p_lat, rope_diff, lp['diff_lq1'], lp['diff_lk1'], lp['diff_lq2'], lp['diff_lk2'],
                   lp['diff_g_sub'], lambda_init, ctx_out),
        lru_mixer(p_ctx, p_lat, lp['lru_conv_w'], lp['lru_conv_b'], lp['lru_w_a'], lp['lru_b_a'],
                  lp['lru_w_x'], lp['lru_b_x'], lp['lru_lam'], ctx_out),
        ssd_mixer(p_ctx, p_lat, lp['ssd_conv_w'], lp['ssd_conv_b'], lp['ssd_a_log'], lp['ssd_dt_bias'],
                  lp['ssd_d'], lp['ssd_g_norm'], ctx_out),
    )
    y_lat = merge_branches(p_lat, [br[1] for br in branches], lp['w_branch'], lp['w_out'])
    y_ctx = merge_branches(p_ctx, [br[0] for br in branches], lp['w_branch'], lp['w_out']) if ctx_out else None
    return y_ctx, y_lat


def clamped_swiglu(u):
    glu = jnp.minimum(u[..., ::2], SWIGLU_LIMIT)
    lin = jnp.clip(u[..., 1::2], -SWIGLU_LIMIT, SWIGLU_LIMIT)
    return glu * jax.nn.sigmoid(SWIGLU_ALPHA * glu) * (lin + 1)


def moe_ffn(h, router_w, router_b, w1, b1, w2, b2):
    t, d = h.shape
    logits = (h @ router_w + router_b).astype(jnp.float32)
    top_logit, top_idx = lax.top_k(logits, TOP_K)
    top_w = jax.nn.softmax(top_logit, axis=-1).astype(h.dtype)
    n_assign = t * TOP_K
    flat_e = top_idx.reshape(-1)
    order = jnp.argsort(flat_e)
    sorted_e = flat_e[order]
    sorted_tok = order // TOP_K
    sorted_w = top_w.reshape(-1)[order]
    counts = jnp.bincount(flat_e, length=N_EXPERTS)
    padded = (counts + MOE_BLOCK - 1) // MOE_BLOCK * MOE_BLOCK
    start = jnp.cumsum(counts) - counts
    pend = jnp.cumsum(padded)
    pstart = pend - padded
    dest = pstart[sorted_e] + jnp.arange(n_assign) - start[sorted_e]
    n_blocks = (n_assign + N_EXPERTS * (MOE_BLOCK - 1) + MOE_BLOCK - 1) // MOE_BLOCK
    n_slots = n_blocks * MOE_BLOCK
    slot_tok = jnp.full((n_slots,), t, jnp.int32).at[dest].set(sorted_tok.astype(jnp.int32))
    slot_w = jnp.zeros((n_slots,), h.dtype).at[dest].set(sorted_w)
    block_e = jnp.minimum(jnp.searchsorted(pend, jnp.arange(n_blocks) * MOE_BLOCK, side='right'), N_EXPERTS - 1)
    h_pad = jnp.concatenate([h, jnp.zeros((1, d), h.dtype)], axis=0)

    def expert_block(args):
        tok, wgt, e = args
        u = h_pad[tok] @ w1[e] + b1[e]
        return (clamped_swiglu(u) @ w2[e] + b2[e]) * wgt[:, None]

    y = lax.map(expert_block, (slot_tok.reshape(n_blocks, MOE_BLOCK), slot_w.reshape(n_blocks, MOE_BLOCK), block_e))
    return jax.ops.segment_sum(y.reshape(n_slots, d), slot_tok, num_segments=t + 1)[:t]


def trunk_layer(x_ctx, x_lat, c, c_ctx, lp, rope_mla, rope_diff, layer_idx, ctx_out):
    b, n, d = x_lat.shape
    mod_lat = (jax.nn.silu(c) @ lp['w_mod'] + lp['b_mod'])[:, None, :]
    mod_ctx = jax.nn.silu(c_ctx) @ lp['w_mod'] + lp['b_mod']
    sh1, sc1, g1, sh2, sc2, g2 = jnp.split(mod_lat, 6, axis=-1)
    sh1c, sc1c, g1c, sh2c, sc2c, g2c = jnp.split(mod_ctx, 6, axis=-1)
    m_ctx, m_lat = token_mixer(modulate(x_ctx, sh1c, sc1c), modulate(x_lat, sh1, sc1), lp,
                               rope_mla, rope_diff, layer_idx, ctx_out)
    x_lat = layer_norm(DEEPNORM_ALPHA * x_lat + g1 * m_lat, lp['ln1_g'], lp['ln1_b'])
    v_lat = modulate(x_lat, sh2, sc2).reshape(b * n, d)
    if ctx_out:
        nc = x_ctx.shape[1]
        x_ctx = layer_norm(DEEPNORM_ALPHA * x_ctx + g1c * m_ctx, lp['ln1_g'], lp['ln1_b'])
        v_ctx = modulate(x_ctx, sh2c, sc2c).reshape(b * nc, d)
        f = moe_ffn(jnp.concatenate([v_ctx, v_lat], axis=0), lp['router_w'], lp['router_b'],
                    lp['w1'], lp['b1'], lp['w2'], lp['b2'])
        f_lat = f[b * nc:].reshape(b, n, d)
        x_ctx = layer_norm(DEEPNORM_ALPHA * x_ctx + g2c * f[:b * nc].reshape(b, nc, d), lp['ln2_g'], lp['ln2_b'])
    else:
        f_lat = moe_ffn(v_lat, lp['router_w'], lp['router_b'], lp['w1'], lp['b1'], lp['w2'], lp['b2']).reshape(b, n, d)
        x_ctx = None
    x_lat = layer_norm(DEEPNORM_ALPHA * x_lat + g2 * f_lat, lp['ln2_g'], lp['ln2_b'])
    return x_ctx, x_lat


def setup_inputs(seed: int = 0) -> dict:
    key = jax.random.key(seed)
    keys = iter(jax.random.split(key, 64))
    f32 = jnp.float32
    L, D = DEPTH, D_MODEL
    beta = DEEPNORM_BETA

    def normal(shape, scale):
        return jax.random.normal(next(keys), shape, f32) * scale

    def gain(shape):
        return 1.0 + normal(shape, 0.02)

    def small(shape):
        return normal(shape, 0.02)

    u_lru = jax.random.uniform(next(keys), (L, 2, LRU_WIDTH), f32, 0.9, 0.999) ** (1.0 / LRU_C)
    lru_lam = jnp.log(u_lru) - jnp.log1p(-u_lru)
    a_log = jnp.log(jax.random.uniform(next(keys), (L, 2, SSD_HEADS), f32, 1.0, 16.0))
    dt0 = jnp.exp(jax.random.uniform(next(keys), (L, 2, SSD_HEADS), f32, math.log(1e-3), math.log(1e-1)))
    dt_bias = dt0 + jnp.log(-jnp.expm1(-dt0))
    return {
        'x': normal((BATCH, SEQ, D), 1.0),
        'c': normal((BATCH, D), 1.0),
        'ctx': normal((BATCH, CTX_LEN, D), 1.0),
        'c_ctx': normal((D,), 1.0),
        'w_mod': normal((L, D, 6 * D), 0.3 * D ** -0.5),
        'b_mod': small((L, 6 * D)),
        'w_in': normal((L, D, N_IN_COLS), D ** -0.5),
        'mla_g_q': gain((L, MLA_Q_LORA)),
        'mla_g_kv': gain((L, MLA_KV_LORA)),
        'mla_w_uq': normal((L, MLA_Q_LORA, MLA_HEADS * (MLA_NOPE + MLA_ROPE)), MLA_Q_LORA ** -0.5),
        'mla_w_ukv': normal((L, MLA_KV_LORA, MLA_HEADS * (MLA_NOPE + MLA_V)), MLA_KV_LORA ** -0.5),
        'diff_lq1': normal((L, DIFF_HD), 0.1),
        'diff_lk1': normal((L, DIFF_HD), 0.1),
        'diff_lq2': normal((L, DIFF_HD), 0.1),
        'diff_lk2': normal((L, DIFF_HD), 0.1),
        'diff_g_sub': gain((L, 2 * DIFF_HD)),
        'lru_conv_w': normal((L, LRU_CONV, LRU_WIDTH), LRU_CONV ** -0.5),
        'lru_conv_b': small((L, LRU_WIDTH)),
        'lru_w_a': normal((L, 2, LRU_BLOCKS, LRU_WIDTH // LRU_BLOCKS, LRU_WIDTH // LRU_BLOCKS), (LRU_WIDTH // LRU_BLOCKS) ** -0.5),
        'lru_b_a': small((L, 2, LRU_WIDTH)),
        'lru_w_x': normal((L, 2, LRU_BLOCKS, LRU_WIDTH // LRU_BLOCKS, LRU_WIDTH // LRU_BLOCKS), (LRU_WIDTH // LRU_BLOCKS) ** -0.5),
        'lru_b_x': small((L, 2, LRU_WIDTH)),
        'lru_lam': lru_lam,
        'ssd_conv_w': normal((L, SSD_CONV, SSD_XBC_W), SSD_CONV ** -0.5),
        'ssd_conv_b': small((L, SSD_XBC_W)),
        'ssd_a_log': a_log,
        'ssd_dt_bias': dt_bias,
        'ssd_d': gain((L, SSD_HEADS)),
        'ssd_g_norm': gain((L, SSD_WIDTH)),
        'w_branch': normal((L, N_BRANCH, BRANCH_WIDTH, D), beta * BRANCH_WIDTH ** -0.5),
        'w_out': normal((L, D, D), beta * D ** -0.5),
        'ln1_g': gain((L, D)),
        'ln1_b': small((L, D)),
        'ln2_g': gain((L, D)),
        'ln2_b': small((L, D)),
        'router_w': normal((L, D, N_EXPERTS), D ** -0.5),
        'router_b': normal((L, N_EXPERTS), 0.01),
        'w1': normal((L, N_EXPERTS, D, 2 * D_EXPERT), beta * D ** -0.5),
        'b1': small((L, N_EXPERTS, 2 * D_EXPERT)),
        'w2': normal((L, N_EXPERTS, D_EXPERT, D), beta * D_EXPERT ** -0.5),
        'b2': small((L, N_EXPERTS, D)),
    }


def reference(x, c, ctx, c_ctx, w_mod, b_mod, w_in, mla_g_q, mla_g_kv, mla_w_uq, mla_w_ukv,
              diff_lq1, diff_lk1, diff_lq2, diff_lk2, diff_g_sub,
              lru_conv_w, lru_conv_b, lru_w_a, lru_b_a, lru_w_x, lru_b_x, lru_lam,
              ssd_conv_w, ssd_conv_b, ssd_a_log, ssd_dt_bias, ssd_d, ssd_g_norm,
              w_branch, w_out, ln1_g, ln1_b, ln2_g, ln2_b,
              router_w, router_b, w1, b1, w2, b2):
    n_lat = x.shape[1]
    rope_mla = axial_rope(n_lat, MLA_ROPE)
    rope_diff = axial_rope(n_lat, DIFF_HD)
    x_ctx, x_lat = ctx, x
    for i in range(DEPTH):
        lp = {
            'w_mod': w_mod[i], 'b_mod': b_mod[i], 'w_in': w_in[i],
            'mla_g_q': mla_g_q[i], 'mla_g_kv': mla_g_kv[i], 'mla_w_uq': mla_w_uq[i], 'mla_w_ukv': mla_w_ukv[i],
            'diff_lq1': diff_lq1[i], 'diff_lk1': diff_lk1[i], 'diff_lq2': diff_lq2[i], 'diff_lk2': diff_lk2[i],
            'diff_g_sub': diff_g_sub[i],
            'lru_conv_w': lru_conv_w[i], 'lru_conv_b': lru_conv_b[i], 'lru_w_a': lru_w_a[i], 'lru_b_a': lru_b_a[i],
            'lru_w_x': lru_w_x[i], 'lru_b_x': lru_b_x[i], 'lru_lam': lru_lam[i],
            'ssd_conv_w': ssd_conv_w[i], 'ssd_conv_b': ssd_conv_b[i], 'ssd_a_log': ssd_a_log[i],
            'ssd_dt_bias': ssd_dt_bias[i], 'ssd_d': ssd_d[i], 'ssd_g_norm': ssd_g_norm[i],
            'w_branch': w_branch[i], 'w_out': w_out[i],
            'ln1_g': ln1_g[i], 'ln1_b': ln1_b[i], 'ln2_g': ln2_g[i], 'ln2_b': ln2_b[i],
            'router_w': router_w[i], 'router_b': router_b[i], 'w1': w1[i], 'b1': b1[i], 'w2': w2[i], 'b2': b2[i],
        }
        x_ctx, x_lat = trunk_layer(x_ctx, x_lat, c, c_ctx, lp, rope_mla, rope_diff, i, i < DEPTH - 1)
    return x_lat
```

```python
import functools
import math

import jax
import jax.numpy as jnp
from jax import lax
from jax.experimental import pallas as pl
from jax.experimental.pallas import tpu as pltpu

D_MODEL = 2048
DEPTH = 2
GRID_W = 64
ROPE_BASE = 10000.0
Q_BLOCK = 128

MLA_HEADS = 4
MLA_Q_LORA = 384
MLA_KV_LORA = 256
MLA_NOPE = 128
MLA_ROPE = 64
MLA_V = 128
DIFF_HEADS = 4
DIFF_HD = 64
LRU_WIDTH = 512
LRU_C = 8.0
SSD_HEADS = 8
SSD_HD = 64
SSD_WIDTH = SSD_HEADS * SSD_HD
SSD_GROUPS = 2
SSD_STATE = 128
SSD_CHUNK = 128
SSD_XBC_W = SSD_WIDTH + 2 * SSD_GROUPS * SSD_STATE
N_BRANCH = 4
N_EXPERTS = 32
TOP_K = 4
D_EXPERT = D_MODEL
SWIGLU_LIMIT = 7.0
SWIGLU_ALPHA = 1.702
DEEPNORM_ALPHA = (2 * DEPTH) ** 0.25

C_MLA_CKV = 0
C_MLA_KR = C_MLA_CKV + MLA_KV_LORA
C_DIFF_K = C_MLA_KR + MLA_ROPE
C_DIFF_V = C_DIFF_K + 2 * DIFF_HEADS * DIFF_HD
C_LRU_X = C_DIFF_V + DIFF_HEADS * 2 * DIFF_HD
C_SSD_XBC = C_LRU_X + LRU_WIDTH
C_SSD_DT = C_SSD_XBC + SSD_XBC_W
N_STATE_COLS = C_SSD_DT + 2 * SSD_HEADS
C_MLA_CQ = N_STATE_COLS
C_DIFF_Q = C_MLA_CQ + MLA_Q_LORA
C_LRU_GATE = C_DIFF_Q + 2 * DIFF_HEADS * DIFF_HD
C_SSD_Z = C_LRU_GATE + LRU_WIDTH
C_GATES = C_SSD_Z + SSD_WIDTH
N_IN_COLS = C_GATES + N_BRANCH * D_MODEL

V7X_VMEM_LIMIT_BYTES = 48 * 1024 * 1024
LANE = 128

MOE_TM = 512
MOE_TJ = 512

BF16 = jnp.bfloat16
F32 = jnp.float32


def _mm_body(a_ref, b_ref, o_ref):
    o_ref[...] = jnp.dot(a_ref[...], b_ref[...], preferred_element_type=F32).astype(o_ref.dtype)


def _pick(n, prefs):
    for p in prefs:
        if n % p == 0:
            return p
    return n


def matmul(a, b, out_dtype=F32, tm=None, tn=None):
    m, k = a.shape
    _, n = b.shape
    a = a.astype(BF16)
    b = b.astype(BF16)
    n_pad = -n % LANE
    if n_pad:
        b = jnp.pad(b, ((0, 0), (0, n_pad)))
    np_ = n + n_pad
    tm = tm or _pick(m, (1024, 512, 256, 128, 64, 32, 16, 8))
    tn = tn or _pick(np_, (512, 384, 256, 128))
    out = pl.pallas_call(
        _mm_body,
        grid=(m // tm, np_ // tn),
        in_specs=[pl.BlockSpec((tm, k), lambda i, j: (i, 0)),
                  pl.BlockSpec((k, tn), lambda i, j: (0, j))],
        out_specs=pl.BlockSpec((tm, tn), lambda i, j: (i, j)),
        out_shape=jax.ShapeDtypeStruct((m, np_), out_dtype),
        compiler_params=pltpu.CompilerParams(
            dimension_semantics=("parallel", "parallel"),
            vmem_limit_bytes=V7X_VMEM_LIMIT_BYTES),
        name="dense_matmul",
    )(a, b)
    return out[:, :n] if n_pad else out


def _moe_body(be_ref, nvalid_ref, x_ref, w1g_ref, w1l_ref, b1g_ref, b1l_ref, w2_ref, b2_ref, wgt_ref, o_ref):
    i = pl.program_id(0)
    j = pl.program_id(1)
    valid = i < nvalid_ref[0]

    @pl.when(jnp.logical_not(valid))
    def _():
        @pl.when(j == 0)
        def _():
            o_ref[...] = jnp.zeros_like(o_ref)

    @pl.when(valid)
    def _():
        x = x_ref[...]
        glu = jnp.dot(x, w1g_ref[0], preferred_element_type=F32) + b1g_ref[0]
        lin = jnp.dot(x, w1l_ref[0], preferred_element_type=F32) + b1l_ref[0]
        glu = jnp.minimum(glu, SWIGLU_LIMIT)
        lin = jnp.clip(lin, -SWIGLU_LIMIT, SWIGLU_LIMIT)
        h = glu * jax.nn.sigmoid(SWIGLU_ALPHA * glu) * (lin + 1.0)
        part = jnp.dot(h.astype(BF16), w2_ref[0], preferred_element_type=F32)

        @pl.when(j == 0)
        def _():
            o_ref[...] = part

        @pl.when(j > 0)
        def _():
            o_ref[...] += part

        @pl.when(j == pl.num_programs(1) - 1)
        def _():
            o_ref[...] = (o_ref[...] + b2_ref[0]) * wgt_ref[...]


def moe_expert_blocks(x_sorted, slot_w, block_e, n_valid, w1g, w1l, b1g, b1l, w2, b2):
    n_slots, d = x_sorted.shape
    nb = n_slots // MOE_TM
    nj = D_EXPERT // MOE_TJ
    grid_spec = pltpu.PrefetchScalarGridSpec(
        num_scalar_prefetch=2,
        grid=(nb, nj),
        in_specs=[
            pl.BlockSpec((MOE_TM, d), lambda i, j, be, nv: (i, 0)),
            pl.BlockSpec((1, d, MOE_TJ), lambda i, j, be, nv: (be[i], 0, j)),
            pl.BlockSpec((1, d, MOE_TJ), lambda i, j, be, nv: (be[i], 0, j)),
            pl.BlockSpec((1, 1, MOE_TJ), lambda i, j, be, nv: (be[i], 0, j)),
            pl.BlockSpec((1, 1, MOE_TJ), lambda i, j, be, nv: (be[i], 0, j)),
            pl.BlockSpec((1, MOE_TJ, d), lambda i, j, be, nv: (be[i], j, 0)),
            pl.BlockSpec((1, 1, d), lambda i, j, be, nv: (be[i], 0, 0)),
            pl.BlockSpec((MOE_TM, 1), lambda i, j, be, nv: (i, 0)),
        ],
        out_specs=pl.BlockSpec((MOE_TM, d), lambda i, j, be, nv: (i, 0)),
    )
    return pl.pallas_call(
        _moe_body,
        grid_spec=grid_spec,
        out_shape=jax.ShapeDtypeStruct((n_slots, d), F32),
        compiler_params=pltpu.CompilerParams(
            dimension_semantics=("parallel", "arbitrary"),
            vmem_limit_bytes=V7X_VMEM_LIMIT_BYTES),
        name="moe_expert_ffn",
    )(block_e, n_valid, x_sorted, w1g, w1l, b1g, b1l, w2, b2, slot_w)


def moe_ffn(h, router_w, router_b, w1g, w1l, b1g, b1l, w2, b2):
    t, d = h.shape
    logits = jnp.dot(h, router_w, precision=lax.Precision.HIGHEST) + router_b
    top_logit, top_idx = lax.top_k(logits, TOP_K)
    top_w = jax.nn.softmax(top_logit, axis=-1)
    n_assign = t * TOP_K
    flat_e = top_idx.reshape(-1)
    order = jnp.argsort(flat_e)
    sorted_e = flat_e[order]
    sorted_tok = order // TOP_K
    sorted_w = top_w.reshape(-1)[order]
    counts = jnp.bincount(flat_e, length=N_EXPERTS)
    padded = (counts + MOE_TM - 1) // MOE_TM * MOE_TM
    start = jnp.cumsum(counts) - counts
    pend = jnp.cumsum(padded)
    pstart = pend - padded
    dest = (pstart[sorted_e] + jnp.arange(n_assign) - start[sorted_e]).astype(jnp.int32)
    n_blocks = (n_assign + N_EXPERTS * (MOE_TM - 1) + MOE_TM - 1) // MOE_TM
    n_slots = n_blocks * MOE_TM
    slot_tok = jnp.full((n_slots,), t, jnp.int32).at[dest].set(sorted_tok.astype(jnp.int32))
    slot_w = jnp.zeros((n_slots,), F32).at[dest].set(sorted_w)
    n_valid = (pend[-1] // MOE_TM).astype(jnp.int32)
    blk = jnp.arange(n_blocks)
    block_e = jnp.minimum(jnp.searchsorted(pend, jnp.minimum(blk, n_valid - 1) * MOE_TM, side='right'),
                          N_EXPERTS - 1).astype(jnp.int32)
    h_pad = jnp.concatenate([h.astype(BF16), jnp.zeros((1, d), BF16)], axis=0)
    x_sorted = h_pad[slot_tok]
    y = moe_expert_blocks(x_sorted, slot_w[:, None], block_e, n_valid.reshape(1), w1g, w1l, b1g, b1l, w2, b2)
    pos = jnp.zeros((n_assign,), jnp.int32).at[order].set(dest).reshape(t, TOP_K)
    return jnp.sum(y[pos], axis=1)


def cols(p, start, width):
    return p[..., start:start + width]


def layer_norm(x, g=None, b=None, eps=1e-6):
    xf = x.astype(F32)
    xc = xf - jnp.mean(xf, axis=-1, keepdims=True)
    y = xc * lax.rsqrt(jnp.mean(xc * xc, axis=-1, keepdims=True) + eps)
    if g is not None:
        y = y * g.astype(F32) + b.astype(F32)
    return y


def rms_norm(x, g, eps=1e-6):
    xf = x.astype(F32)
    return xf * lax.rsqrt(jnp.mean(xf * xf, axis=-1, keepdims=True) + eps) * g.astype(F32)


def modulate(x, shift, scale):
    return layer_norm(x) * (1 + scale) + shift


def axial_rope(n_tok, rot_dim):
    rows = n_tok // GRID_W
    row = jnp.repeat(jnp.arange(rows), GRID_W).astype(F32)
    col = jnp.tile(jnp.arange(GRID_W), rows).astype(F32)
    n_freq = rot_dim // 4
    inv = jnp.power(ROPE_BASE, -jnp.arange(n_freq, dtype=F32) / n_freq)
    ang = jnp.concatenate([row[:, None] * inv, col[:, None] * inv], axis=-1)
    return jnp.cos(ang), jnp.sin(ang)


def apply_rope(x, cos, sin):
    half = x.shape[-1] // 2
    x1 = x[..., :half].astype(F32)
    x2 = x[..., half:].astype(F32)
    return jnp.concatenate([x1 * cos - x2 * sin, x2 * cos + x1 * sin], axis=-1)


def over_query_blocks(fn, q):
    s, d = q.shape[-2], q.shape[-1]
    nb = s // Q_BLOCK
    qb = jnp.moveaxis(q.reshape(*q.shape[:-2], nb, Q_BLOCK, d), -3, 0)
    out = jnp.moveaxis(lax.map(fn, qb), 0, -3)
    return out.reshape(*out.shape[:-3], s, out.shape[-1])


def softmax_attend(q, k, v, scale):
    s = jnp.einsum('bhqd,bhkd->bhqk', q, k).astype(F32) * scale
    return jnp.einsum('bhqk,bhkd->bhqd', jax.nn.softmax(s, axis=-1), v)


def merge_heads(y):
    b, h, n, d = y.shape
    return y.transpose(0, 2, 1, 3).reshape(b, n, h * d)


def centred_dwconv(x, w, b):
    k, ch = w.shape
    y = lax.conv_general_dilated(x, w[:, None, :].astype(x.dtype), window_strides=(1,),
                                 padding=[(k // 2, k - 1 - k // 2)],
                                 dimension_numbers=('NWC', 'WIO', 'NWC'), feature_group_count=ch)
    return y + b


def linear_scan(a, b, h0, reverse):
    first = -1 if reverse else 0
    b = b.at[:, first].add(a[:, first] * h0)

    def combine(l, r):
        return l[0] * r[0], r[0] * l[1] + r[1]

    return lax.associative_scan(combine, (a, b), axis=1, reverse=reverse)[1]


def block_diag(x, w):
    nblk, bw, _ = w.shape
    y = jnp.einsum('bnkc,kcd->bnkd', x.reshape(*x.shape[:-1], nblk, bw), w)
    return y.reshape(x.shape)


def rglru_inputs(xc, w_a, b_a, w_x, b_x, lam):
    r = jax.nn.sigmoid((block_diag(xc, w_a) + b_a).astype(F32))
    i = jax.nn.sigmoid((block_diag(xc, w_x) + b_x).astype(F32))
    log_a = -LRU_C * r * jax.nn.softplus(-lam.astype(F32))
    return jnp.exp(log_a), jnp.sqrt(-jnp.expm1(2.0 * log_a)) * (i * xc.astype(F32))


def ssd_scan(x, da, bm, cm, h0):
    b, n, h, p = x.shape
    g, s = bm.shape[2], bm.shape[3]
    e = h // g
    nc, ln = n // SSD_CHUNK, SSD_CHUNK
    x = x.reshape(b, nc, ln, g, e, p)
    bm = bm.reshape(b, nc, ln, g, s).astype(F32)
    cm = cm.reshape(b, nc, ln, g, s).astype(F32)
    cs = jnp.cumsum(da.reshape(b, nc, ln, g, e), axis=2)
    tri = jnp.tril(jnp.ones((ln, ln), dtype=bool))[:, :, None, None]
    decay = jnp.exp(jnp.where(tri, cs[:, :, :, None] - cs[:, :, None, :], -jnp.inf))
    cb = jnp.einsum('bclgs,bcmgs->bclmg', cm, bm)
    y_diag = jnp.einsum('bclmg,bclmge,bcmgep->bclgep', cb, decay, x)
    states = jnp.einsum('bclgs,bclge,bclgep->bcgeps', bm, jnp.exp(cs[:, :, -1:] - cs), x)
    ct = jnp.cumsum(jnp.pad(cs[:, :, -1], ((0, 0), (1, 0), (0, 0), (0, 0))), axis=1)
    tri_c = jnp.tril(jnp.ones((nc + 1, nc + 1), dtype=bool))[:, :, None, None]
    decay_c = jnp.exp(jnp.where(tri_c, ct[:, :, None] - ct[:, None, :], -jnp.inf))
    all_states = jnp.concatenate([h0.reshape(b, 1, g, e, p, s), states], axis=1)
    states = jnp.einsum('bzcge,bcgeps->bzgeps', decay_c, all_states)
    y_off = jnp.einsum('bclgs,bcgeps,bclge->bclgep', cm, states[:, :-1], jnp.exp(cs))
    return (y_diag + y_off).reshape(b, n, h, p), states[:, -1].reshape(b, h, p, s)


def ssd_direction(xs, bm, cm, dt_raw, h0, a_log, dt_bias, reverse):
    dt = jax.nn.softplus(dt_raw.astype(F32) + dt_bias.astype(F32))
    da = dt * -jnp.exp(a_log.astype(F32))
    xdt = xs.astype(F32) * dt[..., None]
    if reverse:
        xdt, da, bm, cm = (jnp.flip(t, axis=1) for t in (xdt, da, bm, cm))
    y, h_final = ssd_scan(xdt, da, bm, cm, h0)
    return (jnp.flip(y, axis=1) if reverse else y), h_final


def mla_mixer(pc, pl_, rope, g_q, g_kv, w_uq, w_ukv, ctx_out):
    scale = (MLA_NOPE + MLA_ROPE) ** -0.5

    def keys_values(p, rot):
        b, n, _ = p.shape
        ckv = rms_norm(cols(p, C_MLA_CKV, MLA_KV_LORA), g_kv)
        kv = (ckv @ w_ukv).reshape(b, n, MLA_HEADS, MLA_NOPE + MLA_V).transpose(0, 2, 1, 3)
        k_rope = cols(p, C_MLA_KR, MLA_ROPE)[:, None]
        if rot is not None:
            k_rope = apply_rope(k_rope, *rot)
        k = jnp.concatenate([kv[..., :MLA_NOPE], jnp.broadcast_to(k_rope.astype(F32), (b, MLA_HEADS, n, MLA_ROPE))], axis=-1)
        return k, kv[..., MLA_NOPE:]

    def queries(p, rot):
        b, n, _ = p.shape
        cq = rms_norm(cols(p, C_MLA_CQ, MLA_Q_LORA), g_q)
        q = (cq @ w_uq).reshape(b, n, MLA_HEADS, MLA_NOPE + MLA_ROPE).transpose(0, 2, 1, 3)
        q_rope = q[..., MLA_NOPE:]
        if rot is not None:
            q_rope = apply_rope(q_rope, *rot)
        return jnp.concatenate([q[..., :MLA_NOPE], q_rope], axis=-1)

    k_c, v_c = keys_values(pc, None)
    k_l, v_l = keys_values(pl_, rope)
    k_all = jnp.concatenate([k_c, k_l], axis=2)
    v_all = jnp.concatenate([v_c, v_l], axis=2)
    y_lat = merge_heads(over_query_blocks(lambda qb: softmax_attend(qb, k_all, v_all, scale), queries(pl_, rope)))
    y_ctx = merge_heads(softmax_attend(queries(pc, None), k_c, v_c, scale)) if ctx_out else None
    return y_ctx, y_lat


def diff_mixer(pc, pl_, rope, lq1, lk1, lq2, lk2, g_sub, lambda_init, ctx_out):
    scale = DIFF_HD ** -0.5
    lam = (jnp.exp(jnp.sum(lq1 * lk1).astype(F32)) - jnp.exp(jnp.sum(lq2 * lk2).astype(F32)) + lambda_init)

    def qk(p, start, rot):
        b, n, _ = p.shape
        t = cols(p, start, 2 * DIFF_HEADS * DIFF_HD).reshape(b, n, DIFF_HEADS, 2, DIFF_HD).transpose(0, 2, 3, 1, 4)
        return t.astype(F32) if rot is None else apply_rope(t, *rot)

    def values(p):
        b, n, _ = p.shape
        return cols(p, C_DIFF_V, DIFF_HEADS * 2 * DIFF_HD).reshape(b, n, DIFF_HEADS, 2 * DIFF_HD).transpose(0, 2, 1, 3).astype(F32)

    def attend(q, k, v):
        s = jnp.einsum('bhmqd,bhmkd->bhmqk', q, k).astype(F32) * scale
        prob = jax.nn.softmax(s, axis=-1)
        a = prob[:, :, 0] - lam * prob[:, :, 1]
        o = jnp.einsum('bhqk,bhkd->bhqd', a, v)
        return rms_norm(o, g_sub) * (1.0 - lambda_init)

    k_c, v_c = qk(pc, C_DIFF_K, None), values(pc)
    k_all = jnp.concatenate([k_c, qk(pl_, C_DIFF_K, rope)], axis=3)
    v_all = jnp.concatenate([v_c, values(pl_)], axis=2)
    y_lat = merge_heads(over_query_blocks(lambda qb: attend(qb, k_all, v_all), qk(pl_, C_DIFF_Q, rope)))
    y_ctx = merge_heads(attend(qk(pc, C_DIFF_Q, None), k_c, v_c)) if ctx_out else None
    return y_ctx, y_lat


def lru_mixer(pc, pl_, conv_w, conv_b, w_a, b_a, w_x, b_x, lam, ctx_out):
    xc_c = centred_dwconv(cols(pc, C_LRU_X, LRU_WIDTH).astype(F32), conv_w, conv_b)
    xc_l = centred_dwconv(cols(pl_, C_LRU_X, LRU_WIDTH).astype(F32), conv_w, conv_b)
    h0 = jnp.zeros((pc.shape[0], LRU_WIDTH), F32)
    hs_c, hs_l = [], []
    for d in range(2):
        rev = d == 1
        a_c, in_c = rglru_inputs(xc_c, w_a[d], b_a[d], w_x[d], b_x[d], lam[d])
        h_c = linear_scan(a_c, in_c, h0, rev)
        a_l, in_l = rglru_inputs(xc_l, w_a[d], b_a[d], w_x[d], b_x[d], lam[d])
        hs_l.append(linear_scan(a_l, in_l, h_c[:, 0] if rev else h_c[:, -1], rev))
        hs_c.append(h_c)
    gate_l = jax.nn.gelu(cols(pl_, C_LRU_GATE, LRU_WIDTH).astype(F32))
    y_lat = (hs_l[0] + hs_l[1]) * gate_l
    y_ctx = None
    if ctx_out:
        gate_c = jax.nn.gelu(cols(pc, C_LRU_GATE, LRU_WIDTH).astype(F32))
        y_ctx = (hs_c[0] + hs_c[1]) * gate_c
    return y_ctx, y_lat


def ssd_mixer(pc, pl_, conv_w, conv_b, a_log, dt_bias, d_skip, g_norm, ctx_out):
    gs = SSD_GROUPS * SSD_STATE

    def split_inputs(p):
        xbc = jax.nn.silu(centred_dwconv(cols(p, C_SSD_XBC, SSD_XBC_W).astype(F32), conv_w, conv_b))
        b, n, _ = xbc.shape
        xs = xbc[..., :SSD_WIDTH].reshape(b, n, SSD_HEADS, SSD_HD)
        bm = xbc[..., SSD_WIDTH:SSD_WIDTH + gs].reshape(b, n, SSD_GROUPS, SSD_STATE)
        cm = xbc[..., SSD_WIDTH + gs:].reshape(b, n, SSD_GROUPS, SSD_STATE)
        dt_raw = cols(p, C_SSD_DT, 2 * SSD_HEADS).reshape(b, n, 2, SSD_HEADS)
        return xs, bm, cm, dt_raw

    def gated_out(p, xs, y):
        b, n = xs.shape[:2]
        y = y + xs.astype(F32) * d_skip.astype(F32)[:, None]
        yz = y.reshape(b, n, SSD_WIDTH) * jax.nn.silu(cols(p, C_SSD_Z, SSD_WIDTH).astype(F32))
        yz = rms_norm(yz.reshape(b, n, SSD_GROUPS, SSD_WIDTH // SSD_GROUPS),
                      g_norm.reshape(SSD_GROUPS, SSD_WIDTH // SSD_GROUPS))
        return yz.reshape(b, n, SSD_WIDTH)

    xs_c, bm_c, cm_c, dt_c = split_inputs(pc)
    xs_l, bm_l, cm_l, dt_l = split_inputs(pl_)
    h0 = jnp.zeros((pc.shape[0], SSD_HEADS, SSD_HD, SSD_STATE), F32)
    ys_c, ys_l = [], []
    for d in range(2):
        rev = d == 1
        y_c, h_c = ssd_direction(xs_c, bm_c, cm_c, dt_c[:, :, d], h0, a_log[d], dt_bias[d], rev)
        y_l, _ = ssd_direction(xs_l, bm_l, cm_l, dt_l[:, :, d], h_c, a_log[d], dt_bias[d], rev)
        ys_c.append(y_c)
        ys_l.append(y_l)
    y_lat = gated_out(pl_, xs_l, ys_l[0] + ys_l[1])
    y_ctx = gated_out(pc, xs_c, ys_c[0] + ys_c[1]) if ctx_out else None
    return y_ctx, y_lat


def merge_branches(p, ys, w_branch_cat, w_out):
    b, n, _ = p.shape
    d = D_MODEL
    ycat = jnp.concatenate([y.astype(BF16) for y in ys], axis=-1).reshape(b * n, -1)
    merged = 0.0
    off = 0
    for k, y in enumerate(ys):
        wk = y.shape[-1]
        proj = matmul(ycat[:, off:off + wk], w_branch_cat[k]).reshape(b, n, d)
        merged = merged + jax.nn.sigmoid(cols(p, C_GATES + k * d, d).astype(F32)) * proj
        off += wk
    return matmul(merged.reshape(b * n, d), w_out).reshape(b, n, d)


def token_mixer(u_ctx, u_lat, lp, rope_mla, rope_diff, layer_idx, ctx_out):
    w_in = lp['w_in']
    b, n, d = u_lat.shape
    nc = u_ctx.shape[1]
    p_lat = matmul(u_lat.reshape(b * n, d), w_in, BF16).reshape(b, n, -1)
    p_ctx = matmul(u_ctx.reshape(b * nc, d), w_in if ctx_out else w_in[:, :N_STATE_COLS], BF16).reshape(b, nc, -1)
    lambda_init = 0.8 - 0.6 * math.exp(-0.3 * layer_idx)
    branches = (
        mla_mixer(p_ctx, p_lat, rope_mla, lp['mla_g_q'], lp['mla_g_kv'], lp['mla_w_uq'], lp['mla_w_ukv'], ctx_out),
        diff_mixer(p_ctx, p_lat, rope_diff, lp['diff_lq1'], lp['diff_lk1'], lp['diff_lq2'], lp['diff_lk2'],
                   lp['diff_g_sub'], lambda_init, ctx_out),
        lru_mixer(p_ctx, p_lat, lp['lru_conv_w'], lp['lru_conv_b'], lp['lru_w_a'], lp['lru_b_a'],
                  lp['lru_w_x'], lp['lru_b_x'], lp['lru_lam'], ctx_out),
        ssd_mixer(p_ctx, p_lat, lp['ssd_conv_w'], lp['ssd_conv_b'], lp['ssd_a_log'], lp['ssd_dt_bias'],
                  lp['ssd_d'], lp['ssd_g_norm'], ctx_out),
    )
    y_lat = merge_branches(p_lat, [br[1] for br in branches], lp['w_branch'], lp['w_out'])
    y_ctx = merge_branches(p_ctx, [br[0] for br in branches], lp['w_branch'], lp['w_out']) if ctx_out else None
    return y_ctx, y_lat


def trunk_layer(x_ctx, x_lat, c, c_ctx, lp, rope_mla, rope_diff, layer_idx, ctx_out):
    b, n, d = x_lat.shape
    mod_lat = (jnp.dot(jax.nn.silu(c), lp['w_mod'], precision=lax.Precision.HIGHEST) + lp['b_mod'])[:, None, :]
    mod_ctx = jnp.dot(jax.nn.silu(c_ctx), lp['w_mod'], precision=lax.Precision.HIGHEST) + lp['b_mod']
    sh1, sc1, g1, sh2, sc2, g2 = jnp.split(mod_lat, 6, axis=-1)
    sh1c, sc1c, g1c, sh2c, sc2c, g2c = jnp.split(mod_ctx, 6, axis=-1)
    m_ctx, m_lat = token_mixer(modulate(x_ctx, sh1c, sc1c), modulate(x_lat, sh1, sc1), lp,
                               rope_mla, rope_diff, layer_idx, ctx_out)
    x_lat = layer_norm(DEEPNORM_ALPHA * x_lat + g1 * m_lat, lp['ln1_g'], lp['ln1_b'])
    v_lat = modulate(x_lat, sh2, sc2).reshape(b * n, d)
    moe_w = (lp['w1g'], lp['w1l'], lp['b1g'], lp['b1l'], lp['w2'], lp['b2'])
    if ctx_out:
        nc = x_ctx.shape[1]
        x_ctx = layer_norm(DEEPNORM_ALPHA * x_ctx + g1c * m_ctx, lp['ln1_g'], lp['ln1_b'])
        v_ctx = modulate(x_ctx, sh2c, sc2c).reshape(b * nc, d)
        f = moe_ffn(jnp.concatenate([v_ctx, v_lat], axis=0), lp['router_w'], lp['router_b'], *moe_w)
        f_lat = f[b * nc:].reshape(b, n, d)
        x_ctx = layer_norm(DEEPNORM_ALPHA * x_ctx + g2c * f[:b * nc].reshape(b, nc, d), lp['ln2_g'], lp['ln2_b'])
    else:
        f_lat = moe_ffn(v_lat, lp['router_w'], lp['router_b'], *moe_w).reshape(b, n, d)
        x_ctx = None
    x_lat = layer_norm(DEEPNORM_ALPHA * x_lat + g2 * f_lat, lp['ln2_g'], lp['ln2_b'])
    return x_ctx, x_lat


def kernel(x, c, ctx, c_ctx, w_mod, b_mod, w_in, mla_g_q, mla_g_kv, mla_w_uq, mla_w_ukv, diff_lq1, diff_lk1, diff_lq2, diff_lk2, diff_g_sub, lru_conv_w, lru_conv_b, lru_w_a, lru_b_a, lru_w_x, lru_b_x, lru_lam, ssd_conv_w, ssd_conv_b, ssd_a_log, ssd_dt_bias, ssd_d, ssd_g_norm, w_branch, w_out, ln1_g, ln1_b, ln2_g, ln2_b, router_w, router_b, w1, b1, w2, b2):
    n_lat = x.shape[1]
    rope_mla = axial_rope(n_lat, MLA_ROPE)
    rope_diff = axial_rope(n_lat, DIFF_HD)
    x_ctx, x_lat = ctx, x
    for i in range(DEPTH):
        lp = {
            'w_mod': w_mod[i], 'b_mod': b_mod[i], 'w_in': w_in[i],
            'mla_g_q': mla_g_q[i], 'mla_g_kv': mla_g_kv[i], 'mla_w_uq': mla_w_uq[i], 'mla_w_ukv': mla_w_ukv[i],
            'diff_lq1': diff_lq1[i], 'diff_lk1': diff_lk1[i], 'diff_lq2': diff_lq2[i], 'diff_lk2': diff_lk2[i],
            'diff_g_sub': diff_g_sub[i],
            'lru_conv_w': lru_conv_w[i], 'lru_conv_b': lru_conv_b[i], 'lru_w_a': lru_w_a[i], 'lru_b_a': lru_b_a[i],
            'lru_w_x': lru_w_x[i], 'lru_b_x': lru_b_x[i], 'lru_lam': lru_lam[i],
            'ssd_conv_w': ssd_conv_w[i], 'ssd_conv_b': ssd_conv_b[i], 'ssd_a_log': ssd_a_log[i],
            'ssd_dt_bias': ssd_dt_bias[i], 'ssd_d': ssd_d[i], 'ssd_g_norm': ssd_g_norm[i],
            'w_branch': w_branch[i], 'w_out': w_out[i],
            'ln1_g': ln1_g[i], 'ln1_b': ln1_b[i], 'ln2_g': ln2_g[i], 'ln2_b': ln2_b[i],
            'router_w': router_w[i], 'router_b': router_b[i],
            'w1g': w1[i][:, :, 0::2].astype(BF16), 'w1l': w1[i][:, :, 1::2].astype(BF16),
            'b1g': b1[i][:, None, 0::2], 'b1l': b1[i][:, None, 1::2],
            'w2': w2[i].astype(BF16), 'b2': b2[i][:, None, :],
        }
        x_ctx, x_lat = trunk_layer(x_ctx, x_lat, c, c_ctx, lp, rope_mla, rope_diff, i, i < DEPTH - 1)
    return x_lat
```

```python
import functools
import math

import jax
import jax.numpy as jnp
from jax import lax
from jax.experimental import pallas as pl
from jax.experimental.pallas import tpu as pltpu

D_MODEL = 2048
DEPTH = 2
GRID_W = 64
ROPE_BASE = 10000.0
Q_BLOCK = 128

MLA_HEADS = 4
MLA_Q_LORA = 384
MLA_KV_LORA = 256
MLA_NOPE = 128
MLA_ROPE = 64
MLA_V = 128
DIFF_HEADS = 4
DIFF_HD = 64
LRU_WIDTH = 512
LRU_C = 8.0
SSD_HEADS = 8
SSD_HD = 64
SSD_WIDTH = SSD_HEADS * SSD_HD
SSD_GROUPS = 2
SSD_STATE = 128
SSD_CHUNK = 128
SSD_XBC_W = SSD_WIDTH + 2 * SSD_GROUPS * SSD_STATE
N_BRANCH = 4
N_EXPERTS = 32
TOP_K = 4
D_EXPERT = D_MODEL
SWIGLU_LIMIT = 7.0
SWIGLU_ALPHA = 1.702
DEEPNORM_ALPHA = (2 * DEPTH) ** 0.25

C_MLA_CKV = 0
C_MLA_KR = C_MLA_CKV + MLA_KV_LORA
C_DIFF_K = C_MLA_KR + MLA_ROPE
C_DIFF_V = C_DIFF_K + 2 * DIFF_HEADS * DIFF_HD
C_LRU_X = C_DIFF_V + DIFF_HEADS * 2 * DIFF_HD
C_SSD_XBC = C_LRU_X + LRU_WIDTH
C_SSD_DT = C_SSD_XBC + SSD_XBC_W
N_STATE_COLS = C_SSD_DT + 2 * SSD_HEADS
C_MLA_CQ = N_STATE_COLS
C_DIFF_Q = C_MLA_CQ + MLA_Q_LORA
C_LRU_GATE = C_DIFF_Q + 2 * DIFF_HEADS * DIFF_HD
C_SSD_Z = C_LRU_GATE + LRU_WIDTH
C_GATES = C_SSD_Z + SSD_WIDTH
N_IN_COLS = C_GATES + N_BRANCH * D_MODEL

V7X_VMEM_LIMIT_BYTES = 48 * 1024 * 1024
LANE = 128

MOE_TM = 512
MOE_TJ = 512

BF16 = jnp.bfloat16
F32 = jnp.float32


def _mm_body(a_ref, b_ref, o_ref):
    o_ref[...] = jnp.dot(a_ref[...], b_ref[...], preferred_element_type=F32).astype(o_ref.dtype)


def _pick(n, prefs):
    for p in prefs:
        if n % p == 0:
            return p
    return n


def matmul(a, b, out_dtype=F32, tm=None, tn=None):
    m, k = a.shape
    _, n = b.shape
    a = a.astype(BF16)
    b = b.astype(BF16)
    n_pad = -n % LANE
    if n_pad:
        b = jnp.pad(b, ((0, 0), (0, n_pad)))
    np_ = n + n_pad
    tm = tm or _pick(m, (1024, 512, 256, 128, 64, 32, 16, 8))
    tn = tn or _pick(np_, (512, 384, 256, 128))
    out = pl.pallas_call(
        _mm_body,
        grid=(m // tm, np_ // tn),
        in_specs=[pl.BlockSpec((tm, k), lambda i, j: (i, 0)),
                  pl.BlockSpec((k, tn), lambda i, j: (0, j))],
        out_specs=pl.BlockSpec((tm, tn), lambda i, j: (i, j)),
        out_shape=jax.ShapeDtypeStruct((m, np_), out_dtype),
        compiler_params=pltpu.CompilerParams(
            dimension_semantics=("parallel", "parallel"),
            vmem_limit_bytes=V7X_VMEM_LIMIT_BYTES),
        name="dense_matmul",
    )(a, b)
    return out[:, :n] if n_pad else out


SPLIT_TC = 512
SPLIT_TK = 1024


def _split_body(w_ref, sel_ref, g_ref, l_ref):
    both = jnp.dot(w_ref[0].astype(BF16), sel_ref[...], preferred_element_type=F32)
    g_ref[0] = both[:, :SPLIT_TC // 2].astype(BF16)
    l_ref[0] = both[:, SPLIT_TC // 2:].astype(BF16)


def split_glu_weights(w1):
    e, d, n2 = w1.shape
    half = SPLIT_TC // 2
    src = jnp.arange(SPLIT_TC)[:, None]
    dst = jnp.arange(SPLIT_TC)[None, :]
    sel = (src == jnp.where(dst < half, 2 * dst, 2 * (dst - half) + 1)).astype(BF16)
    out = jax.ShapeDtypeStruct((e, d, n2 // 2), BF16)
    return pl.pallas_call(
        _split_body,
        grid=(e, d // SPLIT_TK, n2 // SPLIT_TC),
        in_specs=[pl.BlockSpec((1, SPLIT_TK, SPLIT_TC), lambda a, r, j: (a, r, j)),
                  pl.BlockSpec((SPLIT_TC, SPLIT_TC), lambda a, r, j: (0, 0))],
        out_specs=[pl.BlockSpec((1, SPLIT_TK, half), lambda a, r, j: (a, r, j)),
                   pl.BlockSpec((1, SPLIT_TK, half), lambda a, r, j: (a, r, j))],
        out_shape=[out, out],
        compiler_params=pltpu.CompilerParams(
            dimension_semantics=("parallel", "parallel", "parallel"),
            vmem_limit_bytes=V7X_VMEM_LIMIT_BYTES),
        name="split_glu_weights",
    )(w1, sel)


def _moe_body(be_ref, nvalid_ref, x_ref, w1g_ref, w1l_ref, b1g_ref, b1l_ref, w2_ref, b2_ref, wgt_ref, o_ref):
    i = pl.program_id(0)
    j = pl.program_id(1)
    valid = i < nvalid_ref[0]

    @pl.when(jnp.logical_not(valid))
    def _():
        @pl.when(j == 0)
        def _():
            o_ref[...] = jnp.zeros_like(o_ref)

    @pl.when(valid)
    def _():
        x = x_ref[...]
        glu = jnp.dot(x, w1g_ref[0], preferred_element_type=F32) + b1g_ref[0]
        lin = jnp.dot(x, w1l_ref[0], preferred_element_type=F32) + b1l_ref[0]
        glu = jnp.minimum(glu, SWIGLU_LIMIT)
        lin = jnp.clip(lin, -SWIGLU_LIMIT, SWIGLU_LIMIT)
        h = glu * jax.nn.sigmoid(SWIGLU_ALPHA * glu) * (lin + 1.0)
        part = jnp.dot(h.astype(BF16), w2_ref[0], preferred_element_type=F32)

        @pl.when(j == 0)
        def _():
            o_ref[...] = part

        @pl.when(j > 0)
        def _():
            o_ref[...] += part

        @pl.when(j == pl.num_programs(1) - 1)
        def _():
            o_ref[...] = (o_ref[...] + b2_ref[0]) * wgt_ref[...]


def moe_expert_blocks(x_sorted, slot_w, block_e, n_valid, w1g, w1l, b1g, b1l, w2, b2):
    n_slots, d = x_sorted.shape
    nb = n_slots // MOE_TM
    nj = D_EXPERT // MOE_TJ
    grid_spec = pltpu.PrefetchScalarGridSpec(
        num_scalar_prefetch=2,
        grid=(nb, nj),
        in_specs=[
            pl.BlockSpec((MOE_TM, d), lambda i, j, be, nv: (i, 0)),
            pl.BlockSpec((1, d, MOE_TJ), lambda i, j, be, nv: (be[i], 0, j)),
            pl.BlockSpec((1, d, MOE_TJ), lambda i, j, be, nv: (be[i], 0, j)),
            pl.BlockSpec((1, 1, MOE_TJ), lambda i, j, be, nv: (be[i], 0, j)),
            pl.BlockSpec((1, 1, MOE_TJ), lambda i, j, be, nv: (be[i], 0, j)),
            pl.BlockSpec((1, MOE_TJ, d), lambda i, j, be, nv: (be[i], j, 0)),
            pl.BlockSpec((1, 1, d), lambda i, j, be, nv: (be[i], 0, 0)),
            pl.BlockSpec((MOE_TM, 1), lambda i, j, be, nv: (i, 0)),
        ],
        out_specs=pl.BlockSpec((MOE_TM, d), lambda i, j, be, nv: (i, 0)),
    )
    return pl.pallas_call(
        _moe_body,
        grid_spec=grid_spec,
        out_shape=jax.ShapeDtypeStruct((n_slots, d), F32),
        compiler_params=pltpu.CompilerParams(
            dimension_semantics=("parallel", "arbitrary"),
            vmem_limit_bytes=V7X_VMEM_LIMIT_BYTES),
        name="moe_expert_ffn",
    )(block_e, n_valid, x_sorted, w1g, w1l, b1g, b1l, w2, b2, slot_w)


def moe_ffn(h, router_w, router_b, w1g, w1l, b1g, b1l, w2, b2):
    t, d = h.shape
    logits = jnp.dot(h, router_w, precision=lax.Precision.HIGHEST) + router_b
    top_logit, top_idx = lax.top_k(logits, TOP_K)
    top_w = jax.nn.softmax(top_logit, axis=-1)
    n_assign = t * TOP_K
    flat_e = top_idx.reshape(-1)
    order = jnp.argsort(flat_e)
    sorted_e = flat_e[order]
    sorted_tok = order // TOP_K
    sorted_w = top_w.reshape(-1)[order]
    counts = jnp.bincount(flat_e, length=N_EXPERTS)
    padded = (counts + MOE_TM - 1) // MOE_TM * MOE_TM
    start = jnp.cumsum(counts) - counts
    pend = jnp.cumsum(padded)
    pstart = pend - padded
    dest = (pstart[sorted_e] + jnp.arange(n_assign) - start[sorted_e]).astype(jnp.int32)
    n_blocks = (n_assign + N_EXPERTS * (MOE_TM - 1) + MOE_TM - 1) // MOE_TM
    n_slots = n_blocks * MOE_TM
    slot_tok = jnp.full((n_slots,), t, jnp.int32).at[dest].set(sorted_tok.astype(jnp.int32))
    slot_w = jnp.zeros((n_slots,), F32).at[dest].set(sorted_w)
    n_valid = (pend[-1] // MOE_TM).astype(jnp.int32)
    blk = jnp.arange(n_blocks)
    block_e = jnp.minimum(jnp.searchsorted(pend, jnp.minimum(blk, n_valid - 1) * MOE_TM, side='right'),
                          N_EXPERTS - 1).astype(jnp.int32)
    h_pad = jnp.concatenate([h.astype(BF16), jnp.zeros((1, d), BF16)], axis=0)
    x_sorted = h_pad[slot_tok]
    y = moe_expert_blocks(x_sorted, slot_w[:, None], block_e, n_valid.reshape(1), w1g, w1l, b1g, b1l, w2, b2)
    pos = jnp.zeros((n_assign,), jnp.int32).at[order].set(dest).reshape(t, TOP_K)
    return jnp.sum(y[pos], axis=1)


def cols(p, start, width):
    return p[..., start:start + width]


def layer_norm(x, g=None, b=None, eps=1e-6):
    xf = x.astype(F32)
    xc = xf - jnp.mean(xf, axis=-1, keepdims=True)
    y = xc * lax.rsqrt(jnp.mean(xc * xc, axis=-1, keepdims=True) + eps)
    if g is not None:
        y = y * g.astype(F32) + b.astype(F32)
    return y


def rms_norm(x, g, eps=1e-6):
    xf = x.astype(F32)
    return xf * lax.rsqrt(jnp.mean(xf * xf, axis=-1, keepdims=True) + eps) * g.astype(F32)


def modulate(x, shift, scale):
    return layer_norm(x) * (1 + scale) + shift


def axial_rope(n_tok, rot_dim):
    rows = n_tok // GRID_W
    row = jnp.repeat(jnp.arange(rows), GRID_W).astype(F32)
    col = jnp.tile(jnp.arange(GRID_W), rows).astype(F32)
    n_freq = rot_dim // 4
    inv = jnp.power(ROPE_BASE, -jnp.arange(n_freq, dtype=F32) / n_freq)
    ang = jnp.concatenate([row[:, None] * inv, col[:, None] * inv], axis=-1)
    return jnp.cos(ang), jnp.sin(ang)


def apply_rope(x, cos, sin):
    half = x.shape[-1] // 2
    x1 = x[..., :half].astype(F32)
    x2 = x[..., half:].astype(F32)
    return jnp.concatenate([x1 * cos - x2 * sin, x2 * cos + x1 * sin], axis=-1)


def over_query_blocks(fn, q):
    s, d = q.shape[-2], q.shape[-1]
    nb = s // Q_BLOCK
    qb = jnp.moveaxis(q.reshape(*q.shape[:-2], nb, Q_BLOCK, d), -3, 0)
    out = jnp.moveaxis(lax.map(fn, qb), 0, -3)
    return out.reshape(*out.shape[:-3], s, out.shape[-1])


def softmax_attend(q, k, v, scale):
    s = jnp.einsum('bhqd,bhkd->bhqk', q, k).astype(F32) * scale
    return jnp.einsum('bhqk,bhkd->bhqd', jax.nn.softmax(s, axis=-1), v)


def merge_heads(y):
    b, h, n, d = y.shape
    return y.transpose(0, 2, 1, 3).reshape(b, n, h * d)


def centred_dwconv(x, w, b):
    k, ch = w.shape
    y = lax.conv_general_dilated(x, w[:, None, :].astype(x.dtype), window_strides=(1,),
                                 padding=[(k // 2, k - 1 - k // 2)],
                                 dimension_numbers=('NWC', 'WIO', 'NWC'), feature_group_count=ch)
    return y + b


def linear_scan(a, b, h0, reverse):
    first = -1 if reverse else 0
    b = b.at[:, first].add(a[:, first] * h0)

    def combine(l, r):
        return l[0] * r[0], r[0] * l[1] + r[1]

    return lax.associative_scan(combine, (a, b), axis=1, reverse=reverse)[1]


def block_diag(x, w):
    nblk, bw, _ = w.shape
    y = jnp.einsum('bnkc,kcd->bnkd', x.reshape(*x.shape[:-1], nblk, bw), w)
    return y.reshape(x.shape)


def rglru_inputs(xc, w_a, b_a, w_x, b_x, lam):
    r = jax.nn.sigmoid((block_diag(xc, w_a) + b_a).astype(F32))
    i = jax.nn.sigmoid((block_diag(xc, w_x) + b_x).astype(F32))
    log_a = -LRU_C * r * jax.nn.softplus(-lam.astype(F32))
    return jnp.exp(log_a), jnp.sqrt(-jnp.expm1(2.0 * log_a)) * (i * xc.astype(F32))


def ssd_scan(x, da, bm, cm, h0):
    b, n, h, p = x.shape
    g, s = bm.shape[2], bm.shape[3]
    e = h // g
    nc, ln = n // SSD_CHUNK, SSD_CHUNK
    x = x.reshape(b, nc, ln, g, e, p)
    bm = bm.reshape(b, nc, ln, g, s).astype(F32)
    cm = cm.reshape(b, nc, ln, g, s).astype(F32)
    cs = jnp.cumsum(da.reshape(b, nc, ln, g, e), axis=2)
    tri = jnp.tril(jnp.ones((ln, ln), dtype=bool))[:, :, None, None]
    decay = jnp.exp(jnp.where(tri, cs[:, :, :, None] - cs[:, :, None, :], -jnp.inf))
    cb = jnp.einsum('bclgs,bcmgs->bclmg', cm, bm)
    y_diag = jnp.einsum('bclmg,bclmge,bcmgep->bclgep', cb, decay, x)
    states = jnp.einsum('bclgs,bclge,bclgep->bcgeps', bm, jnp.exp(cs[:, :, -1:] - cs), x)
    ct = jnp.cumsum(jnp.pad(cs[:, :, -1], ((0, 0), (1, 0), (0, 0), (0, 0))), axis=1)
    tri_c = jnp.tril(jnp.ones((nc + 1, nc + 1), dtype=bool))[:, :, None, None]
    decay_c = jnp.exp(jnp.where(tri_c, ct[:, :, None] - ct[:, None, :], -jnp.inf))
    all_states = jnp.concatenate([h0.reshape(b, 1, g, e, p, s), states], axis=1)
    states = jnp.einsum('bzcge,bcgeps->bzgeps', decay_c, all_states)
    y_off = jnp.einsum('bclgs,bcgeps,bclge->bclgep', cm, states[:, :-1], jnp.exp(cs))
    return (y_diag + y_off).reshape(b, n, h, p), states[:, -1].reshape(b, h, p, s)


def ssd_direction(xs, bm, cm, dt_raw, h0, a_log, dt_bias, reverse):
    dt = jax.nn.softplus(dt_raw.astype(F32) + dt_bias.astype(F32))
    da = dt * -jnp.exp(a_log.astype(F32))
    xdt = xs.astype(F32) * dt[..., None]
    if reverse:
        xdt, da, bm, cm = (jnp.flip(t, axis=1) for t in (xdt, da, bm, cm))
    y, h_final = ssd_scan(xdt, da, bm, cm, h0)
    return (jnp.flip(y, axis=1) if reverse else y), h_final


def mla_mixer(pc, pl_, rope, g_q, g_kv, w_uq, w_ukv, ctx_out):
    scale = (MLA_NOPE + MLA_ROPE) ** -0.5

    def keys_values(p, rot):
        b, n, _ = p.shape
        ckv = rms_norm(cols(p, C_MLA_CKV, MLA_KV_LORA), g_kv)
        kv = (ckv @ w_ukv).reshape(b, n, MLA_HEADS, MLA_NOPE + MLA_V).transpose(0, 2, 1, 3)
        k_rope = cols(p, C_MLA_KR, MLA_ROPE)[:, None]
        if rot is not None:
            k_rope = apply_rope(k_rope, *rot)
        k = jnp.concatenate([kv[..., :MLA_NOPE], jnp.broadcast_to(k_rope.astype(F32), (b, MLA_HEADS, n, MLA_ROPE))], axis=-1)
        return k, kv[..., MLA_NOPE:]

    def queries(p, rot):
        b, n, _ = p.shape
        cq = rms_norm(cols(p, C_MLA_CQ, MLA_Q_LORA), g_q)
        q = (cq @ w_uq).reshape(b, n, MLA_HEADS, MLA_NOPE + MLA_ROPE).transpose(0, 2, 1, 3)
        q_rope = q[..., MLA_NOPE:]
        if rot is not None:
            q_rope = apply_rope(q_rope, *rot)
        return jnp.concatenate([q[..., :MLA_NOPE], q_rope], axis=-1)

    k_c, v_c = keys_values(pc, None)
    k_l, v_l = keys_values(pl_, rope)
    k_all = jnp.concatenate([k_c, k_l], axis=2)
    v_all = jnp.concatenate([v_c, v_l], axis=2)
    y_lat = merge_heads(over_query_blocks(lambda qb: softmax_attend(qb, k_all, v_all, scale), queries(pl_, rope)))
    y_ctx = merge_heads(softmax_attend(queries(pc, None), k_c, v_c, scale)) if ctx_out else None
    return y_ctx, y_lat


def diff_mixer(pc, pl_, rope, lq1, lk1, lq2, lk2, g_sub, lambda_init, ctx_out):
    scale = DIFF_HD ** -0.5
    lam = (jnp.exp(jnp.sum(lq1 * lk1).astype(F32)) - jnp.exp(jnp.sum(lq2 * lk2).astype(F32)) + lambda_init)

    def qk(p, start, rot):
        b, n, _ = p.shape
        t = cols(p, start, 2 * DIFF_HEADS * DIFF_HD).reshape(b, n, DIFF_HEADS, 2, DIFF_HD).transpose(0, 2, 3, 1, 4)
        return t.astype(F32) if rot is None else apply_rope(t, *rot)

    def values(p):
        b, n, _ = p.shape
        return cols(p, C_DIFF_V, DIFF_HEADS * 2 * DIFF_HD).reshape(b, n, DIFF_HEADS, 2 * DIFF_HD).transpose(0, 2, 1, 3).astype(F32)

    def attend(q, k, v):
        s = jnp.einsum('bhmqd,bhmkd->bhmqk', q, k).astype(F32) * scale
        prob = jax.nn.softmax(s, axis=-1)
        a = prob[:, :, 0] - lam * prob[:, :, 1]
        o = jnp.einsum('bhqk,bhkd->bhqd', a, v)
        return rms_norm(o, g_sub) * (1.0 - lambda_init)

    k_c, v_c = qk(pc, C_DIFF_K, None), values(pc)
    k_all = jnp.concatenate([k_c, qk(pl_, C_DIFF_K, rope)], axis=3)
    v_all = jnp.concatenate([v_c, values(pl_)], axis=2)
    y_lat = merge_heads(over_query_blocks(lambda qb: attend(qb, k_all, v_all), qk(pl_, C_DIFF_Q, rope)))
    y_ctx = merge_heads(attend(qk(pc, C_DIFF_Q, None), k_c, v_c)) if ctx_out else None
    return y_ctx, y_lat


def lru_mixer(pc, pl_, conv_w, conv_b, w_a, b_a, w_x, b_x, lam, ctx_out):
    xc_c = centred_dwconv(cols(pc, C_LRU_X, LRU_WIDTH).astype(F32), conv_w, conv_b)
    xc_l = centred_dwconv(cols(pl_, C_LRU_X, LRU_WIDTH).astype(F32), conv_w, conv_b)
    h0 = jnp.zeros((pc.shape[0], LRU_WIDTH), F32)
    hs_c, hs_l = [], []
    for d in range(2):
        rev = d == 1
        a_c, in_c = rglru_inputs(xc_c, w_a[d], b_a[d], w_x[d], b_x[d], lam[d])
        h_c = linear_scan(a_c, in_c, h0, rev)
        a_l, in_l = rglru_inputs(xc_l, w_a[d], b_a[d], w_x[d], b_x[d], lam[d])
        hs_l.append(linear_scan(a_l, in_l, h_c[:, 0] if rev else h_c[:, -1], rev))
        hs_c.append(h_c)
    gate_l = jax.nn.gelu(cols(pl_, C_LRU_GATE, LRU_WIDTH).astype(F32))
    y_lat = (hs_l[0] + hs_l[1]) * gate_l
    y_ctx = None
    if ctx_out:
        gate_c = jax.nn.gelu(cols(pc, C_LRU_GATE, LRU_WIDTH).astype(F32))
        y_ctx = (hs_c[0] + hs_c[1]) * gate_c
    return y_ctx, y_lat


def ssd_mixer(pc, pl_, conv_w, conv_b, a_log, dt_bias, d_skip, g_norm, ctx_out):
    gs = SSD_GROUPS * SSD_STATE

    def split_inputs(p):
        xbc = jax.nn.silu(centred_dwconv(cols(p, C_SSD_XBC, SSD_XBC_W).astype(F32), conv_w, conv_b))
        b, n, _ = xbc.shape
        xs = xbc[..., :SSD_WIDTH].reshape(b, n, SSD_HEADS, SSD_HD)
        bm = xbc[..., SSD_WIDTH:SSD_WIDTH + gs].reshape(b, n, SSD_GROUPS, SSD_STATE)
        cm = xbc[..., SSD_WIDTH + gs:].reshape(b, n, SSD_GROUPS, SSD_STATE)
        dt_raw = cols(p, C_SSD_DT, 2 * SSD_HEADS).reshape(b, n, 2, SSD_HEADS)
        return xs, bm, cm, dt_raw

    def gated_out(p, xs, y):
        b, n = xs.shape[:2]
        y = y + xs.astype(F32) * d_skip.astype(F32)[:, None]
        yz = y.reshape(b, n, SSD_WIDTH) * jax.nn.silu(cols(p, C_SSD_Z, SSD_WIDTH).astype(F32))
        yz = rms_norm(yz.reshape(b, n, SSD_GROUPS, SSD_WIDTH // SSD_GROUPS),
                      g_norm.reshape(SSD_GROUPS, SSD_WIDTH // SSD_GROUPS))
        return yz.reshape(b, n, SSD_WIDTH)

    xs_c, bm_c, cm_c, dt_c = split_inputs(pc)
    xs_l, bm_l, cm_l, dt_l = split_inputs(pl_)
    h0 = jnp.zeros((pc.shape[0], SSD_HEADS, SSD_HD, SSD_STATE), F32)
    ys_c, ys_l = [], []
    for d in range(2):
        rev = d == 1
        y_c, h_c = ssd_direction(xs_c, bm_c, cm_c, dt_c[:, :, d], h0, a_log[d], dt_bias[d], rev)
        y_l, _ = ssd_direction(xs_l, bm_l, cm_l, dt_l[:, :, d], h_c, a_log[d], dt_bias[d], rev)
        ys_c.append(y_c)
        ys_l.append(y_l)
    y_lat = gated_out(pl_, xs_l, ys_l[0] + ys_l[1])
    y_ctx = gated_out(pc, xs_c, ys_c[0] + ys_c[1]) if ctx_out else None
    return y_ctx, y_lat


def merge_branches(p, ys, w_branch_cat, w_out):
    b, n, _ = p.shape
    d = D_MODEL
    ycat = jnp.concatenate([y.astype(BF16) for y in ys], axis=-1).reshape(b * n, -1)
    merged = 0.0
    off = 0
    for k, y in enumerate(ys):
        wk = y.shape[-1]
        proj = matmul(ycat[:, off:off + wk], w_branch_cat[k]).reshape(b, n, d)
        merged = merged + jax.nn.sigmoid(cols(p, C_GATES + k * d, d).astype(F32)) * proj
        off += wk
    return matmul(merged.reshape(b * n, d), w_out).reshape(b, n, d)


def token_mixer(u_ctx, u_lat, lp, rope_mla, rope_diff, layer_idx, ctx_out):
    w_in = lp['w_in']
    b, n, d = u_lat.shape
    nc = u_ctx.shape[1]
    p_lat = matmul(u_lat.reshape(b * n, d), w_in, BF16).reshape(b, n, -1)
    p_ctx = matmul(u_ctx.reshape(b * nc, d), w_in if ctx_out else w_in[:, :N_STATE_COLS], BF16).reshape(b, nc, -1)
    lambda_init = 0.8 - 0.6 * math.exp(-0.3 * layer_idx)
    branches = (
        mla_mixer(p_ctx, p_lat, rope_mla, lp['mla_g_q'], lp['mla_g_kv'], lp['mla_w_uq'], lp['mla_w_ukv'], ctx_out),
        diff_mixer(p_ctx, p_lat, rope_diff, lp['diff_lq1'], lp['diff_lk1'], lp['diff_lq2'], lp['diff_lk2'],
                   lp['diff_g_sub'], lambda_init, ctx_out),
        lru_mixer(p_ctx, p_lat, lp['lru_conv_w'], lp['lru_conv_b'], lp['lru_w_a'], lp['lru_b_a'],
                  lp['lru_w_x'], lp['lru_b_x'], lp['lru_lam'], ctx_out),
        ssd_mixer(p_ctx, p_lat, lp['ssd_conv_w'], lp['ssd_conv_b'], lp['ssd_a_log'], lp['ssd_dt_bias'],
                  lp['ssd_d'], lp['ssd_g_norm'], ctx_out),
    )
    y_lat = merge_branches(p_lat, [br[1] for br in branches], lp['w_branch'], lp['w_out'])
    y_ctx = merge_branches(p_ctx, [br[0] for br in branches], lp['w_branch'], lp['w_out']) if ctx_out else None
    return y_ctx, y_lat


def trunk_layer(x_ctx, x_lat, c, c_ctx, lp, rope_mla, rope_diff, layer_idx, ctx_out):
    b, n, d = x_lat.shape
    mod_lat = (jnp.dot(jax.nn.silu(c), lp['w_mod'], precision=lax.Precision.HIGHEST) + lp['b_mod'])[:, None, :]
    mod_ctx = jnp.dot(jax.nn.silu(c_ctx), lp['w_mod'], precision=lax.Precision.HIGHEST) + lp['b_mod']
    sh1, sc1, g1, sh2, sc2, g2 = jnp.split(mod_lat, 6, axis=-1)
    sh1c, sc1c, g1c, sh2c, sc2c, g2c = jnp.split(mod_ctx, 6, axis=-1)
    m_ctx, m_lat = token_mixer(modulate(x_ctx, sh1c, sc1c), modulate(x_lat, sh1, sc1), lp,
                               rope_mla, rope_diff, layer_idx, ctx_out)
    x_lat = layer_norm(DEEPNORM_ALPHA * x_lat + g1 * m_lat, lp['ln1_g'], lp['ln1_b'])
    v_lat = modulate(x_lat, sh2, sc2).reshape(b * n, d)
    moe_w = (lp['w1g'], lp['w1l'], lp['b1g'], lp['b1l'], lp['w2'], lp['b2'])
    if ctx_out:
        nc = x_ctx.shape[1]
        x_ctx = layer_norm(DEEPNORM_ALPHA * x_ctx + g1c * m_ctx, lp['ln1_g'], lp['ln1_b'])
        v_ctx = modulate(x_ctx, sh2c, sc2c).reshape(b * nc, d)
        f = moe_ffn(jnp.concatenate([v_ctx, v_lat], axis=0), lp['router_w'], lp['router_b'], *moe_w)
        f_lat = f[b * nc:].reshape(b, n, d)
        x_ctx = layer_norm(DEEPNORM_ALPHA * x_ctx + g2c * f[:b * nc].reshape(b, nc, d), lp['ln2_g'], lp['ln2_b'])
    else:
        f_lat = moe_ffn(v_lat, lp['router_w'], lp['router_b'], *moe_w).reshape(b, n, d)
        x_ctx = None
    x_lat = layer_norm(DEEPNORM_ALPHA * x_lat + g2 * f_lat, lp['ln2_g'], lp['ln2_b'])
    return x_ctx, x_lat


def kernel(x, c, ctx, c_ctx, w_mod, b_mod, w_in, mla_g_q, mla_g_kv, mla_w_uq, mla_w_ukv, diff_lq1, diff_lk1, diff_lq2, diff_lk2, diff_g_sub, lru_conv_w, lru_conv_b, lru_w_a, lru_b_a, lru_w_x, lru_b_x, lru_lam, ssd_conv_w, ssd_conv_b, ssd_a_log, ssd_dt_bias, ssd_d, ssd_g_norm, w_branch, w_out, ln1_g, ln1_b, ln2_g, ln2_b, router_w, router_b, w1, b1, w2, b2):
    n_lat = x.shape[1]
    rope_mla = axial_rope(n_lat, MLA_ROPE)
    rope_diff = axial_rope(n_lat, DIFF_HD)
    x_ctx, x_lat = ctx, x
    for i in range(DEPTH):
        lp = {
            'w_mod': w_mod[i], 'b_mod': b_mod[i], 'w_in': w_in[i],
            'mla_g_q': mla_g_q[i], 'mla_g_kv': mla_g_kv[i], 'mla_w_uq': mla_w_uq[i], 'mla_w_ukv': mla_w_ukv[i],
            'diff_lq1': diff_lq1[i], 'diff_lk1': diff_lk1[i], 'diff_lq2': diff_lq2[i], 'diff_lk2': diff_lk2[i],
            'diff_g_sub': diff_g_sub[i],
            'lru_conv_w': lru_conv_w[i], 'lru_conv_b': lru_conv_b[i], 'lru_w_a': lru_w_a[i], 'lru_b_a': lru_b_a[i],
            'lru_w_x': lru_w_x[i], 'lru_b_x': lru_b_x[i], 'lru_lam': lru_lam[i],
            'ssd_conv_w': ssd_conv_w[i], 'ssd_conv_b': ssd_conv_b[i], 'ssd_a_log': ssd_a_log[i],
            'ssd_dt_bias': ssd_dt_bias[i], 'ssd_d': ssd_d[i], 'ssd_g_norm': ssd_g_norm[i],
            'w_branch': w_branch[i], 'w_out': w_out[i],
            'ln1_g': ln1_g[i], 'ln1_b': ln1_b[i], 'ln2_g': ln2_g[i], 'ln2_b': ln2_b[i],
            'router_w': router_w[i], 'router_b': router_b[i],
            'b1g': b1[i][:, None, 0::2], 'b1l': b1[i][:, None, 1::2],
            'w2': w2[i].astype(BF16), 'b2': b2[i][:, None, :],
        }
        lp['w1g'], lp['w1l'] = split_glu_weights(w1[i])
        x_ctx, x_lat = trunk_layer(x_ctx, x_lat, c, c_ctx, lp, rope_mla, rope_diff, i, i < DEPTH - 1)
    return x_lat
```

```python
import functools
import math

import numpy as np
import jax
import jax.numpy as jnp
from jax import lax
from jax.experimental import pallas as pl
from jax.experimental.pallas import tpu as pltpu

D_MODEL = 2048
DEPTH = 2
GRID_W = 64
ROPE_BASE = 10000.0

MLA_HEADS = 4
MLA_Q_LORA = 384
MLA_KV_LORA = 256
MLA_NOPE = 128
MLA_ROPE = 64
MLA_V = 128
DIFF_HEADS = 4
DIFF_HD = 64
LRU_WIDTH = 512
LRU_BLOCKS = 4
LRU_CONV = 4
LRU_C = 8.0
SSD_HEADS = 8
SSD_HD = 64
SSD_WIDTH = SSD_HEADS * SSD_HD
SSD_GROUPS = 2
SSD_STATE = 128
SSD_CONV = 4
SSD_CHUNK = 128
SSD_XBC_W = SSD_WIDTH + 2 * SSD_GROUPS * SSD_STATE
N_BRANCH = 4
BRANCH_WIDTH = 512
N_EXPERTS = 32
TOP_K = 4
D_EXPERT = D_MODEL
SWIGLU_LIMIT = 7.0
SWIGLU_ALPHA = 1.702
DEEPNORM_ALPHA = (2 * DEPTH) ** 0.25
LN_EPS = 1e-6

C_MLA_CKV = 0
C_MLA_KR = C_MLA_CKV + MLA_KV_LORA
C_DIFF_K = C_MLA_KR + MLA_ROPE
C_DIFF_V = C_DIFF_K + 2 * DIFF_HEADS * DIFF_HD
C_LRU_X = C_DIFF_V + DIFF_HEADS * 2 * DIFF_HD
C_SSD_XBC = C_LRU_X + LRU_WIDTH
C_SSD_DT = C_SSD_XBC + SSD_XBC_W
N_STATE_COLS = C_SSD_DT + 2 * SSD_HEADS
C_MLA_CQ = N_STATE_COLS
C_DIFF_Q = C_MLA_CQ + MLA_Q_LORA
C_LRU_GATE = C_DIFF_Q + 2 * DIFF_HEADS * DIFF_HD
C_SSD_Z = C_LRU_GATE + LRU_WIDTH
C_GATES = C_SSD_Z + SSD_WIDTH

GROUP = 512
G_MLA_KV = 0
G_DIFF_K = 1
G_DIFF_KP = 2
G_DIFF_V = 3
G_SSD_XBC = 4
G_LRU_X = 6
G_MLA_CQ = 7
G_DIFF_Q = 8
G_DIFF_QP = 9
G_LRU_GATE = 10
G_SSD_Z = 11
G_GATES = 12
N_GROUPS = G_GATES + N_BRANCH * D_MODEL // GROUP
NP_COLS = N_GROUPS * GROUP

LANE = 128
V7X_VMEM_LIMIT_BYTES = 48 * 1024 * 1024

MOE_TM = 512
MOE_TJ = 512

BF16 = jnp.bfloat16
F32 = jnp.float32


def _cparams(*sem):
    return pltpu.CompilerParams(dimension_semantics=sem, vmem_limit_bytes=V7X_VMEM_LIMIT_BYTES)


def _dot(a, b):
    return jnp.dot(a, b, preferred_element_type=F32)


def _dot_nt(a, b):
    return lax.dot_general(a, b, (((1,), (1,)), ((), ())), preferred_element_type=F32)


def _dot_tn(a, b):
    return lax.dot_general(a, b, (((0,), (0,)), ((), ())), preferred_element_type=F32)


def _sigmoid(x):
    return 1.0 / (1.0 + jnp.exp(-x))


def _silu(x):
    return x * _sigmoid(x)


def _softplus(x):
    return jnp.maximum(x, 0.0) + jnp.log1p(jnp.exp(-jnp.abs(x)))


def _gelu_tanh(x):
    return 0.5 * x * (1.0 + jnp.tanh(math.sqrt(2.0 / math.pi) * (x + 0.044715 * (x * x * x))))


def _split3(x):
    hi = x.astype(BF16)
    r1 = x - hi.astype(F32)
    mid = r1.astype(BF16)
    lo = (r1 - mid.astype(F32)).astype(BF16)
    return hi, mid, lo


def _mm_body(a_ref, b_ref, bias_ref, o_ref):
    o_ref[...] = (_dot(a_ref[...], b_ref[...]) + bias_ref[...]).astype(o_ref.dtype)


def matmul_bias(a, b, bias, tn=512):
    m, k = a.shape
    _, n = b.shape
    return pl.pallas_call(
        _mm_body,
        grid=(n // tn,),
        in_specs=[pl.BlockSpec((m, k), lambda j: (0, 0)),
                  pl.BlockSpec((k, tn), lambda j: (0, j)),
                  pl.BlockSpec((1, tn), lambda j: (0, j))],
        out_specs=pl.BlockSpec((m, tn), lambda j: (0, j)),
        out_shape=jax.ShapeDtypeStruct((m, n), F32),
        compiler_params=_cparams("parallel"),
        name="mod_matmul",
    )(a.astype(BF16), b.astype(BF16), bias.reshape(1, n))


PROJ_TM = 768


def _proj_body(x_ref, shc_ref, scc_ref, shl_ref, scl_ref, w_ref, o_ref, u_ref, *, n_ctx, tm):
    t = pl.program_id(1)

    @pl.when(pl.program_id(2) == 0)
    def _():
        x = x_ref[...]
        xc = x - jnp.mean(x, axis=-1, keepdims=True)
        y = xc * lax.rsqrt(jnp.mean(xc * xc, axis=-1, keepdims=True) + LN_EPS)
        is_ctx = lax.broadcasted_iota(jnp.int32, (tm, 1), 0) + t * tm < n_ctx
        sh = jnp.where(is_ctx, shc_ref[...], shl_ref[...])
        sc = jnp.where(is_ctx, scc_ref[...], scl_ref[...])
        u_ref[...] = (y * (1.0 + sc) + sh).astype(BF16)

    o_ref[...] = _dot(u_ref[...], w_ref[...]).astype(BF16)


def ln_mod_proj(x_all, mod4, w_in_r, n_ctx):
    b, t, d = x_all.shape
    tm = PROJ_TM if t % PROJ_TM == 0 else 128
    npc = w_in_r.shape[1]
    mod_spec = lambda row_fn, k: pl.BlockSpec((None, None, 1, d), lambda bi, ti, j: (row_fn(bi), k, 0, 0))
    return pl.pallas_call(
        functools.partial(_proj_body, n_ctx=n_ctx, tm=tm),
        grid=(b, t // tm, npc // GROUP),
        in_specs=[pl.BlockSpec((None, tm, d), lambda bi, ti, j: (bi, ti, 0)),
                  mod_spec(lambda bi: b, 0), mod_spec(lambda bi: b, 1),
                  mod_spec(lambda bi: bi, 0), mod_spec(lambda bi: bi, 1),
                  pl.BlockSpec((d, GROUP), lambda bi, ti, j: (0, j))],
        out_specs=pl.BlockSpec((None, tm, GROUP), lambda bi, ti, j: (bi, ti, j)),
        out_shape=jax.ShapeDtypeStruct((b, t, npc), BF16),
        scratch_shapes=[pltpu.VMEM((tm, d), BF16)],
        compiler_params=_cparams("parallel", "parallel", "arbitrary"),
        name="ln_mod_proj",
    )(x_all, mod4, mod4, mod4, mod4, w_in_r)


def _mla_body(kvsrc_ref, cq_ref, cosk_ref, sink_ref, cosq_ref, sinq_ref, gkv_ref, gq_ref,
              wuk_ref, wuv_ref, wq_ref, wqp_ref, o_ref, k_s, v_s, *, n_ctx, tq, scale):
    qi = pl.program_id(1)
    t_all = k_s.shape[1]

    @pl.when(qi == 0)
    def _():
        src = kvsrc_ref[...].astype(F32)
        ckv = src[:, :MLA_KV_LORA]
        ckv = ckv * lax.rsqrt(jnp.mean(ckv * ckv, axis=-1, keepdims=True) + LN_EPS) * gkv_ref[...]
        ckv = ckv.astype(BF16)
        k_rope = (src[:, 256:384] * cosk_ref[...] + src[:, 384:512] * sink_ref[...]).astype(BF16)
        for h in range(MLA_HEADS):
            k_s[h, :, 0:LANE] = _dot(ckv, wuk_ref[h]).astype(BF16)
            k_s[h, :, LANE:2 * LANE] = k_rope
            v_s[h] = _dot(ckv, wuv_ref[h]).astype(BF16)

    cq = cq_ref[...].astype(F32)[:, :MLA_Q_LORA]
    cq = cq * lax.rsqrt(jnp.mean(cq * cq, axis=-1, keepdims=True) + LN_EPS) * gq_ref[...]
    cq = cq.astype(BF16)
    cosq = cosq_ref[...]
    sinq = sinq_ref[...]

    def attend(nk):
        for h in range(MLA_HEADS):
            qh = _dot(cq, wq_ref[h])
            qp = _dot(cq, wqp_ref[h])
            q_rope = qh[:, LANE:] * cosq + qp * sinq
            q = (jnp.concatenate([qh[:, :LANE], q_rope], axis=-1) * scale).astype(BF16)
            s = _dot_nt(q, k_s[h, 0:nk, :])
            p = jnp.exp(s - jnp.max(s, axis=-1, keepdims=True))
            l = jnp.sum(p, axis=-1, keepdims=True)
            o = _dot(p.astype(BF16), v_s[h, 0:nk, :])
            o_ref[:, h * MLA_V:(h + 1) * MLA_V] = (o * (1.0 / l)).astype(BF16)

    @pl.when(qi * tq < n_ctx)
    def _():
        attend(n_ctx)

    @pl.when(qi * tq >= n_ctx)
    def _():
        attend(t_all)


def mla_attn(p_all, tabs, wts, n_ctx, tq):
    b, t, _ = p_all.shape
    cosk, sink = tabs
    gkv, gq, wuk, wuv, wq, wqp = wts
    full = lambda shape: pl.BlockSpec(shape, lambda bi, qi: (0,) * len(shape))
    return pl.pallas_call(
        functools.partial(_mla_body, n_ctx=n_ctx, tq=tq, scale=(MLA_NOPE + MLA_ROPE) ** -0.5),
        grid=(b, t // tq),
        in_specs=[pl.BlockSpec((None, t, GROUP), lambda bi, qi: (bi, 0, G_MLA_KV)),
                  pl.BlockSpec((None, tq, GROUP), lambda bi, qi: (bi, qi, G_MLA_CQ)),
                  full((t, LANE)), full((t, LANE)),
                  pl.BlockSpec((tq, LANE), lambda bi, qi: (qi, 0)),
                  pl.BlockSpec((tq, LANE), lambda bi, qi: (qi, 0)),
                  full(gkv.shape), full(gq.shape), full(wuk.shape), full(wuv.shape), full(wq.shape), full(wqp.shape)],
        out_specs=pl.BlockSpec((None, tq, MLA_HEADS * MLA_V), lambda bi, qi: (bi, qi, 0)),
        out_shape=jax.ShapeDtypeStruct((b, t, MLA_HEADS * MLA_V), BF16),
        scratch_shapes=[pltpu.VMEM((MLA_HEADS, t, 2 * LANE), BF16), pltpu.VMEM((MLA_HEADS, t, MLA_V), BF16)],
        compiler_params=_cparams("parallel", "arbitrary"),
        name="mla_attn",
    )(p_all, p_all, cosk, sink, cosk, sink, gkv, gq, wuk, wuv, wq, wqp)


def _diff_body(k_ref, kp_ref, v_ref, q_ref, qp_ref, cosk_ref, sink_ref, cosq_ref, sinq_ref, lam_ref, gsub_ref,
               o_ref, k_s, *, n_ctx, tq, scale, lambda_init):
    qi = pl.program_id(1)
    t_all = k_s.shape[0]
    nrep = GROUP // LANE

    @pl.when(qi == 0)
    def _():
        cosk = jnp.concatenate([cosk_ref[...]] * nrep, axis=-1)
        sink = jnp.concatenate([sink_ref[...]] * nrep, axis=-1)
        k_s[...] = (k_ref[...].astype(F32) * cosk + kp_ref[...].astype(F32) * sink).astype(BF16)

    cosq = jnp.concatenate([cosq_ref[...]] * nrep, axis=-1)
    sinq = jnp.concatenate([sinq_ref[...]] * nrep, axis=-1)
    q = (q_ref[...].astype(F32) * cosq + qp_ref[...].astype(F32) * sinq) * scale
    lp = lam_ref[...]
    lam = (jnp.exp(jnp.sum(lp[0:1] * lp[1:2], axis=-1, keepdims=True))
           - jnp.exp(jnp.sum(lp[2:3] * lp[3:4], axis=-1, keepdims=True)) + lambda_init)
    first = lax.broadcasted_iota(jnp.int32, (1, LANE), 1) < DIFF_HD

    def softmax_pv(qm, kh, vh):
        s = _dot_nt(qm, kh)
        p = jnp.exp(s - jnp.max(s, axis=-1, keepdims=True))
        l = jnp.sum(p, axis=-1, keepdims=True)
        return _dot(p.astype(BF16), vh) * (1.0 / l)

    def attend(nk):
        for h in range(DIFF_HEADS):
            qh = q[:, h * LANE:(h + 1) * LANE]
            q1 = jnp.where(first, qh, 0.0).astype(BF16)
            q2 = jnp.where(first, 0.0, qh).astype(BF16)
            kh = k_s[0:nk, h * LANE:(h + 1) * LANE]
            vh = v_ref[0:nk, h * LANE:(h + 1) * LANE]
            o = softmax_pv(q1, kh, vh) - lam * softmax_pv(q2, kh, vh)
            o = o * lax.rsqrt(jnp.mean(o * o, axis=-1, keepdims=True) + LN_EPS) * gsub_ref[...]
            o_ref[:, h * LANE:(h + 1) * LANE] = (o * (1.0 - lambda_init)).astype(BF16)

    @pl.when(qi * tq < n_ctx)
    def _():
        attend(n_ctx)

    @pl.when(qi * tq >= n_ctx)
    def _():
        attend(t_all)


def diff_attn(p_all, tabs, lam_pack, g_sub, lambda_init, n_ctx, tq):
    b, t, _ = p_all.shape
    cosd, sind = tabs
    kv_spec = lambda g: pl.BlockSpec((None, t, GROUP), lambda bi, qi: (bi, 0, g))
    q_spec = lambda g: pl.BlockSpec((None, tq, GROUP), lambda bi, qi: (bi, qi, g))
    full = lambda shape: pl.BlockSpec(shape, lambda bi, qi: (0,) * len(shape))
    return pl.pallas_call(
        functools.partial(_diff_body, n_ctx=n_ctx, tq=tq, scale=DIFF_HD ** -0.5, lambda_init=lambda_init),
        grid=(b, t // tq),
        in_specs=[kv_spec(G_DIFF_K), kv_spec(G_DIFF_KP), kv_spec(G_DIFF_V), q_spec(G_DIFF_Q), q_spec(G_DIFF_QP),
                  full((t, LANE)), full((t, LANE)),
                  pl.BlockSpec((tq, LANE), lambda bi, qi: (qi, 0)),
                  pl.BlockSpec((tq, LANE), lambda bi, qi: (qi, 0)),
                  full(lam_pack.shape), full(g_sub.shape)],
        out_specs=pl.BlockSpec((None, tq, GROUP), lambda bi, qi: (bi, qi, 0)),
        out_shape=jax.ShapeDtypeStruct((b, t, GROUP), BF16),
        scratch_shapes=[pltpu.VMEM((t, GROUP), BF16)],
        compiler_params=_cparams("parallel", "arbitrary"),
        name="diff_attn",
    )(p_all, p_all, p_all, p_all, p_all, cosd, sind, cosd, sind, lam_pack, g_sub)


def _chunk_of_step(s, ncc, nlc, reverse):
    if not reverse:
        return s
    return jnp.where(s < ncc, ncc - 1 - s, 2 * ncc + nlc - 1 - s)


def _lru_body(*refs, ncc, nlc, tc, reverse):
    if reverse:
        (xp_ref, x_ref, xn_ref, cw_ref, cb_ref, wax_ref, ba_ref, bx_ref, lam_ref, hf_ref, gate_ref,
         o_ref, xe_s, a_s, in_s, h_s) = refs
    else:
        (xp_ref, x_ref, xn_ref, cw_ref, cb_ref, wax_ref, ba_ref, bx_ref, lam_ref,
         o_ref, xe_s, a_s, in_s, h_s) = refs
    s = pl.program_id(0)
    c = _chunk_of_step(s, ncc, nlc, reverse)
    nb = x_ref.shape[1]
    bw = LRU_WIDTH // LRU_BLOCKS

    @pl.when(s == 0)
    def _():
        h_s[...] = jnp.zeros_like(h_s)

    has_prev = jnp.logical_and(c != 0, c != ncc).astype(F32)
    has_next = jnp.logical_and(c != ncc - 1, c != ncc + nlc - 1).astype(F32)
    xe_s[0:2] = xp_ref[...].astype(F32) * has_prev
    xe_s[2:tc + 2] = x_ref[...].astype(F32)
    xe_s[tc + 2:tc + 3] = xn_ref[...].astype(F32) * has_next

    for blk in range(LRU_BLOCKS):
        ls = slice(blk * bw, (blk + 1) * bw)
        xc = cb_ref[:, ls].reshape(1, 1, bw)
        for k in range(LRU_CONV):
            xc = xc + cw_ref[k:k + 1, ls].reshape(1, 1, bw) * xe_s[k:k + tc, :, ls]
        xc2 = xc.reshape(tc * nb, bw)
        gates = _dot(xc2.astype(BF16), wax_ref[blk])
        r = _sigmoid(gates[:, :bw] + ba_ref[:, ls])
        i = _sigmoid(gates[:, bw:] + bx_ref[:, ls])
        log_a = -LRU_C * r * _softplus(-lam_ref[:, ls])
        a = jnp.exp(log_a)
        a_s[:, :, ls] = a.reshape(tc, nb, bw)
        in_s[:, :, ls] = (jnp.sqrt(1.0 - a * a) * (i * xc2)).reshape(tc, nb, bw)

    def step(j, h):
        tt = tc - 1 - j if reverse else j
        h = a_s[tt] * h + in_s[tt]
        in_s[tt] = h
        return h

    h_s[...] = lax.fori_loop(0, tc, step, h_s[...], unroll=8)

    if reverse:
        o_ref[...] = ((hf_ref[...] + in_s[...]) * _gelu_tanh(gate_ref[...].astype(F32))).astype(BF16)
    else:
        o_ref[...] = in_s[...]


def lru_direction(x_t, gate_t, h_fwd, wts, n_ctx, tc, reverse):
    t, nb, w = x_t.shape
    ncc, nlc = n_ctx // tc, (t - n_ctx) // tc
    cw, cb, wax, ba, bx, lam = wts
    cidx = lambda s: _chunk_of_step(s, ncc, nlc, reverse)
    full = lambda shape: pl.BlockSpec(shape, lambda s: (0,) * len(shape))
    chunk = pl.BlockSpec((tc, nb, w), lambda s: (cidx(s), 0, 0))
    in_specs = [pl.BlockSpec((2, nb, w), lambda s: (jnp.maximum(cidx(s) * (tc // 2) - 1, 0), 0, 0)),
                chunk,
                pl.BlockSpec((1, nb, w), lambda s: (jnp.minimum((cidx(s) + 1) * tc, t - 1), 0, 0)),
                full(cw.shape), full(cb.shape), full(wax.shape), full(ba.shape), full(bx.shape), full(lam.shape)]
    args = [x_t, x_t, x_t, cw, cb, wax, ba, bx, lam]
    if reverse:
        in_specs += [chunk, chunk]
        args += [h_fwd, gate_t]
    return pl.pallas_call(
        functools.partial(_lru_body, ncc=ncc, nlc=nlc, tc=tc, reverse=reverse),
        grid=(ncc + nlc,),
        in_specs=in_specs,
        out_specs=chunk,
        out_shape=jax.ShapeDtypeStruct((t, nb, w), BF16 if reverse else F32),
        scratch_shapes=[pltpu.VMEM((tc + 3, nb, w), F32), pltpu.VMEM((tc, nb, w), F32),
                        pltpu.VMEM((tc, nb, w), F32), pltpu.VMEM((nb, w), F32)],
        compiler_params=_cparams("arbitrary"),
        name="lru_bwd" if reverse else "lru_fwd",
    )(*args)


def _ssd_body(*refs, ncc, nlc, reverse):
    if reverse:
        (xp_ref, x_ref, xn_ref, dtc_ref, dtr_ref, cw_ref, cb_ref, arow_ref, acol_ref, brow_ref, bcol_ref,
         yf_ref, z_ref, dskip_ref, gn_ref, o_ref, xe_s, st_s) = refs
    else:
        (xp_ref, x_ref, xn_ref, dtc_ref, dtr_ref, cw_ref, cb_ref, arow_ref, acol_ref, brow_ref, bcol_ref,
         o_ref, xe_s, st_s) = refs
    s = pl.program_id(1)
    c = _chunk_of_step(s, ncc, nlc, reverse)
    ln = SSD_CHUNK
    npair = SSD_HEADS // 2
    lane0 = SSD_HEADS if reverse else 0

    @pl.when(s == 0)
    def _():
        st_s[...] = jnp.zeros_like(st_s)

    has_prev = jnp.logical_and(c != 0, c != ncc).astype(F32)
    has_next = jnp.logical_and(c != ncc - 1, c != ncc + nlc - 1).astype(F32)
    xe_s[0:8] = xp_ref[...].astype(F32)[8:16] * has_prev
    xe_s[8:ln + 8] = x_ref[...].astype(F32)
    xe_s[ln + 8:ln + 16] = xn_ref[...].astype(F32)[0:8] * has_next
    xbc = cb_ref[...]
    for k in range(SSD_CONV):
        xbc = xbc + cw_ref[k:k + 1, :] * xe_s[pl.ds(6 + k, ln), :]
    xbc = _silu(xbc)

    dt_c = _softplus(dtc_ref[...].astype(F32)[:, 3 * LANE:4 * LANE] + brow_ref[...])
    da_c = dt_c * arow_ref[...]
    dt_r = _softplus(dtr_ref[...] + bcol_ref[...])
    da_r = dt_r * acol_ref[...]
    li = lax.broadcasted_iota(jnp.int32, (ln, ln), 0)
    mi = lax.broadcasted_iota(jnp.int32, (ln, ln), 1)
    keep = (li <= mi) if reverse else (li >= mi)
    keep_bf = jnp.where(keep, 1.0, 0.0).astype(BF16)
    keep_t_bf = jnp.where((li >= mi) if reverse else (li <= mi), 1.0, 0.0).astype(BF16)
    cs_c = sum(_dot(keep_bf, part) for part in _split3(da_c))
    cs_r = sum(_dot(part, keep_t_bf) for part in _split3(da_r))
    tot_c = jnp.sum(da_c, axis=0, keepdims=True)
    e_in = jnp.exp(cs_c)
    e_out = jnp.exp(tot_c - cs_c)
    e_tot = jnp.exp(tot_c)

    lane_first = lax.broadcasted_iota(jnp.int32, (1, LANE), 1) < SSD_HD
    row_first = lax.broadcasted_iota(jnp.int32, (LANE, 1), 0) < SSD_HD

    def col(v, h):
        return jnp.broadcast_to(v[:, lane0 + h:lane0 + h + 1], (v.shape[0], LANE))

    for pr in range(npair):
        g = pr // (npair // SSD_GROUPS)
        h0, h1 = 2 * pr, 2 * pr + 1
        bm = xbc[:, SSD_WIDTH + g * SSD_STATE:SSD_WIDTH + (g + 1) * SSD_STATE].astype(BF16)
        cm = xbc[:, SSD_WIDTH + (SSD_GROUPS + g) * SSD_STATE:SSD_WIDTH + (SSD_GROUPS + g + 1) * SSD_STATE].astype(BF16)
        xs = xbc[:, pr * LANE:(pr + 1) * LANE]
        xdt = xs * jnp.where(lane_first, col(dt_c, h0), col(dt_c, h1))
        cb = _dot_nt(cm, bm)
        y = None
        for h, sel in ((h0, lane_first), (h1, jnp.logical_not(lane_first))):
            dec = jnp.exp(jnp.where(keep, col(cs_c, h) - cs_r[h:h + 1, :], -jnp.inf))
            part = _dot((cb * dec).astype(BF16), jnp.where(sel, xdt, 0.0).astype(BF16))
            y = part if y is None else y + part
        st = st_s[pr]
        y_off = _dot_nt(cm, st.astype(BF16)) * jnp.where(lane_first, col(e_in, h0), col(e_in, h1))
        xw = (xdt * jnp.where(lane_first, col(e_out, h0), col(e_out, h1))).astype(BF16)
        new = _dot_tn(xw, bm)
        dec_st = jnp.where(row_first, e_tot[:, lane0 + h0:lane0 + h0 + 1], e_tot[:, lane0 + h1:lane0 + h1 + 1])
        st_s[pr] = st * dec_st + new
        y = y + y_off
        if reverse:
            y = y + yf_ref[:, pr * LANE:(pr + 1) * LANE] + xs * dskip_ref[:, pr * LANE:(pr + 1) * LANE]
            o_ref[:, pr * LANE:(pr + 1) * LANE] = y
        else:
            o_ref[:, pr * LANE:(pr + 1) * LANE] = y

    if reverse:
        yz = o_ref[...] * _silu(z_ref[...].astype(F32))
        gw = SSD_WIDTH // SSD_GROUPS
        for g in range(SSD_GROUPS):
            blk = yz[:, g * gw:(g + 1) * gw]
            blk = blk * lax.rsqrt(jnp.mean(blk * blk, axis=-1, keepdims=True) + LN_EPS) * gn_ref[:, g * gw:(g + 1) * gw]
            o_ref[:, g * gw:(g + 1) * gw] = blk


def ssd_direction(p_all, dt_t, y_fwd, wts, n_ctx, reverse):
    b, t, _ = p_all.shape
    ln = SSD_CHUNK
    ncc, nlc = n_ctx // ln, (t - n_ctx) // ln
    cw, cb, arow, acol, brow, bcol, dskip, gn = wts
    cidx = lambda s: _chunk_of_step(s, ncc, nlc, reverse)
    full = lambda a: pl.BlockSpec(a.shape, lambda bi, s: (0,) * a.ndim)
    xw = 2 * GROUP
    gx = G_SSD_XBC // 2
    hb = ln // 16
    in_specs = [pl.BlockSpec((None, 16, xw), lambda bi, s: (bi, jnp.maximum(cidx(s) * hb - 1, 0), gx)),
                pl.BlockSpec((None, ln, xw), lambda bi, s: (bi, cidx(s), gx)),
                pl.BlockSpec((None, 16, xw), lambda bi, s: (bi, jnp.minimum((cidx(s) + 1) * hb, t // 16 - 1), gx)),
                pl.BlockSpec((None, ln, GROUP), lambda bi, s: (bi, cidx(s), G_MLA_CQ)),
                pl.BlockSpec((None, SSD_HEADS, ln), lambda bi, s: (bi, 1 if reverse else 0, cidx(s))),
                full(cw), full(cb), full(arow), full(acol), full(brow), full(bcol)]
    args = [p_all, p_all, p_all, p_all, dt_t, cw, cb, arow, acol, brow, bcol]
    out_spec = pl.BlockSpec((None, ln, SSD_WIDTH), lambda bi, s: (bi, cidx(s), 0))
    if reverse:
        in_specs += [out_spec, pl.BlockSpec((None, ln, GROUP), lambda bi, s: (bi, cidx(s), G_SSD_Z)), full(dskip), full(gn)]
        args += [y_fwd, p_all, dskip, gn]
    return pl.pallas_call(
        functools.partial(_ssd_body, ncc=ncc, nlc=nlc, reverse=reverse),
        grid=(b, ncc + nlc),
        in_specs=in_specs,
        out_specs=out_spec,
        out_shape=jax.ShapeDtypeStruct((b, t, SSD_WIDTH), F32),
        scratch_shapes=[pltpu.VMEM((ln + 16, xw), F32), pltpu.VMEM((SSD_HEADS // 2, LANE, SSD_STATE), F32)],
        compiler_params=_cparams("parallel", "arbitrary"),
        name="ssd_bwd" if reverse else "ssd_fwd",
    )(*args)


MERGE_TM = 256


def _merge_body(x_ref, y0_ref, y1_ref, y2_ref, y3_ref, gt0_ref, gt1_ref, gt2_ref, gt3_ref, wb_ref, wo_ref, g1_ref, sh2_ref, sc2_ref,
                lng_ref, lnb_ref, rw_ref, rb_ref, x1_ref, v_ref, lg_ref):
    d = D_MODEL
    merged = None
    for k, (y_ref, gt_ref) in enumerate(((y0_ref, gt0_ref), (y1_ref, gt1_ref), (y2_ref, gt2_ref), (y3_ref, gt3_ref))):
        gate = _sigmoid(gt_ref[...].astype(F32))
        term = gate * _dot(y_ref[...].astype(BF16), wb_ref[k])
        merged = term if merged is None else merged + term
    m = _dot(merged.astype(BF16), wo_ref[...])
    r = DEEPNORM_ALPHA * x_ref[...] + g1_ref[...] * m
    rc = r - jnp.mean(r, axis=-1, keepdims=True)
    x1 = rc * lax.rsqrt(jnp.mean(rc * rc, axis=-1, keepdims=True) + LN_EPS) * lng_ref[...] + lnb_ref[...]
    x1_ref[...] = x1
    xc = x1 - jnp.mean(x1, axis=-1, keepdims=True)
    v = xc * lax.rsqrt(jnp.mean(xc * xc, axis=-1, keepdims=True) + LN_EPS) * (1.0 + sc2_ref[...]) + sh2_ref[...]
    v_ref[...] = v.astype(BF16)
    vh = v.astype(BF16)
    vl = (v - vh.astype(F32)).astype(BF16)
    lg_ref[...] = _dot(vh, rw_ref[0]) + _dot(vl, rw_ref[0]) + _dot(vh, rw_ref[1]) + rb_ref[...]


def merge_ln(x_all, ys, p_all, mod4, wts, n_ctx, row0):
    b, t, d = x_all.shape
    tm = MERGE_TM if (t % MERGE_TM == 0 and n_ctx % MERGE_TM == 0) else 128
    t0 = row0 // tm
    nt = t // tm - t0
    wb, wo, lng, lnb, rw, rb = wts
    nctx_t = n_ctx // tm
    row = lambda w, dt: pl.BlockSpec((None, tm, w), lambda bi, ti: (bi, ti + t0, 0))
    mod_spec = lambda k: pl.BlockSpec((None, None, 1, d),
                                      lambda bi, ti: (jnp.where(ti + t0 < nctx_t, b, bi), k, 0, 0))
    const = lambda a: pl.BlockSpec(a.shape, lambda bi, ti: (0,) * a.ndim, pipeline_mode=pl.Buffered(1))
    out_row = lambda w: pl.BlockSpec((None, tm, w), lambda bi, ti: (bi, ti, 0))
    gate_spec = lambda k: pl.BlockSpec((None, tm, d), lambda bi, ti: (bi, ti + t0, G_GATES * GROUP // d + k))
    return pl.pallas_call(
        _merge_body,
        grid=(b, nt),
        in_specs=[row(d, F32), row(GROUP, BF16), row(GROUP, BF16), row(GROUP, BF16), row(GROUP, F32),
                  gate_spec(0), gate_spec(1), gate_spec(2), gate_spec(3),
                  const(wb), const(wo), mod_spec(2), mod_spec(3), mod_spec(4),
                  const(lng), const(lnb), const(rw), const(rb)],
        out_specs=[out_row(d), out_row(d), out_row(LANE)],
        out_shape=[jax.ShapeDtypeStruct((b, nt * tm, d), F32), jax.ShapeDtypeStruct((b, nt * tm, d), BF16),
                   jax.ShapeDtypeStruct((b, nt * tm, LANE), F32)],
        compiler_params=_cparams("parallel", "parallel"),
        name="merge_ln",
    )(x_all, *ys, p_all, p_all, p_all, p_all, wb, wo, mod4, mod4, mod4, lng, lnb, rw, rb)


def _post_body(x1_ref, f_ref, g2_ref, lng_ref, lnb_ref, o_ref):
    r = DEEPNORM_ALPHA * x1_ref[...] + g2_ref[...] * f_ref[...]
    rc = r - jnp.mean(r, axis=-1, keepdims=True)
    o_ref[...] = rc * lax.rsqrt(jnp.mean(rc * rc, axis=-1, keepdims=True) + LN_EPS) * lng_ref[...] + lnb_ref[...]


def post_ln(x1, f, mod4, lng, lnb, n_ctx_rows):
    b, t, d = x1.shape
    tm = MERGE_TM if (t % MERGE_TM == 0 and n_ctx_rows % MERGE_TM == 0) else 128
    nctx_t = n_ctx_rows // tm
    row = pl.BlockSpec((None, tm, d), lambda bi, ti: (bi, ti, 0))
    const = lambda a: pl.BlockSpec(a.shape, lambda bi, ti: (0,) * a.ndim)
    return pl.pallas_call(
        _post_body,
        grid=(b, t // tm),
        in_specs=[row, row,
                  pl.BlockSpec((None, None, 1, d), lambda bi, ti: (jnp.where(ti < nctx_t, b, bi), 5, 0, 0)),
                  const(lng), const(lnb)],
        out_specs=row,
        out_shape=jax.ShapeDtypeStruct((b, t, d), F32),
        compiler_params=_cparams("parallel", "parallel"),
        name="post_ln",
    )(x1, f, mod4, lng, lnb)


SPLIT_TC = 512
SPLIT_TK = 2048


def _split_body(w_ref, sel_ref, g_ref, l_ref):
    both = _dot(w_ref[...].astype(BF16), sel_ref[...])
    g_ref[...] = both[:, :SPLIT_TC // 2].astype(BF16)
    l_ref[...] = both[:, SPLIT_TC // 2:].astype(BF16)


def split_glu_weights(w1, layer):
    _, e, d, n2 = w1.shape
    half = SPLIT_TC // 2
    tk = min(SPLIT_TK, d)
    src = jnp.arange(SPLIT_TC)[:, None]
    dst = jnp.arange(SPLIT_TC)[None, :]
    sel = (src == jnp.where(dst < half, 2 * dst, 2 * (dst - half) + 1)).astype(BF16)
    out = jax.ShapeDtypeStruct((e, d, n2 // 2), BF16)
    return pl.pallas_call(
        _split_body,
        grid=(e, d // tk, n2 // SPLIT_TC),
        in_specs=[pl.BlockSpec((None, None, tk, SPLIT_TC), lambda a, r, j: (layer, a, r, j)),
                  pl.BlockSpec((SPLIT_TC, SPLIT_TC), lambda a, r, j: (0, 0))],
        out_specs=[pl.BlockSpec((None, tk, half), lambda a, r, j: (a, r, j)),
                   pl.BlockSpec((None, tk, half), lambda a, r, j: (a, r, j))],
        out_shape=[out, out],
        compiler_params=_cparams("parallel", "parallel", "parallel"),
        name="split_glu_weights",
    )(w1, sel)


def _cast_body(w_ref, o_ref):
    o_ref[...] = w_ref[...].astype(BF16)


def cast_expert_weights(w2, layer):
    _, e, k, n = w2.shape
    return pl.pallas_call(
        _cast_body,
        grid=(e,),
        in_specs=[pl.BlockSpec((None, None, k, n), lambda a: (layer, a, 0, 0))],
        out_specs=pl.BlockSpec((None, k, n), lambda a: (a, 0, 0)),
        out_shape=jax.ShapeDtypeStruct((e, k, n), BF16),
        compiler_params=_cparams("parallel"),
        name="cast_expert_weights",
    )(w2)


def _moe_body(be_ref, nvalid_ref, x_ref, w1g_ref, w1l_ref, b1g_ref, b1l_ref, w2_ref, b2_ref, wgt_ref, o_ref):
    i = pl.program_id(0)
    j = pl.program_id(1)
    valid = i < nvalid_ref[0]

    @pl.when(jnp.logical_not(valid))
    def _():
        @pl.when(j == 0)
        def _():
            o_ref[...] = jnp.zeros_like(o_ref)

    @pl.when(valid)
    def _():
        x = x_ref[...]
        glu = _dot(x, w1g_ref[0]) + b1g_ref[0]
        lin = _dot(x, w1l_ref[0]) + b1l_ref[0]
        glu = jnp.minimum(glu, SWIGLU_LIMIT)
        lin = jnp.clip(lin, -SWIGLU_LIMIT, SWIGLU_LIMIT)
        h = glu * _sigmoid(SWIGLU_ALPHA * glu) * (lin + 1.0)
        part = _dot(h.astype(BF16), w2_ref[0])

        @pl.when(j == 0)
        def _():
            o_ref[...] = part

        @pl.when(j > 0)
        def _():
            o_ref[...] += part

        @pl.when(j == pl.num_programs(1) - 1)
        def _():
            o_ref[...] = (o_ref[...] + b2_ref[0]) * wgt_ref[...]


def moe_expert_blocks(x_sorted, slot_w, block_e, n_valid, w1g, w1l, b1g, b1l, w2, b2):
    n_slots, d = x_sorted.shape
    nb = n_slots // MOE_TM
    nj = D_EXPERT // MOE_TJ
    grid_spec = pltpu.PrefetchScalarGridSpec(
        num_scalar_prefetch=2,
        grid=(nb, nj),
        in_specs=[
            pl.BlockSpec((MOE_TM, d), lambda i, j, be, nv: (i, 0)),
            pl.BlockSpec((1, d, MOE_TJ), lambda i, j, be, nv: (be[i], 0, j)),
            pl.BlockSpec((1, d, MOE_TJ), lambda i, j, be, nv: (be[i], 0, j)),
            pl.BlockSpec((1, 1, MOE_TJ), lambda i, j, be, nv: (be[i], 0, j)),
            pl.BlockSpec((1, 1, MOE_TJ), lambda i, j, be, nv: (be[i], 0, j)),
            pl.BlockSpec((1, MOE_TJ, d), lambda i, j, be, nv: (be[i], j, 0)),
            pl.BlockSpec((1, 1, d), lambda i, j, be, nv: (be[i], 0, 0)),
            pl.BlockSpec((MOE_TM, 1), lambda i, j, be, nv: (i, 0)),
        ],
        out_specs=pl.BlockSpec((MOE_TM, d), lambda i, j, be, nv: (i, 0)),
    )
    return pl.pallas_call(
        _moe_body,
        grid_spec=grid_spec,
        out_shape=jax.ShapeDtypeStruct((n_slots, d), F32),
        compiler_params=_cparams("parallel", "arbitrary"),
        name="moe_expert_ffn",
    )(block_e, n_valid, x_sorted, w1g, w1l, b1g, b1l, w2, b2, slot_w)


def moe_ffn(h, logits, w1g, w1l, b1g, b1l, w2, b2):
    t, d = h.shape
    top_logit, top_idx = lax.top_k(logits, TOP_K)
    top_w = jax.nn.softmax(top_logit, axis=-1)
    n_assign = t * TOP_K
    flat_e = top_idx.reshape(-1)
    order = jnp.argsort(flat_e)
    sorted_e = flat_e[order]
    sorted_tok = order // TOP_K
    sorted_w = top_w.reshape(-1)[order]
    counts = jnp.bincount(flat_e, length=N_EXPERTS)
    padded = (counts + MOE_TM - 1) // MOE_TM * MOE_TM
    start = jnp.cumsum(counts) - counts
    pend = jnp.cumsum(padded)
    pstart = pend - padded
    dest = (pstart[sorted_e] + jnp.arange(n_assign) - start[sorted_e]).astype(jnp.int32)
    n_blocks = (n_assign + N_EXPERTS * (MOE_TM - 1) + MOE_TM - 1) // MOE_TM
    n_slots = n_blocks * MOE_TM
    slot_tok = jnp.zeros((n_slots,), jnp.int32).at[dest].set(sorted_tok.astype(jnp.int32))
    slot_w = jnp.zeros((n_slots,), F32).at[dest].set(sorted_w)
    n_valid = (pend[-1] // MOE_TM).astype(jnp.int32)
    blk = jnp.arange(n_blocks)
    block_e = jnp.minimum(jnp.searchsorted(pend, jnp.minimum(blk, n_valid - 1) * MOE_TM, side='right'),
                          N_EXPERTS - 1).astype(jnp.int32)
    x_sorted = h[slot_tok]
    y = moe_expert_blocks(x_sorted, slot_w[:, None], block_e, n_valid.reshape(1), w1g, w1l, b1g, b1l, w2, b2)
    pos = jnp.zeros((n_assign,), jnp.int32).at[order].set(dest).reshape(t, TOP_K)
    return jnp.sum(y[pos], axis=1)


def _rot_partner(w, width=64):
    d, n = w.shape
    g = w.reshape(d, n // width, 2, width // 2)
    return jnp.stack([-g[:, :, 1], g[:, :, 0]], axis=2).reshape(d, n)


def _axial_angles(n_tok, rot_dim):
    rows = n_tok // GRID_W
    row = jnp.repeat(jnp.arange(rows), GRID_W).astype(F32)
    col = jnp.tile(jnp.arange(GRID_W), rows).astype(F32)
    n_freq = rot_dim // 4
    inv = jnp.power(ROPE_BASE, -jnp.arange(n_freq, dtype=F32) / n_freq)
    return jnp.concatenate([row[:, None] * inv, col[:, None] * inv], axis=-1)


def _rope_tables(n_ctx, n_lat, rot_dim, n_groups):
    ang = _axial_angles(n_lat, rot_dim)
    cos = jnp.concatenate([jnp.ones((n_ctx, rot_dim // 2), F32), jnp.cos(ang)], axis=0)
    sin = jnp.concatenate([jnp.zeros((n_ctx, rot_dim // 2), F32), jnp.sin(ang)], axis=0)
    t = n_ctx + n_lat
    pad = LANE - n_groups * rot_dim
    cos = jnp.concatenate([cos] * (2 * n_groups) + [jnp.ones((t, pad), F32)], axis=1)
    sin = jnp.concatenate([sin] * (2 * n_groups) + [jnp.zeros((t, pad), F32)], axis=1)
    return cos, sin


def _layer_weights(i, a):
    d = D_MODEL
    w_in = a['w_in'][i]
    z = lambda n: jnp.zeros((d, n), F32)
    seg = lambda c0, n: w_in[:, c0:c0 + n]
    kr = seg(C_MLA_KR, MLA_ROPE)
    groups = [
        seg(C_MLA_CKV, MLA_KV_LORA), kr, z(64), _rot_partner(kr), z(64),
        seg(C_DIFF_K, 512), _rot_partner(seg(C_DIFF_K, 512)), seg(C_DIFF_V, 512),
        seg(C_SSD_XBC, SSD_XBC_W), seg(C_LRU_X, LRU_WIDTH),
        seg(C_MLA_CQ, MLA_Q_LORA), seg(C_SSD_DT, 2 * SSD_HEADS), z(LANE - 2 * SSD_HEADS),
        seg(C_DIFF_Q, 512), _rot_partner(seg(C_DIFF_Q, 512)),
        seg(C_LRU_GATE, LRU_WIDTH), seg(C_SSD_Z, SSD_WIDTH), seg(C_GATES, N_BRANCH * d),
    ]
    w_in_r = jnp.concatenate(groups, axis=1).astype(BF16)

    w_uq = a['mla_w_uq'][i].reshape(MLA_Q_LORA, MLA_HEADS, MLA_NOPE + MLA_ROPE)
    w_ukv = a['mla_w_ukv'][i].reshape(MLA_KV_LORA, MLA_HEADS, MLA_NOPE + MLA_V)
    zq = jnp.zeros((MLA_Q_LORA, 64), F32)
    wq = jnp.stack([jnp.concatenate([w_uq[:, h, :], zq], axis=1) for h in range(MLA_HEADS)]).astype(BF16)
    wqp = jnp.stack([jnp.concatenate([_rot_partner(w_uq[:, h, MLA_NOPE:]), zq], axis=1)
                     for h in range(MLA_HEADS)]).astype(BF16)
    wuk = jnp.transpose(w_ukv[:, :, :MLA_NOPE], (1, 0, 2)).astype(BF16)
    wuv = jnp.transpose(w_ukv[:, :, MLA_NOPE:], (1, 0, 2)).astype(BF16)
    mla = (a['mla_g_kv'][i].reshape(1, -1), a['mla_g_q'][i].reshape(1, -1), wuk, wuv, wq, wqp)

    lam_pack = jnp.zeros((8, LANE), F32)
    for r, name in enumerate(('diff_lq1', 'diff_lk1', 'diff_lq2', 'diff_lk2')):
        lam_pack = lam_pack.at[r, :DIFF_HD].set(a[name][i])
    g_sub = a['diff_g_sub'][i].reshape(1, -1)

    lru = []
    for dd in range(2):
        wax = jnp.concatenate([a['lru_w_a'][i, dd], a['lru_w_x'][i, dd]], axis=-1).astype(BF16)
        lru.append((a['lru_conv_w'][i], a['lru_conv_b'][i].reshape(1, -1), wax,
                    a['lru_b_a'][i, dd].reshape(1, -1), a['lru_b_x'][i, dd].reshape(1, -1),
                    a['lru_lam'][i, dd].reshape(1, -1)))

    ssd = []
    pad_row = lambda v: jnp.zeros((1, LANE), F32).at[0, :SSD_HEADS].set(v)
    for dd in range(2):
        a_neg = -jnp.exp(a['ssd_a_log'][i, dd])
        bias = a['ssd_dt_bias'][i, dd]
        arow = jnp.zeros((1, LANE), F32).at[0, dd * SSD_HEADS:(dd + 1) * SSD_HEADS].set(a_neg)
        brow = jnp.zeros((1, LANE), F32).at[0, dd * SSD_HEADS:(dd + 1) * SSD_HEADS].set(bias)
        ssd.append((a['ssd_conv_w'][i], a['ssd_conv_b'][i].reshape(1, -1), arow, a_neg.reshape(-1, 1),
                    brow, bias.reshape(-1, 1),
                    jnp.repeat(a['ssd_d'][i], SSD_HD).reshape(1, -1), a['ssd_g_norm'][i].reshape(1, -1), dd))

    rw = jnp.pad(a['router_w'][i], ((0, 0), (0, LANE - N_EXPERTS)))
    rwh = rw.astype(BF16)
    rwl = (rw - rwh.astype(F32)).astype(BF16)
    rb = jnp.pad(a['router_b'][i], (0, LANE - N_EXPERTS)).reshape(1, -1)
    merge = (a['w_branch'][i].astype(BF16), a['w_out'][i].astype(BF16),
             a['ln1_g'][i].reshape(1, -1), a['ln1_b'][i].reshape(1, -1), jnp.stack([rwh, rwl]), rb)
    return dict(w_in_r=w_in_r, mla=mla, lam_pack=lam_pack, g_sub=g_sub, lru=lru, ssd=ssd, merge=merge)


def _token_mixers(p_all, lw, tabs_mla, tabs_diff, layer_idx, n_ctx):
    b, t, _ = p_all.shape
    tq = 256 if (t % 256 == 0 and n_ctx % 256 == 0) else 128
    y_mla = mla_attn(p_all, tabs_mla, lw['mla'], n_ctx, tq)
    lambda_init = 0.8 - 0.6 * math.exp(-0.3 * layer_idx)
    y_diff = diff_attn(p_all, tabs_diff, lw['lam_pack'], lw['g_sub'], lambda_init, n_ctx, tq)

    x_t = jnp.transpose(p_all[:, :, G_LRU_X * GROUP:(G_LRU_X + 1) * GROUP], (1, 0, 2))
    g_t = jnp.transpose(p_all[:, :, G_LRU_GATE * GROUP:(G_LRU_GATE + 1) * GROUP], (1, 0, 2))
    tc = 256 if (t % 256 == 0 and n_ctx % 256 == 0) else 128
    h_f = lru_direction(x_t, None, None, lw['lru'][0], n_ctx, tc, False)
    y_lru = jnp.transpose(lru_direction(x_t, g_t, h_f, lw['lru'][1], n_ctx, tc, True), (1, 0, 2))

    dt0 = G_MLA_CQ * GROUP + MLA_Q_LORA
    dt_t = jnp.transpose(p_all[:, :, dt0:dt0 + 2 * SSD_HEADS].astype(F32), (0, 2, 1))
    ssd_f, ssd_b = lw['ssd']
    y_f = ssd_direction(p_all, dt_t, None, ssd_f[:8], n_ctx, False)
    y_ssd = ssd_direction(p_all, dt_t, y_f, ssd_b[:8], n_ctx, True)
    return y_mla, y_diff, y_lru, y_ssd


def kernel(x, c, ctx, c_ctx, w_mod, b_mod, w_in, mla_g_q, mla_g_kv, mla_w_uq, mla_w_ukv, diff_lq1, diff_lk1, diff_lq2, diff_lk2, diff_g_sub, lru_conv_w, lru_conv_b, lru_w_a, lru_b_a, lru_w_x, lru_b_x, lru_lam, ssd_conv_w, ssd_conv_b, ssd_a_log, ssd_dt_bias, ssd_d, ssd_g_norm, w_branch, w_out, ln1_g, ln1_b, ln2_g, ln2_b, router_w, router_b, w1, b1, w2, b2):
    a = dict(w_in=w_in, mla_g_q=mla_g_q, mla_g_kv=mla_g_kv, mla_w_uq=mla_w_uq, mla_w_ukv=mla_w_ukv,
             diff_lq1=diff_lq1, diff_lk1=diff_lk1, diff_lq2=diff_lq2, diff_lk2=diff_lk2, diff_g_sub=diff_g_sub,
             lru_conv_w=lru_conv_w, lru_conv_b=lru_conv_b, lru_w_a=lru_w_a, lru_b_a=lru_b_a, lru_w_x=lru_w_x,
             lru_b_x=lru_b_x, lru_lam=lru_lam, ssd_conv_w=ssd_conv_w, ssd_conv_b=ssd_conv_b, ssd_a_log=ssd_a_log,
             ssd_dt_bias=ssd_dt_bias, ssd_d=ssd_d, ssd_g_norm=ssd_g_norm, w_branch=w_branch, w_out=w_out,
             ln1_g=ln1_g, ln1_b=ln1_b, router_w=router_w, router_b=router_b)
    bsz, n_lat, d = x.shape
    n_ctx = ctx.shape[1]
    t = n_ctx + n_lat
    depth = w_in.shape[0]
    tabs_mla = _rope_tables(n_ctx, n_lat, MLA_ROPE, 1)
    tabs_diff = _rope_tables(n_ctx, n_lat, DIFF_HD, 2)
    x_all = jnp.concatenate([ctx, x], axis=1)
    cond = jnp.zeros((16, d), F32).at[:bsz].set(jax.nn.silu(c)).at[bsz].set(jax.nn.silu(c_ctx))

    for i in range(depth):
        last = i == depth - 1
        lw = _layer_weights(i, a)
        mod4 = matmul_bias(cond, w_mod[i], b_mod[i]).reshape(16, 6, 1, d)
        p_all = ln_mod_proj(x_all, mod4, lw['w_in_r'], n_ctx)
        ys = _token_mixers(p_all, lw, tabs_mla, tabs_diff, i, n_ctx)
        row0 = n_ctx if last else 0
        x1, v, logits = merge_ln(x_all, ys, p_all, mod4, lw['merge'], n_ctx, row0)
        tr = t - row0
        w1g, w1l = split_glu_weights(w1, i)
        f = moe_ffn(v.reshape(bsz * tr, d), logits.reshape(bsz * tr, LANE)[:, :N_EXPERTS],
                    w1g, w1l, b1[i][:, None, 0::2], b1[i][:, None, 1::2],
                    cast_expert_weights(w2, i), b2[i][:, None, :])
        x_all = post_ln(x1, f.reshape(bsz, tr, d), mod4, ln2_g[i].reshape(1, -1), ln2_b[i].reshape(1, -1),
                        n_ctx - row0)
    return x_all
```

```python
import functools
import math

import numpy as np
import jax
import jax.numpy as jnp
from jax import lax
from jax.experimental import pallas as pl
from jax.experimental.pallas import tpu as pltpu

D_MODEL = 2048
DEPTH = 2
GRID_W = 64
ROPE_BASE = 10000.0

MLA_HEADS = 4
MLA_Q_LORA = 384
MLA_KV_LORA = 256
MLA_NOPE = 128
MLA_ROPE = 64
MLA_V = 128
DIFF_HEADS = 4
DIFF_HD = 64
LRU_WIDTH = 512
LRU_BLOCKS = 4
LRU_CONV = 4
LRU_C = 8.0
SSD_HEADS = 8
SSD_HD = 64
SSD_WIDTH = SSD_HEADS * SSD_HD
SSD_GROUPS = 2
SSD_STATE = 128
SSD_CONV = 4
SSD_CHUNK = 128
SSD_XBC_W = SSD_WIDTH + 2 * SSD_GROUPS * SSD_STATE
N_BRANCH = 4
BRANCH_WIDTH = 512
N_EXPERTS = 32
TOP_K = 4
D_EXPERT = D_MODEL
SWIGLU_LIMIT = 7.0
SWIGLU_ALPHA = 1.702
DEEPNORM_ALPHA = (2 * DEPTH) ** 0.25
LN_EPS = 1e-6

C_MLA_CKV = 0
C_MLA_KR = C_MLA_CKV + MLA_KV_LORA
C_DIFF_K = C_MLA_KR + MLA_ROPE
C_DIFF_V = C_DIFF_K + 2 * DIFF_HEADS * DIFF_HD
C_LRU_X = C_DIFF_V + DIFF_HEADS * 2 * DIFF_HD
C_SSD_XBC = C_LRU_X + LRU_WIDTH
C_SSD_DT = C_SSD_XBC + SSD_XBC_W
N_STATE_COLS = C_SSD_DT + 2 * SSD_HEADS
C_MLA_CQ = N_STATE_COLS
C_DIFF_Q = C_MLA_CQ + MLA_Q_LORA
C_LRU_GATE = C_DIFF_Q + 2 * DIFF_HEADS * DIFF_HD
C_SSD_Z = C_LRU_GATE + LRU_WIDTH
C_GATES = C_SSD_Z + SSD_WIDTH

GROUP = 512
G_MLA_KV = 0
G_DIFF_K = 1
G_DIFF_KP = 2
G_DIFF_V = 3
G_SSD_XBC = 4
G_LRU_X = 6
G_MLA_CQ = 7
G_DIFF_Q = 8
G_DIFF_QP = 9
G_LRU_GATE = 10
G_SSD_Z = 11
G_GATES = 12
N_GROUPS = G_GATES + N_BRANCH * D_MODEL // GROUP
NP_COLS = N_GROUPS * GROUP

LANE = 128
V7X_VMEM_LIMIT_BYTES = 48 * 1024 * 1024

MOE_TM = 512
MOE_TJ = 512

BF16 = jnp.bfloat16
F32 = jnp.float32


def _cparams(*sem):
    return pltpu.CompilerParams(dimension_semantics=sem, vmem_limit_bytes=V7X_VMEM_LIMIT_BYTES)


def _dot(a, b):
    return jnp.dot(a, b, preferred_element_type=F32)


def _dot_nt(a, b):
    return lax.dot_general(a, b, (((1,), (1,)), ((), ())), preferred_element_type=F32)


def _dot_tn(a, b):
    return lax.dot_general(a, b, (((0,), (0,)), ((), ())), preferred_element_type=F32)


def _sigmoid(x):
    return 1.0 / (1.0 + jnp.exp(-x))


def _silu(x):
    return x * _sigmoid(x)


def _softplus(x):
    return jnp.maximum(x, 0.0) + jnp.log1p(jnp.exp(-jnp.abs(x)))


def _gelu_tanh(x):
    return 0.5 * x * (1.0 + jnp.tanh(math.sqrt(2.0 / math.pi) * (x + 0.044715 * (x * x * x))))


def _split3(x):
    hi = x.astype(BF16)
    r1 = x - hi.astype(F32)
    mid = r1.astype(BF16)
    lo = (r1 - mid.astype(F32)).astype(BF16)
    return hi, mid, lo


def _mm_body(a_ref, b_ref, bias_ref, o_ref):
    o_ref[...] = (_dot(a_ref[...], b_ref[...]) + bias_ref[...]).astype(o_ref.dtype)


def matmul_bias(a, b, bias, tn=512):
    m, k = a.shape
    _, n = b.shape
    return pl.pallas_call(
        _mm_body,
        grid=(n // tn,),
        in_specs=[pl.BlockSpec((m, k), lambda j: (0, 0)),
                  pl.BlockSpec((k, tn), lambda j: (0, j)),
                  pl.BlockSpec((1, tn), lambda j: (0, j))],
        out_specs=pl.BlockSpec((m, tn), lambda j: (0, j)),
        out_shape=jax.ShapeDtypeStruct((m, n), F32),
        compiler_params=_cparams("parallel"),
        name="mod_matmul",
    )(a.astype(BF16), b.astype(BF16), bias.reshape(1, n))


PROJ_TM = 768
PROJ_TN = 1024


def _proj_body(x_ref, shc_ref, scc_ref, shl_ref, scl_ref, w_ref, o_ref, u_ref, *, n_ctx, tm):
    t = pl.program_id(1)

    @pl.when(pl.program_id(2) == 0)
    def _():
        x = x_ref[...]
        xc = x - jnp.mean(x, axis=-1, keepdims=True)
        y = xc * lax.rsqrt(jnp.mean(xc * xc, axis=-1, keepdims=True) + LN_EPS)
        is_ctx = lax.broadcasted_iota(jnp.int32, (tm, 1), 0) + t * tm < n_ctx
        sh = jnp.where(is_ctx, shc_ref[...], shl_ref[...])
        sc = jnp.where(is_ctx, scc_ref[...], scl_ref[...])
        u_ref[...] = (y * (1.0 + sc) + sh).astype(BF16)

    o_ref[...] = _dot(u_ref[...], w_ref[...]).astype(BF16)


def ln_mod_proj(x_all, mod4, w_in_r, n_ctx):
    b, t, d = x_all.shape
    tm = PROJ_TM if t % PROJ_TM == 0 else 128
    npc = w_in_r.shape[1]
    mod_spec = lambda row_fn, k: pl.BlockSpec((None, None, 1, d), lambda bi, ti, j: (row_fn(bi), k, 0, 0))
    return pl.pallas_call(
        functools.partial(_proj_body, n_ctx=n_ctx, tm=tm),
        grid=(b, t // tm, npc // PROJ_TN),
        in_specs=[pl.BlockSpec((None, tm, d), lambda bi, ti, j: (bi, ti, 0)),
                  mod_spec(lambda bi: b, 0), mod_spec(lambda bi: b, 1),
                  mod_spec(lambda bi: bi, 0), mod_spec(lambda bi: bi, 1),
                  pl.BlockSpec((d, PROJ_TN), lambda bi, ti, j: (0, j))],
        out_specs=pl.BlockSpec((None, tm, PROJ_TN), lambda bi, ti, j: (bi, ti, j)),
        out_shape=jax.ShapeDtypeStruct((b, t, npc), BF16),
        scratch_shapes=[pltpu.VMEM((tm, d), BF16)],
        compiler_params=_cparams("parallel", "parallel", "arbitrary"),
        name="ln_mod_proj",
    )(x_all, mod4, mod4, mod4, mod4, w_in_r)


def _mla_body(kvsrc_ref, cq_ref, cosk_ref, sink_ref, cosq_ref, sinq_ref, gkv_ref, gq_ref,
              wuk_ref, wuv_ref, wq_ref, wqp_ref, o_ref, k_s, v_s, *, n_ctx, tq, scale):
    qi = pl.program_id(1)
    t_all = k_s.shape[1]

    @pl.when(qi == 0)
    def _():
        src = kvsrc_ref[...].astype(F32)
        ckv = src[:, :MLA_KV_LORA]
        ckv = ckv * lax.rsqrt(jnp.mean(ckv * ckv, axis=-1, keepdims=True) + LN_EPS) * gkv_ref[...]
        ckv = ckv.astype(BF16)
        k_rope = (src[:, 256:384] * cosk_ref[...] + src[:, 384:512] * sink_ref[...]).astype(BF16)
        for h in range(MLA_HEADS):
            k_s[h, :, 0:LANE] = _dot(ckv, wuk_ref[h]).astype(BF16)
            k_s[h, :, LANE:2 * LANE] = k_rope
            v_s[h] = _dot(ckv, wuv_ref[h]).astype(BF16)

    cq = cq_ref[...].astype(F32)[:, :MLA_Q_LORA]
    cq = cq * lax.rsqrt(jnp.mean(cq * cq, axis=-1, keepdims=True) + LN_EPS) * gq_ref[...]
    cq = cq.astype(BF16)
    cosq = cosq_ref[...]
    sinq = sinq_ref[...]

    def attend(nk):
        for h in range(MLA_HEADS):
            qh = _dot(cq, wq_ref[h])
            qp = _dot(cq, wqp_ref[h])
            q_rope = qh[:, LANE:] * cosq + qp * sinq
            q = (jnp.concatenate([qh[:, :LANE], q_rope], axis=-1) * scale).astype(BF16)
            s = _dot_nt(q, k_s[h, 0:nk, :])
            p = jnp.exp(s - jnp.max(s, axis=-1, keepdims=True))
            l = jnp.sum(p, axis=-1, keepdims=True)
            o = _dot(p.astype(BF16), v_s[h, 0:nk, :])
            o_ref[:, h * MLA_V:(h + 1) * MLA_V] = (o * (1.0 / l)).astype(BF16)

    @pl.when(qi * tq < n_ctx)
    def _():
        attend(n_ctx)

    @pl.when(qi * tq >= n_ctx)
    def _():
        attend(t_all)


def mla_attn(p_all, tabs, wts, n_ctx, tq):
    b, t, _ = p_all.shape
    cosk, sink = tabs
    gkv, gq, wuk, wuv, wq, wqp = wts
    full = lambda shape: pl.BlockSpec(shape, lambda bi, qi: (0,) * len(shape))
    return pl.pallas_call(
        functools.partial(_mla_body, n_ctx=n_ctx, tq=tq, scale=(MLA_NOPE + MLA_ROPE) ** -0.5),
        grid=(b, t // tq),
        in_specs=[pl.BlockSpec((None, t, GROUP), lambda bi, qi: (bi, 0, G_MLA_KV)),
                  pl.BlockSpec((None, tq, GROUP), lambda bi, qi: (bi, qi, G_MLA_CQ)),
                  full((t, LANE)), full((t, LANE)),
                  pl.BlockSpec((tq, LANE), lambda bi, qi: (qi, 0)),
                  pl.BlockSpec((tq, LANE), lambda bi, qi: (qi, 0)),
                  full(gkv.shape), full(gq.shape), full(wuk.shape), full(wuv.shape), full(wq.shape), full(wqp.shape)],
        out_specs=pl.BlockSpec((None, tq, MLA_HEADS * MLA_V), lambda bi, qi: (bi, qi, 0)),
        out_shape=jax.ShapeDtypeStruct((b, t, MLA_HEADS * MLA_V), BF16),
        scratch_shapes=[pltpu.VMEM((MLA_HEADS, t, 2 * LANE), BF16), pltpu.VMEM((MLA_HEADS, t, MLA_V), BF16)],
        compiler_params=_cparams("parallel", "arbitrary"),
        name="mla_attn",
    )(p_all, p_all, cosk, sink, cosk, sink, gkv, gq, wuk, wuv, wq, wqp)


def _diff_body(k_ref, kp_ref, v_ref, q_ref, qp_ref, cosk_ref, sink_ref, cosq_ref, sinq_ref, lam_ref, gsub_ref,
               o_ref, k_s, *, n_ctx, tq, scale, lambda_init):
    qi = pl.program_id(1)
    t_all = k_s.shape[0]
    nrep = GROUP // LANE

    @pl.when(qi == 0)
    def _():
        cosk = jnp.concatenate([cosk_ref[...]] * nrep, axis=-1)
        sink = jnp.concatenate([sink_ref[...]] * nrep, axis=-1)
        k_s[...] = (k_ref[...].astype(F32) * cosk + kp_ref[...].astype(F32) * sink).astype(BF16)

    cosq = jnp.concatenate([cosq_ref[...]] * nrep, axis=-1)
    sinq = jnp.concatenate([sinq_ref[...]] * nrep, axis=-1)
    q = (q_ref[...].astype(F32) * cosq + qp_ref[...].astype(F32) * sinq) * scale
    lp = lam_ref[...]
    lam = (jnp.exp(jnp.sum(lp[0:1] * lp[1:2], axis=-1, keepdims=True))
           - jnp.exp(jnp.sum(lp[2:3] * lp[3:4], axis=-1, keepdims=True)) + lambda_init)
    first = lax.broadcasted_iota(jnp.int32, (1, LANE), 1) < DIFF_HD

    def softmax_pv(qm, kh, vh):
        s = _dot_nt(qm, kh)
        p = jnp.exp(s - jnp.max(s, axis=-1, keepdims=True))
        l = jnp.sum(p, axis=-1, keepdims=True)
        return _dot(p.astype(BF16), vh) * (1.0 / l)

    def attend(nk):
        for h in range(DIFF_HEADS):
            qh = q[:, h * LANE:(h + 1) * LANE]
            q1 = jnp.where(first, qh, 0.0).astype(BF16)
            q2 = jnp.where(first, 0.0, qh).astype(BF16)
            kh = k_s[0:nk, h * LANE:(h + 1) * LANE]
            vh = v_ref[0:nk, h * LANE:(h + 1) * LANE]
            o = softmax_pv(q1, kh, vh) - lam * softmax_pv(q2, kh, vh)
            o = o * lax.rsqrt(jnp.mean(o * o, axis=-1, keepdims=True) + LN_EPS) * gsub_ref[...]
            o_ref[:, h * LANE:(h + 1) * LANE] = (o * (1.0 - lambda_init)).astype(BF16)

    @pl.when(qi * tq < n_ctx)
    def _():
        attend(n_ctx)

    @pl.when(qi * tq >= n_ctx)
    def _():
        attend(t_all)


def diff_attn(p_all, tabs, lam_pack, g_sub, lambda_init, n_ctx, tq):
    b, t, _ = p_all.shape
    cosd, sind = tabs
    kv_spec = lambda g: pl.BlockSpec((None, t, GROUP), lambda bi, qi: (bi, 0, g))
    q_spec = lambda g: pl.BlockSpec((None, tq, GROUP), lambda bi, qi: (bi, qi, g))
    full = lambda shape: pl.BlockSpec(shape, lambda bi, qi: (0,) * len(shape))
    return pl.pallas_call(
        functools.partial(_diff_body, n_ctx=n_ctx, tq=tq, scale=DIFF_HD ** -0.5, lambda_init=lambda_init),
        grid=(b, t // tq),
        in_specs=[kv_spec(G_DIFF_K), kv_spec(G_DIFF_KP), kv_spec(G_DIFF_V), q_spec(G_DIFF_Q), q_spec(G_DIFF_QP),
                  full((t, LANE)), full((t, LANE)),
                  pl.BlockSpec((tq, LANE), lambda bi, qi: (qi, 0)),
                  pl.BlockSpec((tq, LANE), lambda bi, qi: (qi, 0)),
                  full(lam_pack.shape), full(g_sub.shape)],
        out_specs=pl.BlockSpec((None, tq, GROUP), lambda bi, qi: (bi, qi, 0)),
        out_shape=jax.ShapeDtypeStruct((b, t, GROUP), BF16),
        scratch_shapes=[pltpu.VMEM((t, GROUP), BF16)],
        compiler_params=_cparams("parallel", "arbitrary"),
        name="diff_attn",
    )(p_all, p_all, p_all, p_all, p_all, cosd, sind, cosd, sind, lam_pack, g_sub)


def _chunk_of_step(s, ncc, nlc, reverse):
    if not reverse:
        return s
    return jnp.where(s < ncc, ncc - 1 - s, 2 * ncc + nlc - 1 - s)


def _lru_body(*refs, ncc, nlc, tc, reverse):
    if reverse:
        (xp_ref, x_ref, xn_ref, cw_ref, cb_ref, wax_ref, ba_ref, bx_ref, lam_ref, hf_ref, gate_ref,
         o_ref, xe_s, a_s, in_s, h_s) = refs
    else:
        (xp_ref, x_ref, xn_ref, cw_ref, cb_ref, wax_ref, ba_ref, bx_ref, lam_ref,
         o_ref, xe_s, a_s, in_s, h_s) = refs
    s = pl.program_id(0)
    c = _chunk_of_step(s, ncc, nlc, reverse)
    nb = x_ref.shape[1]
    bw = LRU_WIDTH // LRU_BLOCKS

    @pl.when(s == 0)
    def _():
        h_s[...] = jnp.zeros_like(h_s)

    has_prev = jnp.logical_and(c != 0, c != ncc).astype(F32)
    has_next = jnp.logical_and(c != ncc - 1, c != ncc + nlc - 1).astype(F32)
    xe_s[0:2] = xp_ref[...].astype(F32) * has_prev
    xe_s[2:tc + 2] = x_ref[...].astype(F32)
    xe_s[tc + 2:tc + 3] = xn_ref[...].astype(F32) * has_next

    for blk in range(LRU_BLOCKS):
        ls = slice(blk * bw, (blk + 1) * bw)
        xc = cb_ref[:, ls].reshape(1, 1, bw)
        for k in range(LRU_CONV):
            xc = xc + cw_ref[k:k + 1, ls].reshape(1, 1, bw) * xe_s[k:k + tc, :, ls]
        xc2 = xc.reshape(tc * nb, bw)
        gates = _dot(xc2.astype(BF16), wax_ref[blk])
        r = _sigmoid(gates[:, :bw] + ba_ref[:, ls])
        i = _sigmoid(gates[:, bw:] + bx_ref[:, ls])
        log_a = -LRU_C * r * _softplus(-lam_ref[:, ls])
        a = jnp.exp(log_a)
        a_s[:, :, ls] = a.reshape(tc, nb, bw)
        in_s[:, :, ls] = (jnp.sqrt(1.0 - a * a) * (i * xc2)).reshape(tc, nb, bw)

    def step(j, h):
        tt = tc - 1 - j if reverse else j
        h = a_s[tt] * h + in_s[tt]
        in_s[tt] = h
        return h

    h_s[...] = lax.fori_loop(0, tc, step, h_s[...], unroll=8)

    if reverse:
        o_ref[...] = ((hf_ref[...] + in_s[...]) * _gelu_tanh(gate_ref[...].astype(F32))).astype(BF16)
    else:
        o_ref[...] = in_s[...]


def lru_direction(x_t, gate_t, h_fwd, wts, n_ctx, tc, reverse):
    t, nb, w = x_t.shape
    ncc, nlc = n_ctx // tc, (t - n_ctx) // tc
    cw, cb, wax, ba, bx, lam = wts
    cidx = lambda s: _chunk_of_step(s, ncc, nlc, reverse)
    full = lambda shape: pl.BlockSpec(shape, lambda s: (0,) * len(shape))
    chunk = pl.BlockSpec((tc, nb, w), lambda s: (cidx(s), 0, 0))
    in_specs = [pl.BlockSpec((2, nb, w), lambda s: (jnp.maximum(cidx(s) * (tc // 2) - 1, 0), 0, 0)),
                chunk,
                pl.BlockSpec((1, nb, w), lambda s: (jnp.minimum((cidx(s) + 1) * tc, t - 1), 0, 0)),
                full(cw.shape), full(cb.shape), full(wax.shape), full(ba.shape), full(bx.shape), full(lam.shape)]
    args = [x_t, x_t, x_t, cw, cb, wax, ba, bx, lam]
    if reverse:
        in_specs += [chunk, chunk]
        args += [h_fwd, gate_t]
    return pl.pallas_call(
        functools.partial(_lru_body, ncc=ncc, nlc=nlc, tc=tc, reverse=reverse),
        grid=(ncc + nlc,),
        in_specs=in_specs,
        out_specs=chunk,
        out_shape=jax.ShapeDtypeStruct((t, nb, w), BF16 if reverse else F32),
        scratch_shapes=[pltpu.VMEM((tc + 3, nb, w), F32), pltpu.VMEM((tc, nb, w), F32),
                        pltpu.VMEM((tc, nb, w), F32), pltpu.VMEM((nb, w), F32)],
        compiler_params=_cparams("arbitrary"),
        name="lru_bwd" if reverse else "lru_fwd",
    )(*args)


def _ssd_body(*refs, ncc, nlc, reverse):
    if reverse:
        (xp_ref, x_ref, xn_ref, dtc_ref, dtr_ref, cw_ref, cb_ref, arow_ref, acol_ref, brow_ref, bcol_ref,
         yf_ref, z_ref, dskip_ref, gn_ref, o_ref, xe_s, st_s) = refs
    else:
        (xp_ref, x_ref, xn_ref, dtc_ref, dtr_ref, cw_ref, cb_ref, arow_ref, acol_ref, brow_ref, bcol_ref,
         o_ref, xe_s, st_s) = refs
    s = pl.program_id(1)
    c = _chunk_of_step(s, ncc, nlc, reverse)
    ln = SSD_CHUNK
    npair = SSD_HEADS // 2
    lane0 = SSD_HEADS if reverse else 0

    @pl.when(s == 0)
    def _():
        st_s[...] = jnp.zeros_like(st_s)

    has_prev = jnp.logical_and(c != 0, c != ncc).astype(F32)
    has_next = jnp.logical_and(c != ncc - 1, c != ncc + nlc - 1).astype(F32)
    xe_s[0:8] = xp_ref[...].astype(F32)[8:16] * has_prev
    xe_s[8:ln + 8] = x_ref[...].astype(F32)
    xe_s[ln + 8:ln + 16] = xn_ref[...].astype(F32)[0:8] * has_next
    xbc = cb_ref[...]
    for k in range(SSD_CONV):
        xbc = xbc + cw_ref[k:k + 1, :] * xe_s[pl.ds(6 + k, ln), :]
    xbc = _silu(xbc)

    dt_c = _softplus(dtc_ref[...].astype(F32)[:, 3 * LANE:4 * LANE] + brow_ref[...])
    da_c = dt_c * arow_ref[...]
    dt_r = _softplus(dtr_ref[...] + bcol_ref[...])
    da_r = dt_r * acol_ref[...]
    li = lax.broadcasted_iota(jnp.int32, (ln, ln), 0)
    mi = lax.broadcasted_iota(jnp.int32, (ln, ln), 1)
    keep = (li <= mi) if reverse else (li >= mi)
    keep_bf = jnp.where(keep, 1.0, 0.0).astype(BF16)
    keep_t_bf = jnp.where((li >= mi) if reverse else (li <= mi), 1.0, 0.0).astype(BF16)
    cs_c = sum(_dot(keep_bf, part) for part in _split3(da_c))
    cs_r = sum(_dot(part, keep_t_bf) for part in _split3(da_r))
    tot_c = jnp.sum(da_c, axis=0, keepdims=True)
    e_in = jnp.exp(cs_c)
    e_out = jnp.exp(tot_c - cs_c)
    e_tot = jnp.exp(tot_c)

    lane_first = lax.broadcasted_iota(jnp.int32, (1, LANE), 1) < SSD_HD
    row_first = lax.broadcasted_iota(jnp.int32, (LANE, 1), 0) < SSD_HD

    def col(v, h):
        return jnp.broadcast_to(v[:, lane0 + h:lane0 + h + 1], (v.shape[0], LANE))

    for pr in range(npair):
        g = pr // (npair // SSD_GROUPS)
        h0, h1 = 2 * pr, 2 * pr + 1
        bm = xbc[:, SSD_WIDTH + g * SSD_STATE:SSD_WIDTH + (g + 1) * SSD_STATE].astype(BF16)
        cm = xbc[:, SSD_WIDTH + (SSD_GROUPS + g) * SSD_STATE:SSD_WIDTH + (SSD_GROUPS + g + 1) * SSD_STATE].astype(BF16)
        xs = xbc[:, pr * LANE:(pr + 1) * LANE]
        xdt = xs * jnp.where(lane_first, col(dt_c, h0), col(dt_c, h1))
        cb = _dot_nt(cm, bm)
        y = None
        for h, sel in ((h0, lane_first), (h1, jnp.logical_not(lane_first))):
            dec = jnp.exp(jnp.where(keep, col(cs_c, h) - cs_r[h:h + 1, :], -jnp.inf))
            part = _dot((cb * dec).astype(BF16), jnp.where(sel, xdt, 0.0).astype(BF16))
            y = part if y is None else y + part
        st = st_s[pr]
        y_off = _dot_nt(cm, st.astype(BF16)) * jnp.where(lane_first, col(e_in, h0), col(e_in, h1))
        xw = (xdt * jnp.where(lane_first, col(e_out, h0), col(e_out, h1))).astype(BF16)
        new = _dot_tn(xw, bm)
        dec_st = jnp.where(row_first, e_tot[:, lane0 + h0:lane0 + h0 + 1], e_tot[:, lane0 + h1:lane0 + h1 + 1])
        st_s[pr] = st * dec_st + new
        y = y + y_off
        if reverse:
            y = y + yf_ref[:, pr * LANE:(pr + 1) * LANE] + xs * dskip_ref[:, pr * LANE:(pr + 1) * LANE]
            o_ref[:, pr * LANE:(pr + 1) * LANE] = y
        else:
            o_ref[:, pr * LANE:(pr + 1) * LANE] = y

    if reverse:
        yz = o_ref[...] * _silu(z_ref[...].astype(F32))
        gw = SSD_WIDTH // SSD_GROUPS
        for g in range(SSD_GROUPS):
            blk = yz[:, g * gw:(g + 1) * gw]
            blk = blk * lax.rsqrt(jnp.mean(blk * blk, axis=-1, keepdims=True) + LN_EPS) * gn_ref[:, g * gw:(g + 1) * gw]
            o_ref[:, g * gw:(g + 1) * gw] = blk


def ssd_direction(p_all, dt_t, y_fwd, wts, n_ctx, reverse):
    b, t, _ = p_all.shape
    ln = SSD_CHUNK
    ncc, nlc = n_ctx // ln, (t - n_ctx) // ln
    cw, cb, arow, acol, brow, bcol, dskip, gn = wts
    cidx = lambda s: _chunk_of_step(s, ncc, nlc, reverse)
    full = lambda a: pl.BlockSpec(a.shape, lambda bi, s: (0,) * a.ndim)
    xw = 2 * GROUP
    gx = G_SSD_XBC // 2
    hb = ln // 16
    in_specs = [pl.BlockSpec((None, 16, xw), lambda bi, s: (bi, jnp.maximum(cidx(s) * hb - 1, 0), gx)),
                pl.BlockSpec((None, ln, xw), lambda bi, s: (bi, cidx(s), gx)),
                pl.BlockSpec((None, 16, xw), lambda bi, s: (bi, jnp.minimum((cidx(s) + 1) * hb, t // 16 - 1), gx)),
                pl.BlockSpec((None, ln, GROUP), lambda bi, s: (bi, cidx(s), G_MLA_CQ)),
                pl.BlockSpec((None, SSD_HEADS, ln), lambda bi, s: (bi, 1 if reverse else 0, cidx(s))),
                full(cw), full(cb), full(arow), full(acol), full(brow), full(bcol)]
    args = [p_all, p_all, p_all, p_all, dt_t, cw, cb, arow, acol, brow, bcol]
    out_spec = pl.BlockSpec((None, ln, SSD_WIDTH), lambda bi, s: (bi, cidx(s), 0))
    if reverse:
        in_specs += [out_spec, pl.BlockSpec((None, ln, GROUP), lambda bi, s: (bi, cidx(s), G_SSD_Z)), full(dskip), full(gn)]
        args += [y_fwd, p_all, dskip, gn]
    return pl.pallas_call(
        functools.partial(_ssd_body, ncc=ncc, nlc=nlc, reverse=reverse),
        grid=(b, ncc + nlc),
        in_specs=in_specs,
        out_specs=out_spec,
        out_shape=jax.ShapeDtypeStruct((b, t, SSD_WIDTH), F32),
        scratch_shapes=[pltpu.VMEM((ln + 16, xw), F32), pltpu.VMEM((SSD_HEADS // 2, LANE, SSD_STATE), F32)],
        compiler_params=_cparams("parallel", "arbitrary"),
        name="ssd_bwd" if reverse else "ssd_fwd",
    )(*args)


MERGE_TM = 256


def _merge_body(x_ref, y0_ref, y1_ref, y2_ref, y3_ref, gt0_ref, gt1_ref, gt2_ref, gt3_ref, wb_ref, wo_ref, g1_ref, sh2_ref, sc2_ref,
                lng_ref, lnb_ref, rw_ref, rb_ref, x1_ref, v_ref, lg_ref):
    d = D_MODEL
    merged = None
    for k, (y_ref, gt_ref) in enumerate(((y0_ref, gt0_ref), (y1_ref, gt1_ref), (y2_ref, gt2_ref), (y3_ref, gt3_ref))):
        gate = _sigmoid(gt_ref[...].astype(F32))
        term = gate * _dot(y_ref[...].astype(BF16), wb_ref[k])
        merged = term if merged is None else merged + term
    m = _dot(merged.astype(BF16), wo_ref[...])
    r = DEEPNORM_ALPHA * x_ref[...] + g1_ref[...] * m
    rc = r - jnp.mean(r, axis=-1, keepdims=True)
    x1 = rc * lax.rsqrt(jnp.mean(rc * rc, axis=-1, keepdims=True) + LN_EPS) * lng_ref[...] + lnb_ref[...]
    x1_ref[...] = x1
    xc = x1 - jnp.mean(x1, axis=-1, keepdims=True)
    v = xc * lax.rsqrt(jnp.mean(xc * xc, axis=-1, keepdims=True) + LN_EPS) * (1.0 + sc2_ref[...]) + sh2_ref[...]
    v_ref[...] = v.astype(BF16)
    vh = v.astype(BF16)
    vl = (v - vh.astype(F32)).astype(BF16)
    lg_ref[...] = _dot(vh, rw_ref[0]) + _dot(vl, rw_ref[0]) + _dot(vh, rw_ref[1]) + rb_ref[...]


def merge_ln(x_all, ys, p_all, mod4, wts, n_ctx, row0):
    b, t, d = x_all.shape
    tm = MERGE_TM if (t % MERGE_TM == 0 and n_ctx % MERGE_TM == 0) else 128
    t0 = row0 // tm
    nt = t // tm - t0
    wb, wo, lng, lnb, rw, rb = wts
    nctx_t = n_ctx // tm
    row = lambda w, dt: pl.BlockSpec((None, tm, w), lambda bi, ti: (bi, ti + t0, 0))
    mod_spec = lambda k: pl.BlockSpec((None, None, 1, d),
                                      lambda bi, ti: (jnp.where(ti + t0 < nctx_t, b, bi), k, 0, 0))
    const = lambda a: pl.BlockSpec(a.shape, lambda bi, ti: (0,) * a.ndim, pipeline_mode=pl.Buffered(1))
    out_row = lambda w: pl.BlockSpec((None, tm, w), lambda bi, ti: (bi, ti, 0))
    gate_spec = lambda k: pl.BlockSpec((None, tm, d), lambda bi, ti: (bi, ti + t0, G_GATES * GROUP // d + k))
    return pl.pallas_call(
        _merge_body,
        grid=(b, nt),
        in_specs=[row(d, F32), row(GROUP, BF16), row(GROUP, BF16), row(GROUP, BF16), row(GROUP, F32),
                  gate_spec(0), gate_spec(1), gate_spec(2), gate_spec(3),
                  const(wb), const(wo), mod_spec(2), mod_spec(3), mod_spec(4),
                  const(lng), const(lnb), const(rw), const(rb)],
        out_specs=[out_row(d), out_row(d), out_row(LANE)],
        out_shape=[jax.ShapeDtypeStruct((b, nt * tm, d), F32), jax.ShapeDtypeStruct((b, nt * tm, d), BF16),
                   jax.ShapeDtypeStruct((b, nt * tm, LANE), F32)],
        compiler_params=_cparams("parallel", "parallel"),
        name="merge_ln",
    )(x_all, *ys, p_all, p_all, p_all, p_all, wb, wo, mod4, mod4, mod4, lng, lnb, rw, rb)


def _post_body(x1_ref, f_ref, g2_ref, lng_ref, lnb_ref, o_ref):
    r = DEEPNORM_ALPHA * x1_ref[...] + g2_ref[...] * f_ref[...]
    rc = r - jnp.mean(r, axis=-1, keepdims=True)
    o_ref[...] = rc * lax.rsqrt(jnp.mean(rc * rc, axis=-1, keepdims=True) + LN_EPS) * lng_ref[...] + lnb_ref[...]


def post_ln(x1, f, mod4, lng, lnb, n_ctx_rows):
    b, t, d = x1.shape
    tm = MERGE_TM if (t % MERGE_TM == 0 and n_ctx_rows % MERGE_TM == 0) else 128
    nctx_t = n_ctx_rows // tm
    row = pl.BlockSpec((None, tm, d), lambda bi, ti: (bi, ti, 0))
    const = lambda a: pl.BlockSpec(a.shape, lambda bi, ti: (0,) * a.ndim)
    return pl.pallas_call(
        _post_body,
        grid=(b, t // tm),
        in_specs=[row, row,
                  pl.BlockSpec((None, None, 1, d), lambda bi, ti: (jnp.where(ti < nctx_t, b, bi), 5, 0, 0)),
                  const(lng), const(lnb)],
        out_specs=row,
        out_shape=jax.ShapeDtypeStruct((b, t, d), F32),
        compiler_params=_cparams("parallel", "parallel"),
        name="post_ln",
    )(x1, f, mod4, lng, lnb)


MOE_UP_TC = 1024
MOE_DOWN_TN = 1024
SEL_W = 512


def _refresh(be_ref, i):
    return jnp.logical_or(i == 0, be_ref[i] != be_ref[jnp.maximum(i - 1, 0)])


def _moe_up_body(be_ref, nvalid_ref, x_ref, w1_ref, sel_ref, b1_ref, h_ref, wg_s, wl_s, bg_s, bl_s):
    i = pl.program_id(1)
    half = SEL_W // 2

    @pl.when(_refresh(be_ref, i))
    def _():
        for c in range(MOE_UP_TC // SEL_W):
            both = _dot(w1_ref[:, c * SEL_W:(c + 1) * SEL_W].astype(BF16), sel_ref[...])
            wg_s[:, c * half:(c + 1) * half] = both[:, :half].astype(BF16)
            wl_s[:, c * half:(c + 1) * half] = both[:, half:].astype(BF16)
            b_hi, b_mid, b_lo = _split3(jnp.broadcast_to(b1_ref[:, c * SEL_W:(c + 1) * SEL_W], (8, SEL_W)))
            bias = _dot(b_hi, sel_ref[...]) + _dot(b_mid, sel_ref[...]) + _dot(b_lo, sel_ref[...])
            bg_s[:, c * half:(c + 1) * half] = bias[:, :half]
            bl_s[:, c * half:(c + 1) * half] = bias[:, half:]

    @pl.when(i >= nvalid_ref[0])
    def _():
        h_ref[...] = jnp.zeros_like(h_ref)

    @pl.when(i < nvalid_ref[0])
    def _():
        x = x_ref[...]
        glu = jnp.minimum(_dot(x, wg_s[...]) + bg_s[0:1, :], SWIGLU_LIMIT)
        lin = jnp.clip(_dot(x, wl_s[...]) + bl_s[0:1, :], -SWIGLU_LIMIT, SWIGLU_LIMIT)
        h_ref[...] = (glu * _sigmoid(SWIGLU_ALPHA * glu) * (lin + 1.0)).astype(BF16)


def _moe_down_body(be_ref, nvalid_ref, h_ref, w2_ref, b2_ref, y_ref, w_s):
    i = pl.program_id(1)

    @pl.when(_refresh(be_ref, i))
    def _():
        w_s[...] = w2_ref[...].astype(BF16)

    @pl.when(i >= nvalid_ref[0])
    def _():
        y_ref[...] = jnp.zeros_like(y_ref)

    @pl.when(i < nvalid_ref[0])
    def _():
        y_ref[...] = (_dot(h_ref[...], w_s[...]) + b2_ref[...]).astype(BF16)


def moe_expert_blocks(x_sorted, block_e, n_valid, w1, b1, w2, b2, layer):
    n_slots, d = x_sorted.shape
    nb = n_slots // MOE_TM
    dh = w2.shape[2]
    half = SEL_W // 2
    src = jnp.arange(SEL_W)[:, None]
    dst = jnp.arange(SEL_W)[None, :]
    sel = (src == jnp.where(dst < half, 2 * dst, 2 * (dst - half) + 1)).astype(BF16)
    hidden = pl.pallas_call(
        _moe_up_body,
        grid_spec=pltpu.PrefetchScalarGridSpec(
            num_scalar_prefetch=2,
            grid=(2 * dh // MOE_UP_TC, nb),
            in_specs=[
                pl.BlockSpec((MOE_TM, d), lambda j, i, be, nv: (i, 0)),
                pl.BlockSpec((None, None, d, MOE_UP_TC), lambda j, i, be, nv: (layer, be[i], 0, j)),
                pl.BlockSpec((SEL_W, SEL_W), lambda j, i, be, nv: (0, 0)),
                pl.BlockSpec((None, None, 1, MOE_UP_TC), lambda j, i, be, nv: (layer, be[i], 0, j)),
            ],
            out_specs=pl.BlockSpec((MOE_TM, MOE_UP_TC // 2), lambda j, i, be, nv: (i, j)),
            scratch_shapes=[pltpu.VMEM((d, MOE_UP_TC // 2), BF16), pltpu.VMEM((d, MOE_UP_TC // 2), BF16),
                            pltpu.VMEM((8, MOE_UP_TC // 2), F32), pltpu.VMEM((8, MOE_UP_TC // 2), F32)],
        ),
        out_shape=jax.ShapeDtypeStruct((n_slots, dh), BF16),
        compiler_params=_cparams("arbitrary", "arbitrary"),
        name="moe_up",
    )(block_e, n_valid, x_sorted, w1, sel, b1.reshape(b1.shape[0], b1.shape[1], 1, b1.shape[2]))
    return pl.pallas_call(
        _moe_down_body,
        grid_spec=pltpu.PrefetchScalarGridSpec(
            num_scalar_prefetch=2,
            grid=(d // MOE_DOWN_TN, nb),
            in_specs=[
                pl.BlockSpec((MOE_TM, dh), lambda j, i, be, nv: (i, 0)),
                pl.BlockSpec((None, None, dh, MOE_DOWN_TN), lambda j, i, be, nv: (layer, be[i], 0, j)),
                pl.BlockSpec((None, None, 1, MOE_DOWN_TN), lambda j, i, be, nv: (layer, be[i], 0, j)),
            ],
            out_specs=pl.BlockSpec((MOE_TM, MOE_DOWN_TN), lambda j, i, be, nv: (i, j)),
            scratch_shapes=[pltpu.VMEM((dh, MOE_DOWN_TN), BF16)],
        ),
        out_shape=jax.ShapeDtypeStruct((n_slots, d), BF16),
        compiler_params=_cparams("arbitrary", "arbitrary"),
        name="moe_down",
    )(block_e, n_valid, hidden, w2, b2.reshape(b2.shape[0], b2.shape[1], 1, b2.shape[2]))


def moe_ffn(h, logits, w1, b1, w2, b2, layer):
    t, d = h.shape
    top_logit, top_idx = lax.top_k(logits, TOP_K)
    top_w = jax.nn.softmax(top_logit, axis=-1)
    n_assign = t * TOP_K
    flat_e = top_idx.reshape(-1)
    onehot = (flat_e[:, None] == jnp.arange(N_EXPERTS, dtype=flat_e.dtype)[None, :]).astype(jnp.int32)
    csum = jnp.cumsum(onehot, axis=0)
    counts = csum[-1]
    padded = (counts + MOE_TM - 1) // MOE_TM * MOE_TM
    pend = jnp.cumsum(padded)
    pstart = pend - padded
    dest = (jnp.sum(onehot * (csum + pstart[None, :]), axis=1) - 1).astype(jnp.int32)
    n_blocks = (n_assign + N_EXPERTS * (MOE_TM - 1) + MOE_TM - 1) // MOE_TM
    n_slots = n_blocks * MOE_TM
    slot_tok = jnp.zeros((n_slots,), jnp.int32).at[dest].set(jnp.arange(n_assign, dtype=jnp.int32) // TOP_K)
    n_valid = (pend[-1] // MOE_TM).astype(jnp.int32)
    blk = jnp.arange(n_blocks)
    block_e = jnp.minimum(jnp.searchsorted(pend, jnp.minimum(blk, n_valid - 1) * MOE_TM, side='right'),
                          N_EXPERTS - 1).astype(jnp.int32)
    x_sorted = h[slot_tok]
    y = moe_expert_blocks(x_sorted, block_e, n_valid.reshape(1), w1, b1, w2, b2, layer)
    picked = y[dest.reshape(t, TOP_K)].astype(F32)
    return jnp.sum(picked * top_w[:, :, None], axis=1)


def _rot_partner(w, width=64):
    d, n = w.shape
    g = w.reshape(d, n // width, 2, width // 2)
    return jnp.stack([-g[:, :, 1], g[:, :, 0]], axis=2).reshape(d, n)


def _axial_angles(n_tok, rot_dim):
    rows = n_tok // GRID_W
    row = jnp.repeat(jnp.arange(rows), GRID_W).astype(F32)
    col = jnp.tile(jnp.arange(GRID_W), rows).astype(F32)
    n_freq = rot_dim // 4
    inv = jnp.power(ROPE_BASE, -jnp.arange(n_freq, dtype=F32) / n_freq)
    return jnp.concatenate([row[:, None] * inv, col[:, None] * inv], axis=-1)


def _rope_tables(n_ctx, n_lat, rot_dim, n_groups):
    ang = _axial_angles(n_lat, rot_dim)
    cos = jnp.concatenate([jnp.ones((n_ctx, rot_dim // 2), F32), jnp.cos(ang)], axis=0)
    sin = jnp.concatenate([jnp.zeros((n_ctx, rot_dim // 2), F32), jnp.sin(ang)], axis=0)
    t = n_ctx + n_lat
    pad = LANE - n_groups * rot_dim
    cos = jnp.concatenate([cos] * (2 * n_groups) + [jnp.ones((t, pad), F32)], axis=1)
    sin = jnp.concatenate([sin] * (2 * n_groups) + [jnp.zeros((t, pad), F32)], axis=1)
    return cos, sin


def _layer_weights(i, a):
    d = D_MODEL
    w_in = a['w_in'][i]
    z = lambda n: jnp.zeros((d, n), F32)
    seg = lambda c0, n: w_in[:, c0:c0 + n]
    kr = seg(C_MLA_KR, MLA_ROPE)
    groups = [
        seg(C_MLA_CKV, MLA_KV_LORA), kr, z(64), _rot_partner(kr), z(64),
        seg(C_DIFF_K, 512), _rot_partner(seg(C_DIFF_K, 512)), seg(C_DIFF_V, 512),
        seg(C_SSD_XBC, SSD_XBC_W), seg(C_LRU_X, LRU_WIDTH),
        seg(C_MLA_CQ, MLA_Q_LORA), seg(C_SSD_DT, 2 * SSD_HEADS), z(LANE - 2 * SSD_HEADS),
        seg(C_DIFF_Q, 512), _rot_partner(seg(C_DIFF_Q, 512)),
        seg(C_LRU_GATE, LRU_WIDTH), seg(C_SSD_Z, SSD_WIDTH), seg(C_GATES, N_BRANCH * d),
    ]
    w_in_r = jnp.concatenate(groups, axis=1).astype(BF16)

    w_uq = a['mla_w_uq'][i].reshape(MLA_Q_LORA, MLA_HEADS, MLA_NOPE + MLA_ROPE)
    w_ukv = a['mla_w_ukv'][i].reshape(MLA_KV_LORA, MLA_HEADS, MLA_NOPE + MLA_V)
    zq = jnp.zeros((MLA_Q_LORA, 64), F32)
    wq = jnp.stack([jnp.concatenate([w_uq[:, h, :], zq], axis=1) for h in range(MLA_HEADS)]).astype(BF16)
    wqp = jnp.stack([jnp.concatenate([_rot_partner(w_uq[:, h, MLA_NOPE:]), zq], axis=1)
                     for h in range(MLA_HEADS)]).astype(BF16)
    wuk = jnp.transpose(w_ukv[:, :, :MLA_NOPE], (1, 0, 2)).astype(BF16)
    wuv = jnp.transpose(w_ukv[:, :, MLA_NOPE:], (1, 0, 2)).astype(BF16)
    mla = (a['mla_g_kv'][i].reshape(1, -1), a['mla_g_q'][i].reshape(1, -1), wuk, wuv, wq, wqp)

    lam_pack = jnp.zeros((8, LANE), F32)
    for r, name in enumerate(('diff_lq1', 'diff_lk1', 'diff_lq2', 'diff_lk2')):
        lam_pack = lam_pack.at[r, :DIFF_HD].set(a[name][i])
    g_sub = a['diff_g_sub'][i].reshape(1, -1)

    lru = []
    for dd in range(2):
        wax = jnp.concatenate([a['lru_w_a'][i, dd], a['lru_w_x'][i, dd]], axis=-1).astype(BF16)
        lru.append((a['lru_conv_w'][i], a['lru_conv_b'][i].reshape(1, -1), wax,
                    a['lru_b_a'][i, dd].reshape(1, -1), a['lru_b_x'][i, dd].reshape(1, -1),
                    a['lru_lam'][i, dd].reshape(1, -1)))

    ssd = []
    pad_row = lambda v: jnp.zeros((1, LANE), F32).at[0, :SSD_HEADS].set(v)
    for dd in range(2):
        a_neg = -jnp.exp(a['ssd_a_log'][i, dd])
        bias = a['ssd_dt_bias'][i, dd]
        arow = jnp.zeros((1, LANE), F32).at[0, dd * SSD_HEADS:(dd + 1) * SSD_HEADS].set(a_neg)
        brow = jnp.zeros((1, LANE), F32).at[0, dd * SSD_HEADS:(dd + 1) * SSD_HEADS].set(bias)
        ssd.append((a['ssd_conv_w'][i], a['ssd_conv_b'][i].reshape(1, -1), arow, a_neg.reshape(-1, 1),
                    brow, bias.reshape(-1, 1),
                    jnp.repeat(a['ssd_d'][i], SSD_HD).reshape(1, -1), a['ssd_g_norm'][i].reshape(1, -1), dd))

    rw = jnp.pad(a['router_w'][i], ((0, 0), (0, LANE - N_EXPERTS)))
    rwh = rw.astype(BF16)
    rwl = (rw - rwh.astype(F32)).astype(BF16)
    rb = jnp.pad(a['router_b'][i], (0, LANE - N_EXPERTS)).reshape(1, -1)
    merge = (a['w_branch'][i].astype(BF16), a['w_out'][i].astype(BF16),
             a['ln1_g'][i].reshape(1, -1), a['ln1_b'][i].reshape(1, -1), jnp.stack([rwh, rwl]), rb)
    return dict(w_in_r=w_in_r, mla=mla, lam_pack=lam_pack, g_sub=g_sub, lru=lru, ssd=ssd, merge=merge)


def _token_mixers(p_all, lw, tabs_mla, tabs_diff, layer_idx, n_ctx):
    b, t, _ = p_all.shape
    tq = 256 if (t % 256 == 0 and n_ctx % 256 == 0) else 128
    y_mla = mla_attn(p_all, tabs_mla, lw['mla'], n_ctx, tq)
    lambda_init = 0.8 - 0.6 * math.exp(-0.3 * layer_idx)
    y_diff = diff_attn(p_all, tabs_diff, lw['lam_pack'], lw['g_sub'], lambda_init, n_ctx, tq)

    x_t = jnp.transpose(p_all[:, :, G_LRU_X * GROUP:(G_LRU_X + 1) * GROUP], (1, 0, 2))
    g_t = jnp.transpose(p_all[:, :, G_LRU_GATE * GROUP:(G_LRU_GATE + 1) * GROUP], (1, 0, 2))
    tc = 256 if (t % 256 == 0 and n_ctx % 256 == 0) else 128
    h_f = lru_direction(x_t, None, None, lw['lru'][0], n_ctx, tc, False)
    y_lru = jnp.transpose(lru_direction(x_t, g_t, h_f, lw['lru'][1], n_ctx, tc, True), (1, 0, 2))

    dt0 = G_MLA_CQ * GROUP + MLA_Q_LORA
    dt_t = jnp.transpose(p_all[:, :, dt0:dt0 + 2 * SSD_HEADS].astype(F32), (0, 2, 1))
    ssd_f, ssd_b = lw['ssd']
    y_f = ssd_direction(p_all, dt_t, None, ssd_f[:8], n_ctx, False)
    y_ssd = ssd_direction(p_all, dt_t, y_f, ssd_b[:8], n_ctx, True)
    return y_mla, y_diff, y_lru, y_ssd


def kernel(x, c, ctx, c_ctx, w_mod, b_mod, w_in, mla_g_q, mla_g_kv, mla_w_uq, mla_w_ukv, diff_lq1, diff_lk1, diff_lq2, diff_lk2, diff_g_sub, lru_conv_w, lru_conv_b, lru_w_a, lru_b_a, lru_w_x, lru_b_x, lru_lam, ssd_conv_w, ssd_conv_b, ssd_a_log, ssd_dt_bias, ssd_d, ssd_g_norm, w_branch, w_out, ln1_g, ln1_b, ln2_g, ln2_b, router_w, router_b, w1, b1, w2, b2):
    a = dict(w_in=w_in, mla_g_q=mla_g_q, mla_g_kv=mla_g_kv, mla_w_uq=mla_w_uq, mla_w_ukv=mla_w_ukv,
             diff_lq1=diff_lq1, diff_lk1=diff_lk1, diff_lq2=diff_lq2, diff_lk2=diff_lk2, diff_g_sub=diff_g_sub,
             lru_conv_w=lru_conv_w, lru_conv_b=lru_conv_b, lru_w_a=lru_w_a, lru_b_a=lru_b_a, lru_w_x=lru_w_x,
             lru_b_x=lru_b_x, lru_lam=lru_lam, ssd_conv_w=ssd_conv_w, ssd_conv_b=ssd_conv_b, ssd_a_log=ssd_a_log,
             ssd_dt_bias=ssd_dt_bias, ssd_d=ssd_d, ssd_g_norm=ssd_g_norm, w_branch=w_branch, w_out=w_out,
             ln1_g=ln1_g, ln1_b=ln1_b, router_w=router_w, router_b=router_b)
    bsz, n_lat, d = x.shape
    n_ctx = ctx.shape[1]
    t = n_ctx + n_lat
    depth = w_in.shape[0]
    tabs_mla = _rope_tables(n_ctx, n_lat, MLA_ROPE, 1)
    tabs_diff = _rope_tables(n_ctx, n_lat, DIFF_HD, 2)
    x_all = jnp.concatenate([ctx, x], axis=1)
    cond = jnp.zeros((16, d), F32).at[:bsz].set(jax.nn.silu(c)).at[bsz].set(jax.nn.silu(c_ctx))

    for i in range(depth):
        last = i == depth - 1
        lw = _layer_weights(i, a)
        mod4 = matmul_bias(cond, w_mod[i], b_mod[i]).reshape(16, 6, 1, d)
        p_all = ln_mod_proj(x_all, mod4, lw['w_in_r'], n_ctx)
        ys = _token_mixers(p_all, lw, tabs_mla, tabs_diff, i, n_ctx)
        row0 = n_ctx if last else 0
        x1, v, logits = merge_ln(x_all, ys, p_all, mod4, lw['merge'], n_ctx, row0)
        tr = t - row0
        f = moe_ffn(v.reshape(bsz * tr, d), logits.reshape(bsz * tr, LANE)[:, :N_EXPERTS], w1, b1, w2, b2, i)
        x_all = post_ln(x1, f.reshape(bsz, tr, d), mod4, ln2_g[i].reshape(1, -1), ln2_b[i].reshape(1, -1),
                        n_ctx - row0)
    return x_all
```

```python
import functools
import math

import numpy as np
import jax
import jax.numpy as jnp
from jax import lax
from jax.experimental import pallas as pl
from jax.experimental.pallas import tpu as pltpu

D_MODEL = 2048
DEPTH = 2
GRID_W = 64
ROPE_BASE = 10000.0

MLA_HEADS = 4
MLA_Q_LORA = 384
MLA_KV_LORA = 256
MLA_NOPE = 128
MLA_ROPE = 64
MLA_V = 128
DIFF_HEADS = 4
DIFF_HD = 64
LRU_WIDTH = 512
LRU_BLOCKS = 4
LRU_CONV = 4
LRU_C = 8.0
SSD_HEADS = 8
SSD_HD = 64
SSD_WIDTH = SSD_HEADS * SSD_HD
SSD_GROUPS = 2
SSD_STATE = 128
SSD_CONV = 4
SSD_CHUNK = 128
SSD_XBC_W = SSD_WIDTH + 2 * SSD_GROUPS * SSD_STATE
N_BRANCH = 4
BRANCH_WIDTH = 512
N_EXPERTS = 32
TOP_K = 4
D_EXPERT = D_MODEL
SWIGLU_LIMIT = 7.0
SWIGLU_ALPHA = 1.702
DEEPNORM_ALPHA = (2 * DEPTH) ** 0.25
LN_EPS = 1e-6

C_MLA_CKV = 0
C_MLA_KR = C_MLA_CKV + MLA_KV_LORA
C_DIFF_K = C_MLA_KR + MLA_ROPE
C_DIFF_V = C_DIFF_K + 2 * DIFF_HEADS * DIFF_HD
C_LRU_X = C_DIFF_V + DIFF_HEADS * 2 * DIFF_HD
C_SSD_XBC = C_LRU_X + LRU_WIDTH
C_SSD_DT = C_SSD_XBC + SSD_XBC_W
N_STATE_COLS = C_SSD_DT + 2 * SSD_HEADS
C_MLA_CQ = N_STATE_COLS
C_DIFF_Q = C_MLA_CQ + MLA_Q_LORA
C_LRU_GATE = C_DIFF_Q + 2 * DIFF_HEADS * DIFF_HD
C_SSD_Z = C_LRU_GATE + LRU_WIDTH
C_GATES = C_SSD_Z + SSD_WIDTH

GROUP = 512
G_MLA_KV = 0
G_DIFF_K = 1
G_DIFF_KP = 2
G_DIFF_V = 3
G_SSD_XBC = 4
G_LRU_X = 6
G_MLA_CQ = 7
G_DIFF_Q = 8
G_DIFF_QP = 9
G_LRU_GATE = 10
G_SSD_Z = 11
G_GATES = 12
N_GROUPS = G_GATES + N_BRANCH * D_MODEL // GROUP
NP_COLS = N_GROUPS * GROUP

LANE = 128
V7X_VMEM_LIMIT_BYTES = 48 * 1024 * 1024

MOE_TM = 1024

BF16 = jnp.bfloat16
F32 = jnp.float32


def _cparams(*sem):
    return pltpu.CompilerParams(dimension_semantics=sem, vmem_limit_bytes=V7X_VMEM_LIMIT_BYTES)


def _dot(a, b):
    return jnp.dot(a, b, preferred_element_type=F32)


def _dot_nt(a, b):
    return lax.dot_general(a, b, (((1,), (1,)), ((), ())), preferred_element_type=F32)


def _dot_tn(a, b):
    return lax.dot_general(a, b, (((0,), (0,)), ((), ())), preferred_element_type=F32)


def _sigmoid(x):
    return 1.0 / (1.0 + jnp.exp(-x))


def _silu(x):
    return x * _sigmoid(x)


def _softplus(x):
    return jnp.maximum(x, 0.0) + jnp.log1p(jnp.exp(-jnp.abs(x)))


def _gelu_tanh(x):
    return 0.5 * x * (1.0 + jnp.tanh(math.sqrt(2.0 / math.pi) * (x + 0.044715 * (x * x * x))))


def _split3(x):
    hi = x.astype(BF16)
    r1 = x - hi.astype(F32)
    mid = r1.astype(BF16)
    lo = (r1 - mid.astype(F32)).astype(BF16)
    return hi, mid, lo


def _mm_body(a_ref, b_ref, bias_ref, o_ref):
    o_ref[...] = (_dot(a_ref[...], b_ref[...]) + bias_ref[...]).astype(o_ref.dtype)


def matmul_bias(a, b, bias, tn=512):
    m, k = a.shape
    _, n = b.shape
    return pl.pallas_call(
        _mm_body,
        grid=(n // tn,),
        in_specs=[pl.BlockSpec((m, k), lambda j: (0, 0)),
                  pl.BlockSpec((k, tn), lambda j: (0, j)),
                  pl.BlockSpec((1, tn), lambda j: (0, j))],
        out_specs=pl.BlockSpec((m, tn), lambda j: (0, j)),
        out_shape=jax.ShapeDtypeStruct((m, n), F32),
        compiler_params=_cparams("parallel"),
        name="mod_matmul",
    )(a.astype(BF16), b.astype(BF16), bias.reshape(1, n))


PROJ_TM = 768
PROJ_TN = 1024


def _proj_body(x_ref, shc_ref, scc_ref, shl_ref, scl_ref, w_ref, o_ref, u_ref, *, n_ctx, tm):
    t = pl.program_id(1)

    @pl.when(pl.program_id(2) == 0)
    def _():
        x = x_ref[...]
        xc = x - jnp.mean(x, axis=-1, keepdims=True)
        y = xc * lax.rsqrt(jnp.mean(xc * xc, axis=-1, keepdims=True) + LN_EPS)
        is_ctx = lax.broadcasted_iota(jnp.int32, (tm, 1), 0) + t * tm < n_ctx
        sh = jnp.where(is_ctx, shc_ref[...], shl_ref[...])
        sc = jnp.where(is_ctx, scc_ref[...], scl_ref[...])
        u_ref[...] = (y * (1.0 + sc) + sh).astype(BF16)

    o_ref[...] = _dot(u_ref[...], w_ref[...]).astype(BF16)


def ln_mod_proj(x_all, mod4, w_in_r, n_ctx):
    b, t, d = x_all.shape
    tm = PROJ_TM if t % PROJ_TM == 0 else 128
    npc = w_in_r.shape[1]
    mod_spec = lambda row_fn, k: pl.BlockSpec((None, None, 1, d), lambda bi, ti, j: (row_fn(bi), k, 0, 0))
    return pl.pallas_call(
        functools.partial(_proj_body, n_ctx=n_ctx, tm=tm),
        grid=(b, t // tm, npc // PROJ_TN),
        in_specs=[pl.BlockSpec((None, tm, d), lambda bi, ti, j: (bi, ti, 0)),
                  mod_spec(lambda bi: b, 0), mod_spec(lambda bi: b, 1),
                  mod_spec(lambda bi: bi, 0), mod_spec(lambda bi: bi, 1),
                  pl.BlockSpec((d, PROJ_TN), lambda bi, ti, j: (0, j))],
        out_specs=pl.BlockSpec((None, tm, PROJ_TN), lambda bi, ti, j: (bi, ti, j)),
        out_shape=jax.ShapeDtypeStruct((b, t, npc), BF16),
        scratch_shapes=[pltpu.VMEM((tm, d), BF16)],
        compiler_params=_cparams("parallel", "parallel", "arbitrary"),
        name="ln_mod_proj",
    )(x_all, mod4, mod4, mod4, mod4, w_in_r)


def _mla_body(kvsrc_ref, cq_ref, cosk_ref, sink_ref, cosq_ref, sinq_ref, gkv_ref, gq_ref,
              wuk_ref, wuv_ref, wq_ref, wqp_ref, o_ref, k_s, v_s, *, n_ctx, tq, scale):
    qi = pl.program_id(1)
    t_all = k_s.shape[1]

    @pl.when(qi == 0)
    def _():
        src = kvsrc_ref[...].astype(F32)
        ckv = src[:, :MLA_KV_LORA]
        ckv = ckv * lax.rsqrt(jnp.mean(ckv * ckv, axis=-1, keepdims=True) + LN_EPS) * gkv_ref[...]
        ckv = ckv.astype(BF16)
        k_rope = (src[:, 256:384] * cosk_ref[...] + src[:, 384:512] * sink_ref[...]).astype(BF16)
        for h in range(MLA_HEADS):
            k_s[h, :, 0:LANE] = _dot(ckv, wuk_ref[h]).astype(BF16)
            k_s[h, :, LANE:2 * LANE] = k_rope
            v_s[h] = _dot(ckv, wuv_ref[h]).astype(BF16)

    cq = cq_ref[...].astype(F32)[:, :MLA_Q_LORA]
    cq = cq * lax.rsqrt(jnp.mean(cq * cq, axis=-1, keepdims=True) + LN_EPS) * gq_ref[...]
    cq = cq.astype(BF16)
    cosq = cosq_ref[...]
    sinq = sinq_ref[...]

    def attend(nk):
        for h in range(MLA_HEADS):
            qh = _dot(cq, wq_ref[h])
            qp = _dot(cq, wqp_ref[h])
            q_rope = qh[:, LANE:] * cosq + qp * sinq
            q = (jnp.concatenate([qh[:, :LANE], q_rope], axis=-1) * scale).astype(BF16)
            s = _dot_nt(q, k_s[h, 0:nk, :])
            p = jnp.exp(s - jnp.max(s, axis=-1, keepdims=True))
            l = jnp.sum(p, axis=-1, keepdims=True)
            o = _dot(p.astype(BF16), v_s[h, 0:nk, :])
            o_ref[:, h * MLA_V:(h + 1) * MLA_V] = (o * (1.0 / l)).astype(BF16)

    @pl.when(qi * tq < n_ctx)
    def _():
        attend(n_ctx)

    @pl.when(qi * tq >= n_ctx)
    def _():
        attend(t_all)


def mla_attn(p_all, tabs, wts, n_ctx, tq):
    b, t, _ = p_all.shape
    cosk, sink = tabs
    gkv, gq, wuk, wuv, wq, wqp = wts
    full = lambda shape: pl.BlockSpec(shape, lambda bi, qi: (0,) * len(shape))
    return pl.pallas_call(
        functools.partial(_mla_body, n_ctx=n_ctx, tq=tq, scale=(MLA_NOPE + MLA_ROPE) ** -0.5),
        grid=(b, t // tq),
        in_specs=[pl.BlockSpec((None, t, GROUP), lambda bi, qi: (bi, 0, G_MLA_KV)),
                  pl.BlockSpec((None, tq, GROUP), lambda bi, qi: (bi, qi, G_MLA_CQ)),
                  full((t, LANE)), full((t, LANE)),
                  pl.BlockSpec((tq, LANE), lambda bi, qi: (qi, 0)),
                  pl.BlockSpec((tq, LANE), lambda bi, qi: (qi, 0)),
                  full(gkv.shape), full(gq.shape), full(wuk.shape), full(wuv.shape), full(wq.shape), full(wqp.shape)],
        out_specs=pl.BlockSpec((None, tq, MLA_HEADS * MLA_V), lambda bi, qi: (bi, qi, 0)),
        out_shape=jax.ShapeDtypeStruct((b, t, MLA_HEADS * MLA_V), BF16),
        scratch_shapes=[pltpu.VMEM((MLA_HEADS, t, 2 * LANE), BF16), pltpu.VMEM((MLA_HEADS, t, MLA_V), BF16)],
        compiler_params=_cparams("parallel", "arbitrary"),
        name="mla_attn",
    )(p_all, p_all, cosk, sink, cosk, sink, gkv, gq, wuk, wuv, wq, wqp)


def _diff_body(k_ref, kp_ref, v_ref, q_ref, qp_ref, cosk_ref, sink_ref, cosq_ref, sinq_ref, lam_ref, gsub_ref,
               o_ref, k_s, *, n_ctx, tq, scale, lambda_init):
    qi = pl.program_id(1)
    t_all = k_s.shape[0]
    nrep = GROUP // LANE

    @pl.when(qi == 0)
    def _():
        cosk = jnp.concatenate([cosk_ref[...]] * nrep, axis=-1)
        sink = jnp.concatenate([sink_ref[...]] * nrep, axis=-1)
        k_s[...] = (k_ref[...].astype(F32) * cosk + kp_ref[...].astype(F32) * sink).astype(BF16)

    cosq = jnp.concatenate([cosq_ref[...]] * nrep, axis=-1)
    sinq = jnp.concatenate([sinq_ref[...]] * nrep, axis=-1)
    q = (q_ref[...].astype(F32) * cosq + qp_ref[...].astype(F32) * sinq) * scale
    lp = lam_ref[...]
    lam = (jnp.exp(jnp.sum(lp[0:1] * lp[1:2], axis=-1, keepdims=True))
           - jnp.exp(jnp.sum(lp[2:3] * lp[3:4], axis=-1, keepdims=True)) + lambda_init)
    first = lax.broadcasted_iota(jnp.int32, (1, LANE), 1) < DIFF_HD

    def softmax_pv(qm, kh, vh):
        s = _dot_nt(qm, kh)
        p = jnp.exp(s - jnp.max(s, axis=-1, keepdims=True))
        l = jnp.sum(p, axis=-1, keepdims=True)
        return _dot(p.astype(BF16), vh) * (1.0 / l)

    def attend(nk):
        for h in range(DIFF_HEADS):
            qh = q[:, h * LANE:(h + 1) * LANE]
            q1 = jnp.where(first, qh, 0.0).astype(BF16)
            q2 = jnp.where(first, 0.0, qh).astype(BF16)
            kh = k_s[0:nk, h * LANE:(h + 1) * LANE]
            vh = v_ref[0:nk, h * LANE:(h + 1) * LANE]
            o = softmax_pv(q1, kh, vh) - lam * softmax_pv(q2, kh, vh)
            o = o * lax.rsqrt(jnp.mean(o * o, axis=-1, keepdims=True) + LN_EPS) * gsub_ref[...]
            o_ref[:, h * LANE:(h + 1) * LANE] = (o * (1.0 - lambda_init)).astype(BF16)

    @pl.when(qi * tq < n_ctx)
    def _():
        attend(n_ctx)

    @pl.when(qi * tq >= n_ctx)
    def _():
        attend(t_all)


def diff_attn(p_all, tabs, lam_pack, g_sub, lambda_init, n_ctx, tq):
    b, t, _ = p_all.shape
    cosd, sind = tabs
    kv_spec = lambda g: pl.BlockSpec((None, t, GROUP), lambda bi, qi: (bi, 0, g))
    q_spec = lambda g: pl.BlockSpec((None, tq, GROUP), lambda bi, qi: (bi, qi, g))
    full = lambda shape: pl.BlockSpec(shape, lambda bi, qi: (0,) * len(shape))
    return pl.pallas_call(
        functools.partial(_diff_body, n_ctx=n_ctx, tq=tq, scale=DIFF_HD ** -0.5, lambda_init=lambda_init),
        grid=(b, t // tq),
        in_specs=[kv_spec(G_DIFF_K), kv_spec(G_DIFF_KP), kv_spec(G_DIFF_V), q_spec(G_DIFF_Q), q_spec(G_DIFF_QP),
                  full((t, LANE)), full((t, LANE)),
                  pl.BlockSpec((tq, LANE), lambda bi, qi: (qi, 0)),
                  pl.BlockSpec((tq, LANE), lambda bi, qi: (qi, 0)),
                  full(lam_pack.shape), full(g_sub.shape)],
        out_specs=pl.BlockSpec((None, tq, GROUP), lambda bi, qi: (bi, qi, 0)),
        out_shape=jax.ShapeDtypeStruct((b, t, GROUP), BF16),
        scratch_shapes=[pltpu.VMEM((t, GROUP), BF16)],
        compiler_params=_cparams("parallel", "arbitrary"),
        name="diff_attn",
    )(p_all, p_all, p_all, p_all, p_all, cosd, sind, cosd, sind, lam_pack, g_sub)


def _chunk_of_step(s, ncc, nlc, reverse):
    if not reverse:
        return s
    return jnp.where(s < ncc, ncc - 1 - s, 2 * ncc + nlc - 1 - s)


def _lru_body(*refs, ncc, nlc, tc, reverse):
    if reverse:
        (xp_ref, x_ref, xn_ref, cw_ref, cb_ref, wax_ref, ba_ref, bx_ref, lam_ref, hf_ref, gate_ref,
         o_ref, xe_s, a_s, in_s, h_s) = refs
    else:
        (xp_ref, x_ref, xn_ref, cw_ref, cb_ref, wax_ref, ba_ref, bx_ref, lam_ref,
         o_ref, xe_s, a_s, in_s, h_s) = refs
    s = pl.program_id(0)
    c = _chunk_of_step(s, ncc, nlc, reverse)
    nb = x_ref.shape[1]
    bw = LRU_WIDTH // LRU_BLOCKS

    @pl.when(s == 0)
    def _():
        h_s[...] = jnp.zeros_like(h_s)

    has_prev = jnp.logical_and(c != 0, c != ncc).astype(F32)
    has_next = jnp.logical_and(c != ncc - 1, c != ncc + nlc - 1).astype(F32)
    xe_s[0:2] = xp_ref[...].astype(F32) * has_prev
    xe_s[2:tc + 2] = x_ref[...].astype(F32)
    xe_s[tc + 2:tc + 3] = xn_ref[...].astype(F32) * has_next

    for blk in range(LRU_BLOCKS):
        ls = slice(blk * bw, (blk + 1) * bw)
        xc = cb_ref[:, ls].reshape(1, 1, bw)
        for k in range(LRU_CONV):
            xc = xc + cw_ref[k:k + 1, ls].reshape(1, 1, bw) * xe_s[k:k + tc, :, ls]
        xc2 = xc.reshape(tc * nb, bw)
        gates = _dot(xc2.astype(BF16), wax_ref[blk])
        r = _sigmoid(gates[:, :bw] + ba_ref[:, ls])
        i = _sigmoid(gates[:, bw:] + bx_ref[:, ls])
        log_a = -LRU_C * r * _softplus(-lam_ref[:, ls])
        a = jnp.exp(log_a)
        a_s[:, :, ls] = a.reshape(tc, nb, bw)
        in_s[:, :, ls] = (jnp.sqrt(1.0 - a * a) * (i * xc2)).reshape(tc, nb, bw)

    def step(j, h):
        tt = tc - 1 - j if reverse else j
        h = a_s[tt] * h + in_s[tt]
        in_s[tt] = h
        return h

    h_s[...] = lax.fori_loop(0, tc, step, h_s[...], unroll=8)

    if reverse:
        o_ref[...] = ((hf_ref[...] + in_s[...]) * _gelu_tanh(gate_ref[...].astype(F32))).astype(BF16)
    else:
        o_ref[...] = in_s[...]


def lru_direction(x_t, gate_t, h_fwd, wts, n_ctx, tc, reverse):
    t, nb, w = x_t.shape
    ncc, nlc = n_ctx // tc, (t - n_ctx) // tc
    cw, cb, wax, ba, bx, lam = wts
    cidx = lambda s: _chunk_of_step(s, ncc, nlc, reverse)
    full = lambda shape: pl.BlockSpec(shape, lambda s: (0,) * len(shape))
    chunk = pl.BlockSpec((tc, nb, w), lambda s: (cidx(s), 0, 0))
    in_specs = [pl.BlockSpec((2, nb, w), lambda s: (jnp.maximum(cidx(s) * (tc // 2) - 1, 0), 0, 0)),
                chunk,
                pl.BlockSpec((1, nb, w), lambda s: (jnp.minimum((cidx(s) + 1) * tc, t - 1), 0, 0)),
                full(cw.shape), full(cb.shape), full(wax.shape), full(ba.shape), full(bx.shape), full(lam.shape)]
    args = [x_t, x_t, x_t, cw, cb, wax, ba, bx, lam]
    if reverse:
        in_specs += [chunk, chunk]
        args += [h_fwd, gate_t]
    return pl.pallas_call(
        functools.partial(_lru_body, ncc=ncc, nlc=nlc, tc=tc, reverse=reverse),
        grid=(ncc + nlc,),
        in_specs=in_specs,
        out_specs=chunk,
        out_shape=jax.ShapeDtypeStruct((t, nb, w), BF16 if reverse else F32),
        scratch_shapes=[pltpu.VMEM((tc + 3, nb, w), F32), pltpu.VMEM((tc, nb, w), F32),
                        pltpu.VMEM((tc, nb, w), F32), pltpu.VMEM((nb, w), F32)],
        compiler_params=_cparams("arbitrary"),
        name="lru_bwd" if reverse else "lru_fwd",
    )(*args)


def _ssd_body(*refs, ncc, nlc, reverse):
    if reverse:
        (xp_ref, x_ref, xn_ref, dtc_ref, dtr_ref, cw_ref, cb_ref, arow_ref, acol_ref, brow_ref, bcol_ref,
         yf_ref, z_ref, dskip_ref, gn_ref, o_ref, xe_s, st_s) = refs
    else:
        (xp_ref, x_ref, xn_ref, dtc_ref, dtr_ref, cw_ref, cb_ref, arow_ref, acol_ref, brow_ref, bcol_ref,
         o_ref, xe_s, st_s) = refs
    s = pl.program_id(1)
    c = _chunk_of_step(s, ncc, nlc, reverse)
    ln = SSD_CHUNK
    npair = SSD_HEADS // 2
    lane0 = SSD_HEADS if reverse else 0

    @pl.when(s == 0)
    def _():
        st_s[...] = jnp.zeros_like(st_s)

    has_prev = jnp.logical_and(c != 0, c != ncc).astype(F32)
    has_next = jnp.logical_and(c != ncc - 1, c != ncc + nlc - 1).astype(F32)
    xe_s[0:8] = xp_ref[...].astype(F32)[8:16] * has_prev
    xe_s[8:ln + 8] = x_ref[...].astype(F32)
    xe_s[ln + 8:ln + 16] = xn_ref[...].astype(F32)[0:8] * has_next
    xbc = cb_ref[...]
    for k in range(SSD_CONV):
        xbc = xbc + cw_ref[k:k + 1, :] * xe_s[pl.ds(6 + k, ln), :]
    xbc = _silu(xbc)

    dt_c = _softplus(dtc_ref[...].astype(F32)[:, 3 * LANE:4 * LANE] + brow_ref[...])
    da_c = dt_c * arow_ref[...]
    dt_r = _softplus(dtr_ref[...] + bcol_ref[...])
    da_r = dt_r * acol_ref[...]
    li = lax.broadcasted_iota(jnp.int32, (ln, ln), 0)
    mi = lax.broadcasted_iota(jnp.int32, (ln, ln), 1)
    keep = (li <= mi) if reverse else (li >= mi)
    keep_bf = jnp.where(keep, 1.0, 0.0).astype(BF16)
    keep_t_bf = jnp.where((li >= mi) if reverse else (li <= mi), 1.0, 0.0).astype(BF16)
    cs_c = sum(_dot(keep_bf, part) for part in _split3(da_c))
    cs_r = sum(_dot(part, keep_t_bf) for part in _split3(da_r))
    tot_c = jnp.sum(da_c, axis=0, keepdims=True)
    e_in = jnp.exp(cs_c)
    e_out = jnp.exp(tot_c - cs_c)
    e_tot = jnp.exp(tot_c)

    lane_first = lax.broadcasted_iota(jnp.int32, (1, LANE), 1) < SSD_HD
    row_first = lax.broadcasted_iota(jnp.int32, (LANE, 1), 0) < SSD_HD

    def col(v, h):
        return jnp.broadcast_to(v[:, lane0 + h:lane0 + h + 1], (v.shape[0], LANE))

    for pr in range(npair):
        g = pr // (npair // SSD_GROUPS)
        h0, h1 = 2 * pr, 2 * pr + 1
        bm = xbc[:, SSD_WIDTH + g * SSD_STATE:SSD_WIDTH + (g + 1) * SSD_STATE].astype(BF16)
        cm = xbc[:, SSD_WIDTH + (SSD_GROUPS + g) * SSD_STATE:SSD_WIDTH + (SSD_GROUPS + g + 1) * SSD_STATE].astype(BF16)
        xs = xbc[:, pr * LANE:(pr + 1) * LANE]
        xdt = xs * jnp.where(lane_first, col(dt_c, h0), col(dt_c, h1))
        cb = _dot_nt(cm, bm)
        y = None
        for h, sel in ((h0, lane_first), (h1, jnp.logical_not(lane_first))):
            dec = jnp.exp(jnp.where(keep, col(cs_c, h) - cs_r[h:h + 1, :], -jnp.inf))
            part = _dot((cb * dec).astype(BF16), jnp.where(sel, xdt, 0.0).astype(BF16))
            y = part if y is None else y + part
        st = st_s[pr]
        y_off = _dot_nt(cm, st.astype(BF16)) * jnp.where(lane_first, col(e_in, h0), col(e_in, h1))
        xw = (xdt * jnp.where(lane_first, col(e_out, h0), col(e_out, h1))).astype(BF16)
        new = _dot_tn(xw, bm)
        dec_st = jnp.where(row_first, e_tot[:, lane0 + h0:lane0 + h0 + 1], e_tot[:, lane0 + h1:lane0 + h1 + 1])
        st_s[pr] = st * dec_st + new
        y = y + y_off
        if reverse:
            y = y + yf_ref[:, pr * LANE:(pr + 1) * LANE] + xs * dskip_ref[:, pr * LANE:(pr + 1) * LANE]
            o_ref[:, pr * LANE:(pr + 1) * LANE] = y
        else:
            o_ref[:, pr * LANE:(pr + 1) * LANE] = y

    if reverse:
        yz = o_ref[...] * _silu(z_ref[...].astype(F32))
        gw = SSD_WIDTH // SSD_GROUPS
        for g in range(SSD_GROUPS):
            blk = yz[:, g * gw:(g + 1) * gw]
            blk = blk * lax.rsqrt(jnp.mean(blk * blk, axis=-1, keepdims=True) + LN_EPS) * gn_ref[:, g * gw:(g + 1) * gw]
            o_ref[:, g * gw:(g + 1) * gw] = blk


def ssd_direction(p_all, dt_t, y_fwd, wts, n_ctx, reverse):
    b, t, _ = p_all.shape
    ln = SSD_CHUNK
    ncc, nlc = n_ctx // ln, (t - n_ctx) // ln
    cw, cb, arow, acol, brow, bcol, dskip, gn = wts
    cidx = lambda s: _chunk_of_step(s, ncc, nlc, reverse)
    full = lambda a: pl.BlockSpec(a.shape, lambda bi, s: (0,) * a.ndim)
    xw = 2 * GROUP
    gx = G_SSD_XBC // 2
    hb = ln // 16
    in_specs = [pl.BlockSpec((None, 16, xw), lambda bi, s: (bi, jnp.maximum(cidx(s) * hb - 1, 0), gx)),
                pl.BlockSpec((None, ln, xw), lambda bi, s: (bi, cidx(s), gx)),
                pl.BlockSpec((None, 16, xw), lambda bi, s: (bi, jnp.minimum((cidx(s) + 1) * hb, t // 16 - 1), gx)),
                pl.BlockSpec((None, ln, GROUP), lambda bi, s: (bi, cidx(s), G_MLA_CQ)),
                pl.BlockSpec((None, SSD_HEADS, ln), lambda bi, s: (bi, 1 if reverse else 0, cidx(s))),
                full(cw), full(cb), full(arow), full(acol), full(brow), full(bcol)]
    args = [p_all, p_all, p_all, p_all, dt_t, cw, cb, arow, acol, brow, bcol]
    out_spec = pl.BlockSpec((None, ln, SSD_WIDTH), lambda bi, s: (bi, cidx(s), 0))
    if reverse:
        in_specs += [out_spec, pl.BlockSpec((None, ln, GROUP), lambda bi, s: (bi, cidx(s), G_SSD_Z)), full(dskip), full(gn)]
        args += [y_fwd, p_all, dskip, gn]
    return pl.pallas_call(
        functools.partial(_ssd_body, ncc=ncc, nlc=nlc, reverse=reverse),
        grid=(b, ncc + nlc),
        in_specs=in_specs,
        out_specs=out_spec,
        out_shape=jax.ShapeDtypeStruct((b, t, SSD_WIDTH), F32),
        scratch_shapes=[pltpu.VMEM((ln + 16, xw), F32), pltpu.VMEM((SSD_HEADS // 2, LANE, SSD_STATE), F32)],
        compiler_params=_cparams("parallel", "arbitrary"),
        name="ssd_bwd" if reverse else "ssd_fwd",
    )(*args)


MERGE_TM = 256


def _merge_body(x_ref, y0_ref, y1_ref, y2_ref, y3_ref, gt0_ref, gt1_ref, gt2_ref, gt3_ref, wb_ref, wo_ref, g1_ref, sh2_ref, sc2_ref,
                lng_ref, lnb_ref, rw_ref, rb_ref, x1_ref, v_ref, lg_ref):
    d = D_MODEL
    merged = None
    for k, (y_ref, gt_ref) in enumerate(((y0_ref, gt0_ref), (y1_ref, gt1_ref), (y2_ref, gt2_ref), (y3_ref, gt3_ref))):
        gate = _sigmoid(gt_ref[...].astype(F32))
        term = gate * _dot(y_ref[...].astype(BF16), wb_ref[k])
        merged = term if merged is None else merged + term
    m = _dot(merged.astype(BF16), wo_ref[...])
    r = DEEPNORM_ALPHA * x_ref[...] + g1_ref[...] * m
    rc = r - jnp.mean(r, axis=-1, keepdims=True)
    x1 = rc * lax.rsqrt(jnp.mean(rc * rc, axis=-1, keepdims=True) + LN_EPS) * lng_ref[...] + lnb_ref[...]
    x1_ref[...] = x1
    xc = x1 - jnp.mean(x1, axis=-1, keepdims=True)
    v = xc * lax.rsqrt(jnp.mean(xc * xc, axis=-1, keepdims=True) + LN_EPS) * (1.0 + sc2_ref[...]) + sh2_ref[...]
    v_ref[...] = v.astype(BF16)
    vh = v.astype(BF16)
    vl = (v - vh.astype(F32)).astype(BF16)
    lg_ref[...] = _dot(vh, rw_ref[0]) + _dot(vl, rw_ref[0]) + _dot(vh, rw_ref[1]) + rb_ref[...]


def merge_ln(x_all, ys, p_all, mod4, wts, n_ctx, row0):
    b, t, d = x_all.shape
    tm = MERGE_TM if (t % MERGE_TM == 0 and n_ctx % MERGE_TM == 0) else 128
    t0 = row0 // tm
    nt = t // tm - t0
    wb, wo, lng, lnb, rw, rb = wts
    nctx_t = n_ctx // tm
    row = lambda w, dt: pl.BlockSpec((None, tm, w), lambda bi, ti: (bi, ti + t0, 0))
    mod_spec = lambda k: pl.BlockSpec((None, None, 1, d),
                                      lambda bi, ti: (jnp.where(ti + t0 < nctx_t, b, bi), k, 0, 0))
    const = lambda a: pl.BlockSpec(a.shape, lambda bi, ti: (0,) * a.ndim, pipeline_mode=pl.Buffered(1))
    out_row = lambda w: pl.BlockSpec((None, tm, w), lambda bi, ti: (bi, ti, 0))
    gate_spec = lambda k: pl.BlockSpec((None, tm, d), lambda bi, ti: (bi, ti + t0, G_GATES * GROUP // d + k))
    return pl.pallas_call(
        _merge_body,
        grid=(b, nt),
        in_specs=[row(d, F32), row(GROUP, BF16), row(GROUP, BF16), row(GROUP, BF16), row(GROUP, F32),
                  gate_spec(0), gate_spec(1), gate_spec(2), gate_spec(3),
                  const(wb), const(wo), mod_spec(2), mod_spec(3), mod_spec(4),
                  const(lng), const(lnb), const(rw), const(rb)],
        out_specs=[out_row(d), out_row(d), out_row(LANE)],
        out_shape=[jax.ShapeDtypeStruct((b, nt * tm, d), F32), jax.ShapeDtypeStruct((b, nt * tm, d), BF16),
                   jax.ShapeDtypeStruct((b, nt * tm, LANE), F32)],
        compiler_params=_cparams("parallel", "parallel"),
        name="merge_ln",
    )(x_all, *ys, p_all, p_all, p_all, p_all, wb, wo, mod4, mod4, mod4, lng, lnb, rw, rb)


def _post_body(x1_ref, f_ref, g2_ref, lng_ref, lnb_ref, o_ref):
    r = DEEPNORM_ALPHA * x1_ref[...] + g2_ref[...] * f_ref[...]
    rc = r - jnp.mean(r, axis=-1, keepdims=True)
    o_ref[...] = rc * lax.rsqrt(jnp.mean(rc * rc, axis=-1, keepdims=True) + LN_EPS) * lng_ref[...] + lnb_ref[...]


def post_ln(x1, f, mod4, lng, lnb, n_ctx_rows):
    b, t, d = x1.shape
    tm = MERGE_TM if (t % MERGE_TM == 0 and n_ctx_rows % MERGE_TM == 0) else 128
    nctx_t = n_ctx_rows // tm
    row = pl.BlockSpec((None, tm, d), lambda bi, ti: (bi, ti, 0))
    const = lambda a: pl.BlockSpec(a.shape, lambda bi, ti: (0,) * a.ndim)
    return pl.pallas_call(
        _post_body,
        grid=(b, t // tm),
        in_specs=[row, row,
                  pl.BlockSpec((None, None, 1, d), lambda bi, ti: (jnp.where(ti < nctx_t, b, bi), 5, 0, 0)),
                  const(lng), const(lnb)],
        out_specs=row,
        out_shape=jax.ShapeDtypeStruct((b, t, d), F32),
        compiler_params=_cparams("parallel", "parallel"),
        name="post_ln",
    )(x1, f, mod4, lng, lnb)


MOE_UP_TC = 1024
MOE_DOWN_TN = 1024
SEL_W = 512


def _refresh(be_ref, i):
    return jnp.logical_or(i == 0, be_ref[i] != be_ref[jnp.maximum(i - 1, 0)])


def _moe_up_body(be_ref, nvalid_ref, x_ref, w1_ref, sel_ref, b1_ref, h_ref, wg_s, wl_s, bg_s, bl_s):
    i = pl.program_id(1)
    half = SEL_W // 2

    @pl.when(_refresh(be_ref, i))
    def _():
        for c in range(MOE_UP_TC // SEL_W):
            both = _dot(w1_ref[:, c * SEL_W:(c + 1) * SEL_W].astype(BF16), sel_ref[...])
            wg_s[:, c * half:(c + 1) * half] = both[:, :half].astype(BF16)
            wl_s[:, c * half:(c + 1) * half] = both[:, half:].astype(BF16)
            b_hi, b_mid, b_lo = _split3(jnp.broadcast_to(b1_ref[:, c * SEL_W:(c + 1) * SEL_W], (8, SEL_W)))
            bias = _dot(b_hi, sel_ref[...]) + _dot(b_mid, sel_ref[...]) + _dot(b_lo, sel_ref[...])
            bg_s[:, c * half:(c + 1) * half] = bias[:, :half]
            bl_s[:, c * half:(c + 1) * half] = bias[:, half:]

    @pl.when(i >= nvalid_ref[0])
    def _():
        h_ref[...] = jnp.zeros_like(h_ref)

    @pl.when(i < nvalid_ref[0])
    def _():
        x = x_ref[...]
        glu = jnp.minimum(_dot(x, wg_s[...]) + bg_s[0:1, :], SWIGLU_LIMIT)
        lin = jnp.clip(_dot(x, wl_s[...]) + bl_s[0:1, :], -SWIGLU_LIMIT, SWIGLU_LIMIT)
        h_ref[...] = (glu * _sigmoid(SWIGLU_ALPHA * glu) * (lin + 1.0)).astype(BF16)


def _moe_down_body(be_ref, nvalid_ref, h_ref, w2_ref, b2_ref, y_ref, w_s):
    i = pl.program_id(1)

    @pl.when(_refresh(be_ref, i))
    def _():
        w_s[...] = w2_ref[...].astype(BF16)

    @pl.when(i >= nvalid_ref[0])
    def _():
        y_ref[...] = jnp.zeros_like(y_ref)

    @pl.when(i < nvalid_ref[0])
    def _():
        y_ref[...] = _dot(h_ref[...], w_s[...]) + b2_ref[...]


def moe_expert_blocks(x_sorted, block_e, n_valid, w1, b1, w2, b2, layer):
    n_slots, d = x_sorted.shape
    nb = n_slots // MOE_TM
    dh = w2.shape[2]
    half = SEL_W // 2
    src = jnp.arange(SEL_W)[:, None]
    dst = jnp.arange(SEL_W)[None, :]
    sel = (src == jnp.where(dst < half, 2 * dst, 2 * (dst - half) + 1)).astype(BF16)
    hidden = pl.pallas_call(
        _moe_up_body,
        grid_spec=pltpu.PrefetchScalarGridSpec(
            num_scalar_prefetch=2,
            grid=(2 * dh // MOE_UP_TC, nb),
            in_specs=[
                pl.BlockSpec((MOE_TM, d), lambda j, i, be, nv: (i, 0)),
                pl.BlockSpec((None, None, d, MOE_UP_TC), lambda j, i, be, nv: (layer, be[i], 0, j)),
                pl.BlockSpec((SEL_W, SEL_W), lambda j, i, be, nv: (0, 0)),
                pl.BlockSpec((None, None, 1, MOE_UP_TC), lambda j, i, be, nv: (layer, be[i], 0, j)),
            ],
            out_specs=pl.BlockSpec((MOE_TM, MOE_UP_TC // 2), lambda j, i, be, nv: (i, j)),
            scratch_shapes=[pltpu.VMEM((d, MOE_UP_TC // 2), BF16), pltpu.VMEM((d, MOE_UP_TC // 2), BF16),
                            pltpu.VMEM((8, MOE_UP_TC // 2), F32), pltpu.VMEM((8, MOE_UP_TC // 2), F32)],
        ),
        out_shape=jax.ShapeDtypeStruct((n_slots, dh), BF16),
        compiler_params=_cparams("arbitrary", "arbitrary"),
        name="moe_up",
    )(block_e, n_valid, x_sorted, w1, sel, b1.reshape(b1.shape[0], b1.shape[1], 1, b1.shape[2]))
    return pl.pallas_call(
        _moe_down_body,
        grid_spec=pltpu.PrefetchScalarGridSpec(
            num_scalar_prefetch=2,
            grid=(d // MOE_DOWN_TN, nb),
            in_specs=[
                pl.BlockSpec((MOE_TM, dh), lambda j, i, be, nv: (i, 0)),
                pl.BlockSpec((None, None, dh, MOE_DOWN_TN), lambda j, i, be, nv: (layer, be[i], 0, j)),
                pl.BlockSpec((None, None, 1, MOE_DOWN_TN), lambda j, i, be, nv: (layer, be[i], 0, j)),
            ],
            out_specs=pl.BlockSpec((MOE_TM, MOE_DOWN_TN), lambda j, i, be, nv: (i, j)),
            scratch_shapes=[pltpu.VMEM((dh, MOE_DOWN_TN), BF16)],
        ),
        out_shape=jax.ShapeDtypeStruct((n_slots, d), F32),
        compiler_params=_cparams("arbitrary", "arbitrary"),
        name="moe_down",
    )(block_e, n_valid, hidden, w2, b2.reshape(b2.shape[0], b2.shape[1], 1, b2.shape[2]))


def moe_ffn(h, logits, w1, b1, w2, b2, layer):
    t, d = h.shape
    top_logit, top_idx = lax.top_k(logits, TOP_K)
    top_w = jax.nn.softmax(top_logit, axis=-1)
    n_assign = t * TOP_K
    flat_e = top_idx.reshape(-1)
    onehot = (flat_e[:, None] == jnp.arange(N_EXPERTS, dtype=flat_e.dtype)[None, :]).astype(jnp.int32)
    csum = jnp.cumsum(onehot, axis=0)
    counts = csum[-1]
    padded = (counts + MOE_TM - 1) // MOE_TM * MOE_TM
    pend = jnp.cumsum(padded)
    pstart = pend - padded
    dest = (jnp.sum(onehot * (csum + pstart[None, :]), axis=1) - 1).astype(jnp.int32)
    n_blocks = (n_assign + N_EXPERTS * (MOE_TM - 1) + MOE_TM - 1) // MOE_TM
    n_slots = n_blocks * MOE_TM
    slot_tok = jnp.zeros((n_slots,), jnp.int32).at[dest].set(jnp.arange(n_assign, dtype=jnp.int32) // TOP_K)
    n_valid = (pend[-1] // MOE_TM).astype(jnp.int32)
    blk_row = jnp.minimum(jnp.arange(n_blocks), n_valid - 1) * MOE_TM
    block_e = jnp.minimum(jnp.sum((pend[None, :] <= blk_row[:, None]).astype(jnp.int32), axis=1), N_EXPERTS - 1)
    x_sorted = h[slot_tok]
    y = moe_expert_blocks(x_sorted, block_e, n_valid.reshape(1), w1, b1, w2, b2, layer)
    picked = y[dest.reshape(t, TOP_K).T.reshape(-1)].reshape(TOP_K, t, d)
    return jnp.sum(picked * top_w.T[:, :, None], axis=0)


def _rot_partner(w, width=64):
    d, n = w.shape
    g = w.reshape(d, n // width, 2, width // 2)
    return jnp.stack([-g[:, :, 1], g[:, :, 0]], axis=2).reshape(d, n)


def _axial_angles(n_tok, rot_dim):
    rows = n_tok // GRID_W
    row = jnp.repeat(jnp.arange(rows), GRID_W).astype(F32)
    col = jnp.tile(jnp.arange(GRID_W), rows).astype(F32)
    n_freq = rot_dim // 4
    inv = jnp.power(ROPE_BASE, -jnp.arange(n_freq, dtype=F32) / n_freq)
    return jnp.concatenate([row[:, None] * inv, col[:, None] * inv], axis=-1)


def _rope_tables(n_ctx, n_lat, rot_dim, n_groups):
    ang = _axial_angles(n_lat, rot_dim)
    cos = jnp.concatenate([jnp.ones((n_ctx, rot_dim // 2), F32), jnp.cos(ang)], axis=0)
    sin = jnp.concatenate([jnp.zeros((n_ctx, rot_dim // 2), F32), jnp.sin(ang)], axis=0)
    t = n_ctx + n_lat
    pad = LANE - n_groups * rot_dim
    cos = jnp.concatenate([cos] * (2 * n_groups) + [jnp.ones((t, pad), F32)], axis=1)
    sin = jnp.concatenate([sin] * (2 * n_groups) + [jnp.zeros((t, pad), F32)], axis=1)
    return cos, sin


def _layer_weights(i, a):
    d = D_MODEL
    w_in = a['w_in'][i]
    z = lambda n: jnp.zeros((d, n), F32)
    seg = lambda c0, n: w_in[:, c0:c0 + n]
    kr = seg(C_MLA_KR, MLA_ROPE)
    groups = [
        seg(C_MLA_CKV, MLA_KV_LORA), kr, z(64), _rot_partner(kr), z(64),
        seg(C_DIFF_K, 512), _rot_partner(seg(C_DIFF_K, 512)), seg(C_DIFF_V, 512),
        seg(C_SSD_XBC, SSD_XBC_W), seg(C_LRU_X, LRU_WIDTH),
        seg(C_MLA_CQ, MLA_Q_LORA), seg(C_SSD_DT, 2 * SSD_HEADS), z(LANE - 2 * SSD_HEADS),
        seg(C_DIFF_Q, 512), _rot_partner(seg(C_DIFF_Q, 512)),
        seg(C_LRU_GATE, LRU_WIDTH), seg(C_SSD_Z, SSD_WIDTH), seg(C_GATES, N_BRANCH * d),
    ]
    w_in_r = jnp.concatenate(groups, axis=1).astype(BF16)

    w_uq = a['mla_w_uq'][i].reshape(MLA_Q_LORA, MLA_HEADS, MLA_NOPE + MLA_ROPE)
    w_ukv = a['mla_w_ukv'][i].reshape(MLA_KV_LORA, MLA_HEADS, MLA_NOPE + MLA_V)
    zq = jnp.zeros((MLA_Q_LORA, 64), F32)
    wq = jnp.stack([jnp.concatenate([w_uq[:, h, :], zq], axis=1) for h in range(MLA_HEADS)]).astype(BF16)
    wqp = jnp.stack([jnp.concatenate([_rot_partner(w_uq[:, h, MLA_NOPE:]), zq], axis=1)
                     for h in range(MLA_HEADS)]).astype(BF16)
    wuk = jnp.transpose(w_ukv[:, :, :MLA_NOPE], (1, 0, 2)).astype(BF16)
    wuv = jnp.transpose(w_ukv[:, :, MLA_NOPE:], (1, 0, 2)).astype(BF16)
    mla = (a['mla_g_kv'][i].reshape(1, -1), a['mla_g_q'][i].reshape(1, -1), wuk, wuv, wq, wqp)

    lam_pack = jnp.zeros((8, LANE), F32)
    for r, name in enumerate(('diff_lq1', 'diff_lk1', 'diff_lq2', 'diff_lk2')):
        lam_pack = lam_pack.at[r, :DIFF_HD].set(a[name][i])
    g_sub = a['diff_g_sub'][i].reshape(1, -1)

    lru = []
    for dd in range(2):
        wax = jnp.concatenate([a['lru_w_a'][i, dd], a['lru_w_x'][i, dd]], axis=-1).astype(BF16)
        lru.append((a['lru_conv_w'][i], a['lru_conv_b'][i].reshape(1, -1), wax,
                    a['lru_b_a'][i, dd].reshape(1, -1), a['lru_b_x'][i, dd].reshape(1, -1),
                    a['lru_lam'][i, dd].reshape(1, -1)))

    ssd = []
    pad_row = lambda v: jnp.zeros((1, LANE), F32).at[0, :SSD_HEADS].set(v)
    for dd in range(2):
        a_neg = -jnp.exp(a['ssd_a_log'][i, dd])
        bias = a['ssd_dt_bias'][i, dd]
        arow = jnp.zeros((1, LANE), F32).at[0, dd * SSD_HEADS:(dd + 1) * SSD_HEADS].set(a_neg)
        brow = jnp.zeros((1, LANE), F32).at[0, dd * SSD_HEADS:(dd + 1) * SSD_HEADS].set(bias)
        ssd.append((a['ssd_conv_w'][i], a['ssd_conv_b'][i].reshape(1, -1), arow, a_neg.reshape(-1, 1),
                    brow, bias.reshape(-1, 1),
                    jnp.repeat(a['ssd_d'][i], SSD_HD).reshape(1, -1), a['ssd_g_norm'][i].reshape(1, -1), dd))

    rw = jnp.pad(a['router_w'][i], ((0, 0), (0, LANE - N_EXPERTS)))
    rwh = rw.astype(BF16)
    rwl = (rw - rwh.astype(F32)).astype(BF16)
    rb = jnp.pad(a['router_b'][i], (0, LANE - N_EXPERTS)).reshape(1, -1)
    merge = (a['w_branch'][i].astype(BF16), a['w_out'][i].astype(BF16),
             a['ln1_g'][i].reshape(1, -1), a['ln1_b'][i].reshape(1, -1), jnp.stack([rwh, rwl]), rb)
    return dict(w_in_r=w_in_r, mla=mla, lam_pack=lam_pack, g_sub=g_sub, lru=lru, ssd=ssd, merge=merge)


def _token_mixers(p_all, lw, tabs_mla, tabs_diff, layer_idx, n_ctx):
    b, t, _ = p_all.shape
    tq = 256 if (t % 256 == 0 and n_ctx % 256 == 0) else 128
    y_mla = mla_attn(p_all, tabs_mla, lw['mla'], n_ctx, tq)
    lambda_init = 0.8 - 0.6 * math.exp(-0.3 * layer_idx)
    y_diff = diff_attn(p_all, tabs_diff, lw['lam_pack'], lw['g_sub'], lambda_init, n_ctx, tq)

    x_t = jnp.transpose(p_all[:, :, G_LRU_X * GROUP:(G_LRU_X + 1) * GROUP], (1, 0, 2))
    g_t = jnp.transpose(p_all[:, :, G_LRU_GATE * GROUP:(G_LRU_GATE + 1) * GROUP], (1, 0, 2))
    tc = 256 if (t % 256 == 0 and n_ctx % 256 == 0) else 128
    h_f = lru_direction(x_t, None, None, lw['lru'][0], n_ctx, tc, False)
    y_lru = jnp.transpose(lru_direction(x_t, g_t, h_f, lw['lru'][1], n_ctx, tc, True), (1, 0, 2))

    dt0 = G_MLA_CQ * GROUP + MLA_Q_LORA
    dt_t = jnp.transpose(p_all[:, :, dt0:dt0 + 2 * SSD_HEADS].astype(F32), (0, 2, 1))
    ssd_f, ssd_b = lw['ssd']
    y_f = ssd_direction(p_all, dt_t, None, ssd_f[:8], n_ctx, False)
    y_ssd = ssd_direction(p_all, dt_t, y_f, ssd_b[:8], n_ctx, True)
    return y_mla, y_diff, y_lru, y_ssd


def kernel(x, c, ctx, c_ctx, w_mod, b_mod, w_in, mla_g_q, mla_g_kv, mla_w_uq, mla_w_ukv, diff_lq1, diff_lk1, diff_lq2, diff_lk2, diff_g_sub, lru_conv_w, lru_conv_b, lru_w_a, lru_b_a, lru_w_x, lru_b_x, lru_lam, ssd_conv_w, ssd_conv_b, ssd_a_log, ssd_dt_bias, ssd_d, ssd_g_norm, w_branch, w_out, ln1_g, ln1_b, ln2_g, ln2_b, router_w, router_b, w1, b1, w2, b2):
    a = dict(w_in=w_in, mla_g_q=mla_g_q, mla_g_kv=mla_g_kv, mla_w_uq=mla_w_uq, mla_w_ukv=mla_w_ukv,
             diff_lq1=diff_lq1, diff_lk1=diff_lk1, diff_lq2=diff_lq2, diff_lk2=diff_lk2, diff_g_sub=diff_g_sub,
             lru_conv_w=lru_conv_w, lru_conv_b=lru_conv_b, lru_w_a=lru_w_a, lru_b_a=lru_b_a, lru_w_x=lru_w_x,
             lru_b_x=lru_b_x, lru_lam=lru_lam, ssd_conv_w=ssd_conv_w, ssd_conv_b=ssd_conv_b, ssd_a_log=ssd_a_log,
             ssd_dt_bias=ssd_dt_bias, ssd_d=ssd_d, ssd_g_norm=ssd_g_norm, w_branch=w_branch, w_out=w_out,
             ln1_g=ln1_g, ln1_b=ln1_b, router_w=router_w, router_b=router_b)
    bsz, n_lat, d = x.shape
    n_ctx = ctx.shape[1]
    t = n_ctx + n_lat
    depth = w_in.shape[0]
    tabs_mla = _rope_tables(n_ctx, n_lat, MLA_ROPE, 1)
    tabs_diff = _rope_tables(n_ctx, n_lat, DIFF_HD, 2)
    x_all = jnp.concatenate([ctx, x], axis=1)
    cond = jnp.zeros((16, d), F32).at[:bsz].set(jax.nn.silu(c)).at[bsz].set(jax.nn.silu(c_ctx))

    for i in range(depth):
        last = i == depth - 1
        lw = _layer_weights(i, a)
        mod4 = matmul_bias(cond, w_mod[i], b_mod[i]).reshape(16, 6, 1, d)
        p_all = ln_mod_proj(x_all, mod4, lw['w_in_r'], n_ctx)
        ys = _token_mixers(p_all, lw, tabs_mla, tabs_diff, i, n_ctx)
        row0 = n_ctx if last else 0
        x1, v, logits = merge_ln(x_all, ys, p_all, mod4, lw['merge'], n_ctx, row0)
        tr = t - row0
        f = moe_ffn(v.reshape(bsz * tr, d), logits.reshape(bsz * tr, LANE)[:, :N_EXPERTS], w1, b1, w2, b2, i)
        x_all = post_ln(x1, f.reshape(bsz, tr, d), mod4, ln2_g[i].reshape(1, -1), ln2_b[i].reshape(1, -1),
                        n_ctx - row0)
    return x_all
```

```python
import functools
import math

import numpy as np
import jax
import jax.numpy as jnp
from jax import lax
from jax.experimental import pallas as pl
from jax.experimental.pallas import tpu as pltpu

D_MODEL = 2048
DEPTH = 2
GRID_W = 64
ROPE_BASE = 10000.0

MLA_HEADS = 4
MLA_Q_LORA = 384
MLA_KV_LORA = 256
MLA_NOPE = 128
MLA_ROPE = 64
MLA_V = 128
DIFF_HEADS = 4
DIFF_HD = 64
LRU_WIDTH = 512
LRU_BLOCKS = 4
LRU_CONV = 4
LRU_C = 8.0
SSD_HEADS = 8
SSD_HD = 64
SSD_WIDTH = SSD_HEADS * SSD_HD
SSD_GROUPS = 2
SSD_STATE = 128
SSD_CONV = 4
SSD_CHUNK = 128
SSD_XBC_W = SSD_WIDTH + 2 * SSD_GROUPS * SSD_STATE
N_BRANCH = 4
BRANCH_WIDTH = 512
N_EXPERTS = 32
TOP_K = 4
D_EXPERT = D_MODEL
SWIGLU_LIMIT = 7.0
SWIGLU_ALPHA = 1.702
DEEPNORM_ALPHA = (2 * DEPTH) ** 0.25
LN_EPS = 1e-6

C_MLA_CKV = 0
C_MLA_KR = C_MLA_CKV + MLA_KV_LORA
C_DIFF_K = C_MLA_KR + MLA_ROPE
C_DIFF_V = C_DIFF_K + 2 * DIFF_HEADS * DIFF_HD
C_LRU_X = C_DIFF_V + DIFF_HEADS * 2 * DIFF_HD
C_SSD_XBC = C_LRU_X + LRU_WIDTH
C_SSD_DT = C_SSD_XBC + SSD_XBC_W
N_STATE_COLS = C_SSD_DT + 2 * SSD_HEADS
C_MLA_CQ = N_STATE_COLS
C_DIFF_Q = C_MLA_CQ + MLA_Q_LORA
C_LRU_GATE = C_DIFF_Q + 2 * DIFF_HEADS * DIFF_HD
C_SSD_Z = C_LRU_GATE + LRU_WIDTH
C_GATES = C_SSD_Z + SSD_WIDTH

GROUP = 512
G_MLA_KV = 0
G_DIFF_K = 1
G_DIFF_KP = 2
G_DIFF_V = 3
G_SSD_XBC = 4
G_LRU_X = 6
G_MLA_CQ = 7
G_DIFF_Q = 8
G_DIFF_QP = 9
G_LRU_GATE = 10
G_SSD_Z = 11
G_GATES = 12
N_GROUPS = G_GATES + N_BRANCH * D_MODEL // GROUP
NP_COLS = N_GROUPS * GROUP

LANE = 128
V7X_VMEM_LIMIT_BYTES = 48 * 1024 * 1024

MOE_TM = 512

BF16 = jnp.bfloat16
F32 = jnp.float32


def _cparams(*sem):
    return pltpu.CompilerParams(dimension_semantics=sem, vmem_limit_bytes=V7X_VMEM_LIMIT_BYTES)


def _dot(a, b):
    return jnp.dot(a, b, preferred_element_type=F32)


def _dot_nt(a, b):
    return lax.dot_general(a, b, (((1,), (1,)), ((), ())), preferred_element_type=F32)


def _dot_tn(a, b):
    return lax.dot_general(a, b, (((0,), (0,)), ((), ())), preferred_element_type=F32)


def _sigmoid(x):
    return 1.0 / (1.0 + jnp.exp(-x))


def _silu(x):
    return x * _sigmoid(x)


def _softplus(x):
    return jnp.maximum(x, 0.0) + jnp.log1p(jnp.exp(-jnp.abs(x)))


def _gelu_tanh(x):
    return 0.5 * x * (1.0 + jnp.tanh(math.sqrt(2.0 / math.pi) * (x + 0.044715 * (x * x * x))))


def _split3(x):
    hi = x.astype(BF16)
    r1 = x - hi.astype(F32)
    mid = r1.astype(BF16)
    lo = (r1 - mid.astype(F32)).astype(BF16)
    return hi, mid, lo


def _mm_body(a_ref, b_ref, bias_ref, o_ref):
    o_ref[...] = (_dot(a_ref[...], b_ref[...]) + bias_ref[...]).astype(o_ref.dtype)


def matmul_bias(a, b, bias, tn=512):
    m, k = a.shape
    _, n = b.shape
    return pl.pallas_call(
        _mm_body,
        grid=(n // tn,),
        in_specs=[pl.BlockSpec((m, k), lambda j: (0, 0)),
                  pl.BlockSpec((k, tn), lambda j: (0, j)),
                  pl.BlockSpec((1, tn), lambda j: (0, j))],
        out_specs=pl.BlockSpec((m, tn), lambda j: (0, j)),
        out_shape=jax.ShapeDtypeStruct((m, n), F32),
        compiler_params=_cparams("parallel"),
        name="mod_matmul",
    )(a.astype(BF16), b.astype(BF16), bias.reshape(1, n))


PROJ_TM = 768
PROJ_TN = 1024


def _proj_body(x_ref, shc_ref, scc_ref, shl_ref, scl_ref, w_ref, o_ref, u_ref, *, n_ctx, tm):
    t = pl.program_id(1)

    @pl.when(pl.program_id(2) == 0)
    def _():
        x = x_ref[...]
        xc = x - jnp.mean(x, axis=-1, keepdims=True)
        y = xc * lax.rsqrt(jnp.mean(xc * xc, axis=-1, keepdims=True) + LN_EPS)
        is_ctx = lax.broadcasted_iota(jnp.int32, (tm, 1), 0) + t * tm < n_ctx
        sh = jnp.where(is_ctx, shc_ref[...], shl_ref[...])
        sc = jnp.where(is_ctx, scc_ref[...], scl_ref[...])
        u_ref[...] = (y * (1.0 + sc) + sh).astype(BF16)

    o_ref[...] = _dot(u_ref[...], w_ref[...]).astype(BF16)


def ln_mod_proj(x_all, mod4, w_in_r, n_ctx):
    b, t, d = x_all.shape
    tm = PROJ_TM if t % PROJ_TM == 0 else 128
    npc = w_in_r.shape[1]
    mod_spec = lambda row_fn, k: pl.BlockSpec((None, None, 1, d), lambda bi, ti, j: (row_fn(bi), k, 0, 0))
    return pl.pallas_call(
        functools.partial(_proj_body, n_ctx=n_ctx, tm=tm),
        grid=(b, t // tm, npc // PROJ_TN),
        in_specs=[pl.BlockSpec((None, tm, d), lambda bi, ti, j: (bi, ti, 0)),
                  mod_spec(lambda bi: b, 0), mod_spec(lambda bi: b, 1),
                  mod_spec(lambda bi: bi, 0), mod_spec(lambda bi: bi, 1),
                  pl.BlockSpec((d, PROJ_TN), lambda bi, ti, j: (0, j))],
        out_specs=pl.BlockSpec((None, tm, PROJ_TN), lambda bi, ti, j: (bi, ti, j)),
        out_shape=jax.ShapeDtypeStruct((b, t, npc), BF16),
        scratch_shapes=[pltpu.VMEM((tm, d), BF16)],
        compiler_params=_cparams("parallel", "parallel", "arbitrary"),
        name="ln_mod_proj",
    )(x_all, mod4, mod4, mod4, mod4, w_in_r)


def _mla_body(kvsrc_ref, cq_ref, cosk_ref, sink_ref, cosq_ref, sinq_ref, gkv_ref, gq_ref,
              wuk_ref, wuv_ref, wq_ref, wqp_ref, o_ref, k_s, v_s, *, n_ctx, tq, scale):
    qi = pl.program_id(1)
    t_all = k_s.shape[1]

    @pl.when(qi == 0)
    def _():
        src = kvsrc_ref[...].astype(F32)
        ckv = src[:, :MLA_KV_LORA]
        ckv = ckv * lax.rsqrt(jnp.mean(ckv * ckv, axis=-1, keepdims=True) + LN_EPS) * gkv_ref[...]
        ckv = ckv.astype(BF16)
        k_rope = (src[:, 256:384] * cosk_ref[...] + src[:, 384:512] * sink_ref[...]).astype(BF16)
        for h in range(MLA_HEADS):
            k_s[h, :, 0:LANE] = _dot(ckv, wuk_ref[h]).astype(BF16)
            k_s[h, :, LANE:2 * LANE] = k_rope
            v_s[h] = _dot(ckv, wuv_ref[h]).astype(BF16)

    cq = cq_ref[...].astype(F32)[:, :MLA_Q_LORA]
    cq = cq * lax.rsqrt(jnp.mean(cq * cq, axis=-1, keepdims=True) + LN_EPS) * gq_ref[...]
    cq = cq.astype(BF16)
    cosq = cosq_ref[...]
    sinq = sinq_ref[...]

    def attend(nk):
        for h in range(MLA_HEADS):
            qh = _dot(cq, wq_ref[h])
            qp = _dot(cq, wqp_ref[h])
            q_rope = qh[:, LANE:] * cosq + qp * sinq
            q = (jnp.concatenate([qh[:, :LANE], q_rope], axis=-1) * scale).astype(BF16)
            s = _dot_nt(q, k_s[h, 0:nk, :])
            p = jnp.exp(s - jnp.max(s, axis=-1, keepdims=True))
            l = jnp.sum(p, axis=-1, keepdims=True)
            o = _dot(p.astype(BF16), v_s[h, 0:nk, :])
            o_ref[:, h * MLA_V:(h + 1) * MLA_V] = (o * (1.0 / l)).astype(BF16)

    @pl.when(qi * tq < n_ctx)
    def _():
        attend(n_ctx)

    @pl.when(qi * tq >= n_ctx)
    def _():
        attend(t_all)


def mla_attn(p_all, tabs, wts, n_ctx, tq):
    b, t, _ = p_all.shape
    cosk, sink = tabs
    gkv, gq, wuk, wuv, wq, wqp = wts
    full = lambda shape: pl.BlockSpec(shape, lambda bi, qi: (0,) * len(shape))
    return pl.pallas_call(
        functools.partial(_mla_body, n_ctx=n_ctx, tq=tq, scale=(MLA_NOPE + MLA_ROPE) ** -0.5),
        grid=(b, t // tq),
        in_specs=[pl.BlockSpec((None, t, GROUP), lambda bi, qi: (bi, 0, G_MLA_KV)),
                  pl.BlockSpec((None, tq, GROUP), lambda bi, qi: (bi, qi, G_MLA_CQ)),
                  full((t, LANE)), full((t, LANE)),
                  pl.BlockSpec((tq, LANE), lambda bi, qi: (qi, 0)),
                  pl.BlockSpec((tq, LANE), lambda bi, qi: (qi, 0)),
                  full(gkv.shape), full(gq.shape), full(wuk.shape), full(wuv.shape), full(wq.shape), full(wqp.shape)],
        out_specs=pl.BlockSpec((None, tq, MLA_HEADS * MLA_V), lambda bi, qi: (bi, qi, 0)),
        out_shape=jax.ShapeDtypeStruct((b, t, MLA_HEADS * MLA_V), BF16),
        scratch_shapes=[pltpu.VMEM((MLA_HEADS, t, 2 * LANE), BF16), pltpu.VMEM((MLA_HEADS, t, MLA_V), BF16)],
        compiler_params=_cparams("parallel", "arbitrary"),
        name="mla_attn",
    )(p_all, p_all, cosk, sink, cosk, sink, gkv, gq, wuk, wuv, wq, wqp)


def _diff_body(k_ref, kp_ref, v_ref, q_ref, qp_ref, cosk_ref, sink_ref, cosq_ref, sinq_ref, lam_ref, gsub_ref,
               o_ref, k_s, *, n_ctx, tq, scale, lambda_init):
    qi = pl.program_id(1)
    t_all = k_s.shape[0]
    nrep = GROUP // LANE

    @pl.when(qi == 0)
    def _():
        cosk = jnp.concatenate([cosk_ref[...]] * nrep, axis=-1)
        sink = jnp.concatenate([sink_ref[...]] * nrep, axis=-1)
        k_s[...] = (k_ref[...].astype(F32) * cosk + kp_ref[...].astype(F32) * sink).astype(BF16)

    cosq = jnp.concatenate([cosq_ref[...]] * nrep, axis=-1)
    sinq = jnp.concatenate([sinq_ref[...]] * nrep, axis=-1)
    q = (q_ref[...].astype(F32) * cosq + qp_ref[...].astype(F32) * sinq) * scale
    lp = lam_ref[...]
    lam = (jnp.exp(jnp.sum(lp[0:1] * lp[1:2], axis=-1, keepdims=True))
           - jnp.exp(jnp.sum(lp[2:3] * lp[3:4], axis=-1, keepdims=True)) + lambda_init)
    first = lax.broadcasted_iota(jnp.int32, (1, LANE), 1) < DIFF_HD

    def softmax_pv(qm, kh, vh):
        s = _dot_nt(qm, kh)
        p = jnp.exp(s - jnp.max(s, axis=-1, keepdims=True))
        l = jnp.sum(p, axis=-1, keepdims=True)
        return _dot(p.astype(BF16), vh) * (1.0 / l)

    def attend(nk):
        for h in range(DIFF_HEADS):
            qh = q[:, h * LANE:(h + 1) * LANE]
            q1 = jnp.where(first, qh, 0.0).astype(BF16)
            q2 = jnp.where(first, 0.0, qh).astype(BF16)
            kh = k_s[0:nk, h * LANE:(h + 1) * LANE]
            vh = v_ref[0:nk, h * LANE:(h + 1) * LANE]
            o = softmax_pv(q1, kh, vh) - lam * softmax_pv(q2, kh, vh)
            o = o * lax.rsqrt(jnp.mean(o * o, axis=-1, keepdims=True) + LN_EPS) * gsub_ref[...]
            o_ref[:, h * LANE:(h + 1) * LANE] = (o * (1.0 - lambda_init)).astype(BF16)

    @pl.when(qi * tq < n_ctx)
    def _():
        attend(n_ctx)

    @pl.when(qi * tq >= n_ctx)
    def _():
        attend(t_all)


def diff_attn(p_all, tabs, lam_pack, g_sub, lambda_init, n_ctx, tq):
    b, t, _ = p_all.shape
    cosd, sind = tabs
    kv_spec = lambda g: pl.BlockSpec((None, t, GROUP), lambda bi, qi: (bi, 0, g))
    q_spec = lambda g: pl.BlockSpec((None, tq, GROUP), lambda bi, qi: (bi, qi, g))
    full = lambda shape: pl.BlockSpec(shape, lambda bi, qi: (0,) * len(shape))
    return pl.pallas_call(
        functools.partial(_diff_body, n_ctx=n_ctx, tq=tq, scale=DIFF_HD ** -0.5, lambda_init=lambda_init),
        grid=(b, t // tq),
        in_specs=[kv_spec(G_DIFF_K), kv_spec(G_DIFF_KP), kv_spec(G_DIFF_V), q_spec(G_DIFF_Q), q_spec(G_DIFF_QP),
                  full((t, LANE)), full((t, LANE)),
                  pl.BlockSpec((tq, LANE), lambda bi, qi: (qi, 0)),
                  pl.BlockSpec((tq, LANE), lambda bi, qi: (qi, 0)),
                  full(lam_pack.shape), full(g_sub.shape)],
        out_specs=pl.BlockSpec((None, tq, GROUP), lambda bi, qi: (bi, qi, 0)),
        out_shape=jax.ShapeDtypeStruct((b, t, GROUP), BF16),
        scratch_shapes=[pltpu.VMEM((t, GROUP), BF16)],
        compiler_params=_cparams("parallel", "arbitrary"),
        name="diff_attn",
    )(p_all, p_all, p_all, p_all, p_all, cosd, sind, cosd, sind, lam_pack, g_sub)


def _chunk_of_step(s, ncc, nlc, reverse):
    if not reverse:
        return s
    return jnp.where(s < ncc, ncc - 1 - s, 2 * ncc + nlc - 1 - s)


def _lru_body(*refs, ncc, nlc, tc, reverse):
    if reverse:
        (xp_ref, x_ref, xn_ref, cw_ref, cb_ref, wax_ref, ba_ref, bx_ref, lam_ref, hf_ref, gate_ref,
         o_ref, xe_s, a_s, in_s, h_s) = refs
    else:
        (xp_ref, x_ref, xn_ref, cw_ref, cb_ref, wax_ref, ba_ref, bx_ref, lam_ref,
         o_ref, xe_s, a_s, in_s, h_s) = refs
    s = pl.program_id(0)
    c = _chunk_of_step(s, ncc, nlc, reverse)
    nb = x_ref.shape[1]
    bw = LRU_WIDTH // LRU_BLOCKS

    @pl.when(s == 0)
    def _():
        h_s[...] = jnp.zeros_like(h_s)

    has_prev = jnp.logical_and(c != 0, c != ncc).astype(F32)
    has_next = jnp.logical_and(c != ncc - 1, c != ncc + nlc - 1).astype(F32)
    xe_s[0:2] = xp_ref[...].astype(F32) * has_prev
    xe_s[2:tc + 2] = x_ref[...].astype(F32)
    xe_s[tc + 2:tc + 3] = xn_ref[...].astype(F32) * has_next

    for blk in range(LRU_BLOCKS):
        ls = slice(blk * bw, (blk + 1) * bw)
        xc = cb_ref[:, ls].reshape(1, 1, bw)
        for k in range(LRU_CONV):
            xc = xc + cw_ref[k:k + 1, ls].reshape(1, 1, bw) * xe_s[k:k + tc, :, ls]
        xc2 = xc.reshape(tc * nb, bw)
        gates = _dot(xc2.astype(BF16), wax_ref[blk])
        r = _sigmoid(gates[:, :bw] + ba_ref[:, ls])
        i = _sigmoid(gates[:, bw:] + bx_ref[:, ls])
        log_a = -LRU_C * r * _softplus(-lam_ref[:, ls])
        a = jnp.exp(log_a)
        a_s[:, :, ls] = a.reshape(tc, nb, bw)
        in_s[:, :, ls] = (jnp.sqrt(1.0 - a * a) * (i * xc2)).reshape(tc, nb, bw)

    def step(j, h):
        tt = tc - 1 - j if reverse else j
        h = a_s[tt] * h + in_s[tt]
        in_s[tt] = h
        return h

    h_s[...] = lax.fori_loop(0, tc, step, h_s[...], unroll=8)

    if reverse:
        o_ref[...] = ((hf_ref[...] + in_s[...]) * _gelu_tanh(gate_ref[...].astype(F32))).astype(BF16)
    else:
        o_ref[...] = in_s[...]


def lru_direction(x_t, gate_t, h_fwd, wts, n_ctx, tc, reverse):
    t, nb, w = x_t.shape
    ncc, nlc = n_ctx // tc, (t - n_ctx) // tc
    cw, cb, wax, ba, bx, lam = wts
    cidx = lambda s: _chunk_of_step(s, ncc, nlc, reverse)
    full = lambda shape: pl.BlockSpec(shape, lambda s: (0,) * len(shape))
    chunk = pl.BlockSpec((tc, nb, w), lambda s: (cidx(s), 0, 0))
    in_specs = [pl.BlockSpec((2, nb, w), lambda s: (jnp.maximum(cidx(s) * (tc // 2) - 1, 0), 0, 0)),
                chunk,
                pl.BlockSpec((1, nb, w), lambda s: (jnp.minimum((cidx(s) + 1) * tc, t - 1), 0, 0)),
                full(cw.shape), full(cb.shape), full(wax.shape), full(ba.shape), full(bx.shape), full(lam.shape)]
    args = [x_t, x_t, x_t, cw, cb, wax, ba, bx, lam]
    if reverse:
        in_specs += [chunk, chunk]
        args += [h_fwd, gate_t]
    return pl.pallas_call(
        functools.partial(_lru_body, ncc=ncc, nlc=nlc, tc=tc, reverse=reverse),
        grid=(ncc + nlc,),
        in_specs=in_specs,
        out_specs=chunk,
        out_shape=jax.ShapeDtypeStruct((t, nb, w), BF16 if reverse else F32),
        scratch_shapes=[pltpu.VMEM((tc + 3, nb, w), F32), pltpu.VMEM((tc, nb, w), F32),
                        pltpu.VMEM((tc, nb, w), F32), pltpu.VMEM((nb, w), F32)],
        compiler_params=_cparams("arbitrary"),
        name="lru_bwd" if reverse else "lru_fwd",
    )(*args)


def _ssd_body(*refs, ncc, nlc, reverse):
    if reverse:
        (xp_ref, x_ref, xn_ref, dtc_ref, dtr_ref, cw_ref, cb_ref, arow_ref, acol_ref, brow_ref, bcol_ref,
         yf_ref, z_ref, dskip_ref, gn_ref, o_ref, xe_s, st_s) = refs
    else:
        (xp_ref, x_ref, xn_ref, dtc_ref, dtr_ref, cw_ref, cb_ref, arow_ref, acol_ref, brow_ref, bcol_ref,
         o_ref, xe_s, st_s) = refs
    s = pl.program_id(1)
    c = _chunk_of_step(s, ncc, nlc, reverse)
    ln = SSD_CHUNK
    npair = SSD_HEADS // 2
    lane0 = SSD_HEADS if reverse else 0

    @pl.when(s == 0)
    def _():
        st_s[...] = jnp.zeros_like(st_s)

    has_prev = jnp.logical_and(c != 0, c != ncc).astype(F32)
    has_next = jnp.logical_and(c != ncc - 1, c != ncc + nlc - 1).astype(F32)
    xe_s[0:8] = xp_ref[...].astype(F32)[8:16] * has_prev
    xe_s[8:ln + 8] = x_ref[...].astype(F32)
    xe_s[ln + 8:ln + 16] = xn_ref[...].astype(F32)[0:8] * has_next
    xbc = cb_ref[...]
    for k in range(SSD_CONV):
        xbc = xbc + cw_ref[k:k + 1, :] * xe_s[pl.ds(6 + k, ln), :]
    xbc = _silu(xbc)

    dt_c = _softplus(dtc_ref[...].astype(F32)[:, 3 * LANE:4 * LANE] + brow_ref[...])
    da_c = dt_c * arow_ref[...]
    dt_r = _softplus(dtr_ref[...] + bcol_ref[...])
    da_r = dt_r * acol_ref[...]
    li = lax.broadcasted_iota(jnp.int32, (ln, ln), 0)
    mi = lax.broadcasted_iota(jnp.int32, (ln, ln), 1)
    keep = (li <= mi) if reverse else (li >= mi)
    keep_bf = jnp.where(keep, 1.0, 0.0).astype(BF16)
    keep_t_bf = jnp.where((li >= mi) if reverse else (li <= mi), 1.0, 0.0).astype(BF16)
    cs_c = sum(_dot(keep_bf, part) for part in _split3(da_c))
    cs_r = sum(_dot(part, keep_t_bf) for part in _split3(da_r))
    tot_c = jnp.sum(da_c, axis=0, keepdims=True)
    e_in = jnp.exp(cs_c)
    e_out = jnp.exp(tot_c - cs_c)
    e_tot = jnp.exp(tot_c)

    lane_first = lax.broadcasted_iota(jnp.int32, (1, LANE), 1) < SSD_HD
    row_first = lax.broadcasted_iota(jnp.int32, (LANE, 1), 0) < SSD_HD

    def col(v, h):
        return jnp.broadcast_to(v[:, lane0 + h:lane0 + h + 1], (v.shape[0], LANE))

    for pr in range(npair):
        g = pr // (npair // SSD_GROUPS)
        h0, h1 = 2 * pr, 2 * pr + 1
        bm = xbc[:, SSD_WIDTH + g * SSD_STATE:SSD_WIDTH + (g + 1) * SSD_STATE].astype(BF16)
        cm = xbc[:, SSD_WIDTH + (SSD_GROUPS + g) * SSD_STATE:SSD_WIDTH + (SSD_GROUPS + g + 1) * SSD_STATE].astype(BF16)
        xs = xbc[:, pr * LANE:(pr + 1) * LANE]
        xdt = xs * jnp.where(lane_first, col(dt_c, h0), col(dt_c, h1))
        cb = _dot_nt(cm, bm)
        y = None
        for h, sel in ((h0, lane_first), (h1, jnp.logical_not(lane_first))):
            dec = jnp.exp(jnp.where(keep, col(cs_c, h) - cs_r[h:h + 1, :], -jnp.inf))
            part = _dot((cb * dec).astype(BF16), jnp.where(sel, xdt, 0.0).astype(BF16))
            y = part if y is None else y + part
        st = st_s[pr]
        y_off = _dot_nt(cm, st.astype(BF16)) * jnp.where(lane_first, col(e_in, h0), col(e_in, h1))
        xw = (xdt * jnp.where(lane_first, col(e_out, h0), col(e_out, h1))).astype(BF16)
        new = _dot_tn(xw, bm)
        dec_st = jnp.where(row_first, e_tot[:, lane0 + h0:lane0 + h0 + 1], e_tot[:, lane0 + h1:lane0 + h1 + 1])
        st_s[pr] = st * dec_st + new
        y = y + y_off
        if reverse:
            y = y + yf_ref[:, pr * LANE:(pr + 1) * LANE] + xs * dskip_ref[:, pr * LANE:(pr + 1) * LANE]
            o_ref[:, pr * LANE:(pr + 1) * LANE] = y
        else:
            o_ref[:, pr * LANE:(pr + 1) * LANE] = y

    if reverse:
        yz = o_ref[...] * _silu(z_ref[...].astype(F32))
        gw = SSD_WIDTH // SSD_GROUPS
        for g in range(SSD_GROUPS):
            blk = yz[:, g * gw:(g + 1) * gw]
            blk = blk * lax.rsqrt(jnp.mean(blk * blk, axis=-1, keepdims=True) + LN_EPS) * gn_ref[:, g * gw:(g + 1) * gw]
            o_ref[:, g * gw:(g + 1) * gw] = blk


def ssd_direction(p_all, dt_t, y_fwd, wts, n_ctx, reverse):
    b, t, _ = p_all.shape
    ln = SSD_CHUNK
    ncc, nlc = n_ctx // ln, (t - n_ctx) // ln
    cw, cb, arow, acol, brow, bcol, dskip, gn = wts
    cidx = lambda s: _chunk_of_step(s, ncc, nlc, reverse)
    full = lambda a: pl.BlockSpec(a.shape, lambda bi, s: (0,) * a.ndim)
    xw = 2 * GROUP
    gx = G_SSD_XBC // 2
    hb = ln // 16
    in_specs = [pl.BlockSpec((None, 16, xw), lambda bi, s: (bi, jnp.maximum(cidx(s) * hb - 1, 0), gx)),
                pl.BlockSpec((None, ln, xw), lambda bi, s: (bi, cidx(s), gx)),
                pl.BlockSpec((None, 16, xw), lambda bi, s: (bi, jnp.minimum((cidx(s) + 1) * hb, t // 16 - 1), gx)),
                pl.BlockSpec((None, ln, GROUP), lambda bi, s: (bi, cidx(s), G_MLA_CQ)),
                pl.BlockSpec((None, SSD_HEADS, ln), lambda bi, s: (bi, 1 if reverse else 0, cidx(s))),
                full(cw), full(cb), full(arow), full(acol), full(brow), full(bcol)]
    args = [p_all, p_all, p_all, p_all, dt_t, cw, cb, arow, acol, brow, bcol]
    out_spec = pl.BlockSpec((None, ln, SSD_WIDTH), lambda bi, s: (bi, cidx(s), 0))
    if reverse:
        in_specs += [out_spec, pl.BlockSpec((None, ln, GROUP), lambda bi, s: (bi, cidx(s), G_SSD_Z)), full(dskip), full(gn)]
        args += [y_fwd, p_all, dskip, gn]
    return pl.pallas_call(
        functools.partial(_ssd_body, ncc=ncc, nlc=nlc, reverse=reverse),
        grid=(b, ncc + nlc),
        in_specs=in_specs,
        out_specs=out_spec,
        out_shape=jax.ShapeDtypeStruct((b, t, SSD_WIDTH), F32),
        scratch_shapes=[pltpu.VMEM((ln + 16, xw), F32), pltpu.VMEM((SSD_HEADS // 2, LANE, SSD_STATE), F32)],
        compiler_params=_cparams("parallel", "arbitrary"),
        name="ssd_bwd" if reverse else "ssd_fwd",
    )(*args)


MERGE_TM = 256


def _merge_body(x_ref, y0_ref, y1_ref, y2_ref, y3_ref, gt0_ref, gt1_ref, gt2_ref, gt3_ref, wb_ref, wo_ref, g1_ref, sh2_ref, sc2_ref,
                lng_ref, lnb_ref, rw_ref, rb_ref, x1_ref, v_ref, lg_ref):
    d = D_MODEL
    merged = None
    for k, (y_ref, gt_ref) in enumerate(((y0_ref, gt0_ref), (y1_ref, gt1_ref), (y2_ref, gt2_ref), (y3_ref, gt3_ref))):
        gate = _sigmoid(gt_ref[...].astype(F32))
        term = gate * _dot(y_ref[...].astype(BF16), wb_ref[k])
        merged = term if merged is None else merged + term
    m = _dot(merged.astype(BF16), wo_ref[...])
    r = DEEPNORM_ALPHA * x_ref[...] + g1_ref[...] * m
    rc = r - jnp.mean(r, axis=-1, keepdims=True)
    x1 = rc * lax.rsqrt(jnp.mean(rc * rc, axis=-1, keepdims=True) + LN_EPS) * lng_ref[...] + lnb_ref[...]
    x1_ref[...] = x1
    xc = x1 - jnp.mean(x1, axis=-1, keepdims=True)
    v = xc * lax.rsqrt(jnp.mean(xc * xc, axis=-1, keepdims=True) + LN_EPS) * (1.0 + sc2_ref[...]) + sh2_ref[...]
    v_ref[...] = v.astype(BF16)
    vh = v.astype(BF16)
    vl = (v - vh.astype(F32)).astype(BF16)
    lg_ref[...] = _dot(vh, rw_ref[0]) + _dot(vl, rw_ref[0]) + _dot(vh, rw_ref[1]) + rb_ref[...]


def merge_ln(x_all, ys, p_all, mod4, wts, n_ctx, row0):
    b, t, d = x_all.shape
    tm = MERGE_TM if (t % MERGE_TM == 0 and n_ctx % MERGE_TM == 0) else 128
    t0 = row0 // tm
    nt = t // tm - t0
    wb, wo, lng, lnb, rw, rb = wts
    nctx_t = n_ctx // tm
    row = lambda w, dt: pl.BlockSpec((None, tm, w), lambda bi, ti: (bi, ti + t0, 0))
    mod_spec = lambda k: pl.BlockSpec((None, None, 1, d),
                                      lambda bi, ti: (jnp.where(ti + t0 < nctx_t, b, bi), k, 0, 0))
    const = lambda a: pl.BlockSpec(a.shape, lambda bi, ti: (0,) * a.ndim, pipeline_mode=pl.Buffered(1))
    out_row = lambda w: pl.BlockSpec((None, tm, w), lambda bi, ti: (bi, ti, 0))
    gate_spec = lambda k: pl.BlockSpec((None, tm, d), lambda bi, ti: (bi, ti + t0, G_GATES * GROUP // d + k))
    return pl.pallas_call(
        _merge_body,
        grid=(b, nt),
        in_specs=[row(d, F32), row(GROUP, BF16), row(GROUP, BF16), row(GROUP, BF16), row(GROUP, F32),
                  gate_spec(0), gate_spec(1), gate_spec(2), gate_spec(3),
                  const(wb), const(wo), mod_spec(2), mod_spec(3), mod_spec(4),
                  const(lng), const(lnb), const(rw), const(rb)],
        out_specs=[out_row(d), out_row(d), out_row(LANE)],
        out_shape=[jax.ShapeDtypeStruct((b, nt * tm, d), F32), jax.ShapeDtypeStruct((b, nt * tm, d), BF16),
                   jax.ShapeDtypeStruct((b, nt * tm, LANE), F32)],
        compiler_params=_cparams("parallel", "parallel"),
        name="merge_ln",
    )(x_all, *ys, p_all, p_all, p_all, p_all, wb, wo, mod4, mod4, mod4, lng, lnb, rw, rb)


COMBINE_TT = 128


def _combine_body(slot_ref, y_hbm, x1_ref, w_ref, g2_ref, lng_ref, lnb_ref, o_ref, buf, sem):
    tt = x1_ref.shape[0]
    n_rows = TOP_K * tt

    def issue(a, carry):
        pltpu.make_async_copy(y_hbm.at[pl.ds(slot_ref[0, 0, a], 1)], buf.at[pl.ds(a, 1)], sem).start()
        return carry

    lax.fori_loop(0, n_rows, issue, 0, unroll=8)
    pltpu.make_async_copy(y_hbm.at[pl.ds(0, n_rows)], buf, sem).wait()
    f = None
    for k in range(TOP_K):
        term = w_ref[:, k:k + 1] * buf[k * tt:(k + 1) * tt, :]
        f = term if f is None else f + term
    r = DEEPNORM_ALPHA * x1_ref[...] + g2_ref[...] * f
    rc = r - jnp.mean(r, axis=-1, keepdims=True)
    o_ref[...] = rc * lax.rsqrt(jnp.mean(rc * rc, axis=-1, keepdims=True) + LN_EPS) * lng_ref[...] + lnb_ref[...]


def combine_ln(x1, y, slots, top_w, mod4, lng, lnb, n_ctx_rows):
    b, t, d = x1.shape
    tt = COMBINE_TT
    nt = t // tt
    nctx_t = n_ctx_rows // tt
    slot_tiles = slots.reshape(b * nt, tt, TOP_K).transpose(0, 2, 1).reshape(b * nt, 1, TOP_K * tt)
    row = pl.BlockSpec((None, tt, d), lambda bi, ti: (bi, ti, 0))
    const = lambda a: pl.BlockSpec(a.shape, lambda bi, ti: (0,) * a.ndim)
    return pl.pallas_call(
        _combine_body,
        grid=(b, nt),
        in_specs=[pl.BlockSpec((1, 1, TOP_K * tt), lambda bi, ti: (bi * nt + ti, 0, 0), memory_space=pltpu.SMEM),
                  pl.BlockSpec(memory_space=pl.ANY),
                  row,
                  pl.BlockSpec((None, tt, TOP_K), lambda bi, ti: (bi, ti, 0)),
                  pl.BlockSpec((None, None, 1, d), lambda bi, ti: (jnp.where(ti < nctx_t, b, bi), 5, 0, 0)),
                  const(lng), const(lnb)],
        out_specs=row,
        out_shape=jax.ShapeDtypeStruct((b, t, d), F32),
        scratch_shapes=[pltpu.VMEM((TOP_K * tt, d), F32), pltpu.SemaphoreType.DMA(())],
        compiler_params=_cparams("parallel", "parallel"),
        name="combine_ln",
    )(slot_tiles, y, x1, top_w.reshape(b, t, TOP_K), mod4, lng, lnb)


MOE_UP_TC = 1024
MOE_DOWN_TN = 1024
SEL_W = 512


def _refresh(be_ref, i):
    return jnp.logical_or(i == 0, be_ref[i] != be_ref[jnp.maximum(i - 1, 0)])


def _moe_up_body(be_ref, nvalid_ref, x_ref, w1_ref, sel_ref, b1_ref, h_ref, wg_s, wl_s, bg_s, bl_s):
    i = pl.program_id(1)
    half = SEL_W // 2

    @pl.when(_refresh(be_ref, i))
    def _():
        for c in range(MOE_UP_TC // SEL_W):
            both = _dot(w1_ref[:, c * SEL_W:(c + 1) * SEL_W].astype(BF16), sel_ref[...])
            wg_s[:, c * half:(c + 1) * half] = both[:, :half].astype(BF16)
            wl_s[:, c * half:(c + 1) * half] = both[:, half:].astype(BF16)
            b_hi, b_mid, b_lo = _split3(jnp.broadcast_to(b1_ref[:, c * SEL_W:(c + 1) * SEL_W], (8, SEL_W)))
            bias = _dot(b_hi, sel_ref[...]) + _dot(b_mid, sel_ref[...]) + _dot(b_lo, sel_ref[...])
            bg_s[:, c * half:(c + 1) * half] = bias[:, :half]
            bl_s[:, c * half:(c + 1) * half] = bias[:, half:]

    @pl.when(i >= nvalid_ref[0])
    def _():
        h_ref[...] = jnp.zeros_like(h_ref)

    @pl.when(i < nvalid_ref[0])
    def _():
        x = x_ref[...]
        glu = jnp.minimum(_dot(x, wg_s[...]) + bg_s[0:1, :], SWIGLU_LIMIT)
        lin = jnp.clip(_dot(x, wl_s[...]) + bl_s[0:1, :], -SWIGLU_LIMIT, SWIGLU_LIMIT)
        h_ref[...] = (glu * _sigmoid(SWIGLU_ALPHA * glu) * (lin + 1.0)).astype(BF16)


def _moe_down_body(be_ref, nvalid_ref, h_ref, w2_ref, b2_ref, y_ref, w_s):
    i = pl.program_id(1)

    @pl.when(_refresh(be_ref, i))
    def _():
        w_s[...] = w2_ref[...].astype(BF16)

    @pl.when(i >= nvalid_ref[0])
    def _():
        y_ref[...] = jnp.zeros_like(y_ref)

    @pl.when(i < nvalid_ref[0])
    def _():
        y_ref[...] = _dot(h_ref[...], w_s[...]) + b2_ref[...]


def moe_expert_blocks(x_sorted, block_e, n_valid, w1, b1, w2, b2, layer):
    n_slots, d = x_sorted.shape
    nb = n_slots // MOE_TM
    dh = w2.shape[2]
    half = SEL_W // 2
    src = jnp.arange(SEL_W)[:, None]
    dst = jnp.arange(SEL_W)[None, :]
    sel = (src == jnp.where(dst < half, 2 * dst, 2 * (dst - half) + 1)).astype(BF16)
    hidden = pl.pallas_call(
        _moe_up_body,
        grid_spec=pltpu.PrefetchScalarGridSpec(
            num_scalar_prefetch=2,
            grid=(2 * dh // MOE_UP_TC, nb),
            in_specs=[
                pl.BlockSpec((MOE_TM, d), lambda j, i, be, nv: (i, 0)),
                pl.BlockSpec((None, None, d, MOE_UP_TC), lambda j, i, be, nv: (layer, be[i], 0, j)),
                pl.BlockSpec((SEL_W, SEL_W), lambda j, i, be, nv: (0, 0)),
                pl.BlockSpec((None, None, 1, MOE_UP_TC), lambda j, i, be, nv: (layer, be[i], 0, j)),
            ],
            out_specs=pl.BlockSpec((MOE_TM, MOE_UP_TC // 2), lambda j, i, be, nv: (i, j)),
            scratch_shapes=[pltpu.VMEM((d, MOE_UP_TC // 2), BF16), pltpu.VMEM((d, MOE_UP_TC // 2), BF16),
                            pltpu.VMEM((8, MOE_UP_TC // 2), F32), pltpu.VMEM((8, MOE_UP_TC // 2), F32)],
        ),
        out_shape=jax.ShapeDtypeStruct((n_slots, dh), BF16),
        compiler_params=_cparams("arbitrary", "arbitrary"),
        name="moe_up",
    )(block_e, n_valid, x_sorted, w1, sel, b1.reshape(b1.shape[0], b1.shape[1], 1, b1.shape[2]))
    return pl.pallas_call(
        _moe_down_body,
        grid_spec=pltpu.PrefetchScalarGridSpec(
            num_scalar_prefetch=2,
            grid=(d // MOE_DOWN_TN, nb),
            in_specs=[
                pl.BlockSpec((MOE_TM, dh), lambda j, i, be, nv: (i, 0)),
                pl.BlockSpec((None, None, dh, MOE_DOWN_TN), lambda j, i, be, nv: (layer, be[i], 0, j)),
                pl.BlockSpec((None, None, 1, MOE_DOWN_TN), lambda j, i, be, nv: (layer, be[i], 0, j)),
            ],
            out_specs=pl.BlockSpec((MOE_TM, MOE_DOWN_TN), lambda j, i, be, nv: (i, j)),
            scratch_shapes=[pltpu.VMEM((dh, MOE_DOWN_TN), BF16)],
        ),
        out_shape=jax.ShapeDtypeStruct((n_slots, d), F32),
        compiler_params=_cparams("arbitrary", "arbitrary"),
        name="moe_down",
    )(block_e, n_valid, hidden, w2, b2.reshape(b2.shape[0], b2.shape[1], 1, b2.shape[2]))


def moe_ffn(h, logits, w1, b1, w2, b2, layer):
    t, d = h.shape
    top_logit, top_idx = lax.top_k(logits, TOP_K)
    top_w = jax.nn.softmax(top_logit, axis=-1)
    n_assign = t * TOP_K
    flat_e = top_idx.reshape(-1)
    onehot = (flat_e[:, None] == jnp.arange(N_EXPERTS, dtype=flat_e.dtype)[None, :]).astype(jnp.int32)
    csum = jnp.cumsum(onehot, axis=0)
    counts = csum[-1]
    padded = (counts + MOE_TM - 1) // MOE_TM * MOE_TM
    pend = jnp.cumsum(padded)
    pstart = pend - padded
    dest = (jnp.sum(onehot * (csum + pstart[None, :]), axis=1) - 1).astype(jnp.int32)
    n_blocks = (n_assign + N_EXPERTS * (MOE_TM - 1) + MOE_TM - 1) // MOE_TM
    n_slots = n_blocks * MOE_TM
    slot_tok = jnp.zeros((n_slots,), jnp.int32).at[dest].set(jnp.arange(n_assign, dtype=jnp.int32) // TOP_K)
    n_valid = (pend[-1] // MOE_TM).astype(jnp.int32)
    blk_row = jnp.minimum(jnp.arange(n_blocks), n_valid - 1) * MOE_TM
    block_e = jnp.minimum(jnp.sum((pend[None, :] <= blk_row[:, None]).astype(jnp.int32), axis=1), N_EXPERTS - 1)
    x_sorted = h[slot_tok]
    y = moe_expert_blocks(x_sorted, block_e, n_valid.reshape(1), w1, b1, w2, b2, layer)
    return y, dest.reshape(t, TOP_K), top_w


def _rot_partner(w, width=64):
    d, n = w.shape
    g = w.reshape(d, n // width, 2, width // 2)
    return jnp.stack([-g[:, :, 1], g[:, :, 0]], axis=2).reshape(d, n)


def _axial_angles(n_tok, rot_dim):
    rows = n_tok // GRID_W
    row = jnp.repeat(jnp.arange(rows), GRID_W).astype(F32)
    col = jnp.tile(jnp.arange(GRID_W), rows).astype(F32)
    n_freq = rot_dim // 4
    inv = jnp.power(ROPE_BASE, -jnp.arange(n_freq, dtype=F32) / n_freq)
    return jnp.concatenate([row[:, None] * inv, col[:, None] * inv], axis=-1)


def _rope_tables(n_ctx, n_lat, rot_dim, n_groups):
    ang = _axial_angles(n_lat, rot_dim)
    cos = jnp.concatenate([jnp.ones((n_ctx, rot_dim // 2), F32), jnp.cos(ang)], axis=0)
    sin = jnp.concatenate([jnp.zeros((n_ctx, rot_dim // 2), F32), jnp.sin(ang)], axis=0)
    t = n_ctx + n_lat
    pad = LANE - n_groups * rot_dim
    cos = jnp.concatenate([cos] * (2 * n_groups) + [jnp.ones((t, pad), F32)], axis=1)
    sin = jnp.concatenate([sin] * (2 * n_groups) + [jnp.zeros((t, pad), F32)], axis=1)
    return cos, sin


def _layer_weights(i, a):
    d = D_MODEL
    w_in = a['w_in'][i]
    z = lambda n: jnp.zeros((d, n), F32)
    seg = lambda c0, n: w_in[:, c0:c0 + n]
    kr = seg(C_MLA_KR, MLA_ROPE)
    groups = [
        seg(C_MLA_CKV, MLA_KV_LORA), kr, z(64), _rot_partner(kr), z(64),
        seg(C_DIFF_K, 512), _rot_partner(seg(C_DIFF_K, 512)), seg(C_DIFF_V, 512),
        seg(C_SSD_XBC, SSD_XBC_W), seg(C_LRU_X, LRU_WIDTH),
        seg(C_MLA_CQ, MLA_Q_LORA), seg(C_SSD_DT, 2 * SSD_HEADS), z(LANE - 2 * SSD_HEADS),
        seg(C_DIFF_Q, 512), _rot_partner(seg(C_DIFF_Q, 512)),
        seg(C_LRU_GATE, LRU_WIDTH), seg(C_SSD_Z, SSD_WIDTH), seg(C_GATES, N_BRANCH * d),
    ]
    w_in_r = jnp.concatenate(groups, axis=1).astype(BF16)

    w_uq = a['mla_w_uq'][i].reshape(MLA_Q_LORA, MLA_HEADS, MLA_NOPE + MLA_ROPE)
    w_ukv = a['mla_w_ukv'][i].reshape(MLA_KV_LORA, MLA_HEADS, MLA_NOPE + MLA_V)
    zq = jnp.zeros((MLA_Q_LORA, 64), F32)
    wq = jnp.stack([jnp.concatenate([w_uq[:, h, :], zq], axis=1) for h in range(MLA_HEADS)]).astype(BF16)
    wqp = jnp.stack([jnp.concatenate([_rot_partner(w_uq[:, h, MLA_NOPE:]), zq], axis=1)
                     for h in range(MLA_HEADS)]).astype(BF16)
    wuk = jnp.transpose(w_ukv[:, :, :MLA_NOPE], (1, 0, 2)).astype(BF16)
    wuv = jnp.transpose(w_ukv[:, :, MLA_NOPE:], (1, 0, 2)).astype(BF16)
    mla = (a['mla_g_kv'][i].reshape(1, -1), a['mla_g_q'][i].reshape(1, -1), wuk, wuv, wq, wqp)

    lam_pack = jnp.zeros((8, LANE), F32)
    for r, name in enumerate(('diff_lq1', 'diff_lk1', 'diff_lq2', 'diff_lk2')):
        lam_pack = lam_pack.at[r, :DIFF_HD].set(a[name][i])
    g_sub = a['diff_g_sub'][i].reshape(1, -1)

    lru = []
    for dd in range(2):
        wax = jnp.concatenate([a['lru_w_a'][i, dd], a['lru_w_x'][i, dd]], axis=-1).astype(BF16)
        lru.append((a['lru_conv_w'][i], a['lru_conv_b'][i].reshape(1, -1), wax,
                    a['lru_b_a'][i, dd].reshape(1, -1), a['lru_b_x'][i, dd].reshape(1, -1),
                    a['lru_lam'][i, dd].reshape(1, -1)))

    ssd = []
    pad_row = lambda v: jnp.zeros((1, LANE), F32).at[0, :SSD_HEADS].set(v)
    for dd in range(2):
        a_neg = -jnp.exp(a['ssd_a_log'][i, dd])
        bias = a['ssd_dt_bias'][i, dd]
        arow = jnp.zeros((1, LANE), F32).at[0, dd * SSD_HEADS:(dd + 1) * SSD_HEADS].set(a_neg)
        brow = jnp.zeros((1, LANE), F32).at[0, dd * SSD_HEADS:(dd + 1) * SSD_HEADS].set(bias)
        ssd.append((a['ssd_conv_w'][i], a['ssd_conv_b'][i].reshape(1, -1), arow, a_neg.reshape(-1, 1),
                    brow, bias.reshape(-1, 1),
                    jnp.repeat(a['ssd_d'][i], SSD_HD).reshape(1, -1), a['ssd_g_norm'][i].reshape(1, -1), dd))

    rw = jnp.pad(a['router_w'][i], ((0, 0), (0, LANE - N_EXPERTS)))
    rwh = rw.astype(BF16)
    rwl = (rw - rwh.astype(F32)).astype(BF16)
    rb = jnp.pad(a['router_b'][i], (0, LANE - N_EXPERTS)).reshape(1, -1)
    merge = (a['w_branch'][i].astype(BF16), a['w_out'][i].astype(BF16),
             a['ln1_g'][i].reshape(1, -1), a['ln1_b'][i].reshape(1, -1), jnp.stack([rwh, rwl]), rb)
    return dict(w_in_r=w_in_r, mla=mla, lam_pack=lam_pack, g_sub=g_sub, lru=lru, ssd=ssd, merge=merge)


def _token_mixers(p_all, lw, tabs_mla, tabs_diff, layer_idx, n_ctx):
    b, t, _ = p_all.shape
    tq = 256 if (t % 256 == 0 and n_ctx % 256 == 0) else 128
    y_mla = mla_attn(p_all, tabs_mla, lw['mla'], n_ctx, tq)
    lambda_init = 0.8 - 0.6 * math.exp(-0.3 * layer_idx)
    y_diff = diff_attn(p_all, tabs_diff, lw['lam_pack'], lw['g_sub'], lambda_init, n_ctx, tq)

    x_t = jnp.transpose(p_all[:, :, G_LRU_X * GROUP:(G_LRU_X + 1) * GROUP], (1, 0, 2))
    g_t = jnp.transpose(p_all[:, :, G_LRU_GATE * GROUP:(G_LRU_GATE + 1) * GROUP], (1, 0, 2))
    tc = 256 if (t % 256 == 0 and n_ctx % 256 == 0) else 128
    h_f = lru_direction(x_t, None, None, lw['lru'][0], n_ctx, tc, False)
    y_lru = jnp.transpose(lru_direction(x_t, g_t, h_f, lw['lru'][1], n_ctx, tc, True), (1, 0, 2))

    dt0 = G_MLA_CQ * GROUP + MLA_Q_LORA
    dt_t = jnp.transpose(p_all[:, :, dt0:dt0 + 2 * SSD_HEADS].astype(F32), (0, 2, 1))
    ssd_f, ssd_b = lw['ssd']
    y_f = ssd_direction(p_all, dt_t, None, ssd_f[:8], n_ctx, False)
    y_ssd = ssd_direction(p_all, dt_t, y_f, ssd_b[:8], n_ctx, True)
    return y_mla, y_diff, y_lru, y_ssd


def kernel(x, c, ctx, c_ctx, w_mod, b_mod, w_in, mla_g_q, mla_g_kv, mla_w_uq, mla_w_ukv, diff_lq1, diff_lk1, diff_lq2, diff_lk2, diff_g_sub, lru_conv_w, lru_conv_b, lru_w_a, lru_b_a, lru_w_x, lru_b_x, lru_lam, ssd_conv_w, ssd_conv_b, ssd_a_log, ssd_dt_bias, ssd_d, ssd_g_norm, w_branch, w_out, ln1_g, ln1_b, ln2_g, ln2_b, router_w, router_b, w1, b1, w2, b2):
    a = dict(w_in=w_in, mla_g_q=mla_g_q, mla_g_kv=mla_g_kv, mla_w_uq=mla_w_uq, mla_w_ukv=mla_w_ukv,
             diff_lq1=diff_lq1, diff_lk1=diff_lk1, diff_lq2=diff_lq2, diff_lk2=diff_lk2, diff_g_sub=diff_g_sub,
             lru_conv_w=lru_conv_w, lru_conv_b=lru_conv_b, lru_w_a=lru_w_a, lru_b_a=lru_b_a, lru_w_x=lru_w_x,
             lru_b_x=lru_b_x, lru_lam=lru_lam, ssd_conv_w=ssd_conv_w, ssd_conv_b=ssd_conv_b, ssd_a_log=ssd_a_log,
             ssd_dt_bias=ssd_dt_bias, ssd_d=ssd_d, ssd_g_norm=ssd_g_norm, w_branch=w_branch, w_out=w_out,
             ln1_g=ln1_g, ln1_b=ln1_b, router_w=router_w, router_b=router_b)
    bsz, n_lat, d = x.shape
    n_ctx = ctx.shape[1]
    t = n_ctx + n_lat
    depth = w_in.shape[0]
    tabs_mla = _rope_tables(n_ctx, n_lat, MLA_ROPE, 1)
    tabs_diff = _rope_tables(n_ctx, n_lat, DIFF_HD, 2)
    x_all = jnp.concatenate([ctx, x], axis=1)
    cond = jnp.zeros((16, d), F32).at[:bsz].set(jax.nn.silu(c)).at[bsz].set(jax.nn.silu(c_ctx))

    for i in range(depth):
        last = i == depth - 1
        lw = _layer_weights(i, a)
        mod4 = matmul_bias(cond, w_mod[i], b_mod[i]).reshape(16, 6, 1, d)
        p_all = ln_mod_proj(x_all, mod4, lw['w_in_r'], n_ctx)
        ys = _token_mixers(p_all, lw, tabs_mla, tabs_diff, i, n_ctx)
        row0 = n_ctx if last else 0
        x1, v, logits = merge_ln(x_all, ys, p_all, mod4, lw['merge'], n_ctx, row0)
        tr = t - row0
        y, slots, top_w = moe_ffn(v.reshape(bsz * tr, d), logits.reshape(bsz * tr, LANE)[:, :N_EXPERTS],
                                  w1, b1, w2, b2, i)
        x_all = combine_ln(x1, y, slots, top_w, mod4, ln2_g[i].reshape(1, -1), ln2_b[i].reshape(1, -1),
                           n_ctx - row0)
    return x_all
```

```python
import functools
import math

import numpy as np
import jax
import jax.numpy as jnp
from jax import lax
from jax.experimental import pallas as pl
from jax.experimental.pallas import tpu as pltpu

D_MODEL = 2048
DEPTH = 2
GRID_W = 64
ROPE_BASE = 10000.0

MLA_HEADS = 4
MLA_Q_LORA = 384
MLA_KV_LORA = 256
MLA_NOPE = 128
MLA_ROPE = 64
MLA_V = 128
DIFF_HEADS = 4
DIFF_HD = 64
LRU_WIDTH = 512
LRU_BLOCKS = 4
LRU_CONV = 4
LRU_C = 8.0
SSD_HEADS = 8
SSD_HD = 64
SSD_WIDTH = SSD_HEADS * SSD_HD
SSD_GROUPS = 2
SSD_STATE = 128
SSD_CONV = 4
SSD_CHUNK = 128
SSD_XBC_W = SSD_WIDTH + 2 * SSD_GROUPS * SSD_STATE
N_BRANCH = 4
BRANCH_WIDTH = 512
N_EXPERTS = 32
TOP_K = 4
D_EXPERT = D_MODEL
SWIGLU_LIMIT = 7.0
SWIGLU_ALPHA = 1.702
DEEPNORM_ALPHA = (2 * DEPTH) ** 0.25
LN_EPS = 1e-6

C_MLA_CKV = 0
C_MLA_KR = C_MLA_CKV + MLA_KV_LORA
C_DIFF_K = C_MLA_KR + MLA_ROPE
C_DIFF_V = C_DIFF_K + 2 * DIFF_HEADS * DIFF_HD
C_LRU_X = C_DIFF_V + DIFF_HEADS * 2 * DIFF_HD
C_SSD_XBC = C_LRU_X + LRU_WIDTH
C_SSD_DT = C_SSD_XBC + SSD_XBC_W
N_STATE_COLS = C_SSD_DT + 2 * SSD_HEADS
C_MLA_CQ = N_STATE_COLS
C_DIFF_Q = C_MLA_CQ + MLA_Q_LORA
C_LRU_GATE = C_DIFF_Q + 2 * DIFF_HEADS * DIFF_HD
C_SSD_Z = C_LRU_GATE + LRU_WIDTH
C_GATES = C_SSD_Z + SSD_WIDTH

GROUP = 512
G_MLA_KV = 0
G_DIFF_K = 1
G_DIFF_KP = 2
G_DIFF_V = 3
G_SSD_XBC = 4
G_LRU_X = 6
G_MLA_CQ = 7
G_DIFF_Q = 8
G_DIFF_QP = 9
G_LRU_GATE = 10
G_SSD_Z = 11
G_GATES = 12
N_GROUPS = G_GATES + N_BRANCH * D_MODEL // GROUP
NP_COLS = N_GROUPS * GROUP

LANE = 128
V7X_VMEM_LIMIT_BYTES = 48 * 1024 * 1024

MOE_TM = 512

BF16 = jnp.bfloat16
F32 = jnp.float32


def _cparams(*sem):
    return pltpu.CompilerParams(dimension_semantics=sem, vmem_limit_bytes=V7X_VMEM_LIMIT_BYTES)


def _dot(a, b):
    return jnp.dot(a, b, preferred_element_type=F32)


def _dot_nt(a, b):
    return lax.dot_general(a, b, (((1,), (1,)), ((), ())), preferred_element_type=F32)


def _dot_tn(a, b):
    return lax.dot_general(a, b, (((0,), (0,)), ((), ())), preferred_element_type=F32)


def _sigmoid(x):
    return 1.0 / (1.0 + jnp.exp(-x))


def _silu(x):
    return x * _sigmoid(x)


def _softplus(x):
    return jnp.maximum(x, 0.0) + jnp.log1p(jnp.exp(-jnp.abs(x)))


def _gelu_tanh(x):
    return 0.5 * x * (1.0 + jnp.tanh(math.sqrt(2.0 / math.pi) * (x + 0.044715 * (x * x * x))))


def _split3(x):
    hi = x.astype(BF16)
    r1 = x - hi.astype(F32)
    mid = r1.astype(BF16)
    lo = (r1 - mid.astype(F32)).astype(BF16)
    return hi, mid, lo


def _mm_body(a_ref, b_ref, bias_ref, o_ref):
    o_ref[...] = (_dot(a_ref[...], b_ref[...]) + bias_ref[...]).astype(o_ref.dtype)


def matmul_bias(a, b, bias, tn=512):
    m, k = a.shape
    _, n = b.shape
    return pl.pallas_call(
        _mm_body,
        grid=(n // tn,),
        in_specs=[pl.BlockSpec((m, k), lambda j: (0, 0)),
                  pl.BlockSpec((k, tn), lambda j: (0, j)),
                  pl.BlockSpec((1, tn), lambda j: (0, j))],
        out_specs=pl.BlockSpec((m, tn), lambda j: (0, j)),
        out_shape=jax.ShapeDtypeStruct((m, n), F32),
        compiler_params=_cparams("parallel"),
        name="mod_matmul",
    )(a.astype(BF16), b.astype(BF16), bias.reshape(1, n))


PROJ_TM = 768
PROJ_TN = 1024


def _proj_body(x_ref, shc_ref, scc_ref, shl_ref, scl_ref, w_ref, o_ref, u_ref, *, n_ctx, tm):
    t = pl.program_id(1)

    @pl.when(pl.program_id(2) == 0)
    def _():
        x = x_ref[...]
        xc = x - jnp.mean(x, axis=-1, keepdims=True)
        y = xc * lax.rsqrt(jnp.mean(xc * xc, axis=-1, keepdims=True) + LN_EPS)
        is_ctx = lax.broadcasted_iota(jnp.int32, (tm, 1), 0) + t * tm < n_ctx
        sh = jnp.where(is_ctx, shc_ref[...], shl_ref[...])
        sc = jnp.where(is_ctx, scc_ref[...], scl_ref[...])
        u_ref[...] = (y * (1.0 + sc) + sh).astype(BF16)

    o_ref[...] = _dot(u_ref[...], w_ref[...]).astype(BF16)


def ln_mod_proj(x_all, mod4, w_in_r, n_ctx):
    b, t, d = x_all.shape
    tm = PROJ_TM if t % PROJ_TM == 0 else 128
    npc = w_in_r.shape[1]
    mod_spec = lambda row_fn, k: pl.BlockSpec((None, None, 1, d), lambda bi, ti, j: (row_fn(bi), k, 0, 0))
    return pl.pallas_call(
        functools.partial(_proj_body, n_ctx=n_ctx, tm=tm),
        grid=(b, t // tm, npc // PROJ_TN),
        in_specs=[pl.BlockSpec((None, tm, d), lambda bi, ti, j: (bi, ti, 0)),
                  mod_spec(lambda bi: b, 0), mod_spec(lambda bi: b, 1),
                  mod_spec(lambda bi: bi, 0), mod_spec(lambda bi: bi, 1),
                  pl.BlockSpec((d, PROJ_TN), lambda bi, ti, j: (0, j))],
        out_specs=pl.BlockSpec((None, tm, PROJ_TN), lambda bi, ti, j: (bi, ti, j)),
        out_shape=jax.ShapeDtypeStruct((b, t, npc), BF16),
        scratch_shapes=[pltpu.VMEM((tm, d), BF16)],
        compiler_params=_cparams("parallel", "parallel", "arbitrary"),
        name="ln_mod_proj",
    )(x_all, mod4, mod4, mod4, mod4, w_in_r)


def _mla_body(kvsrc_ref, cq_ref, cosk_ref, sink_ref, cosq_ref, sinq_ref, gkv_ref, gq_ref,
              wuk_ref, wuv_ref, wq_ref, wqp_ref, o_ref, k_s, v_s, *, n_ctx, tq, scale):
    qi = pl.program_id(1)
    t_all = k_s.shape[1]

    @pl.when(qi == 0)
    def _():
        src = kvsrc_ref[...].astype(F32)
        ckv = src[:, :MLA_KV_LORA]
        ckv = ckv * lax.rsqrt(jnp.mean(ckv * ckv, axis=-1, keepdims=True) + LN_EPS) * gkv_ref[...]
        ckv = ckv.astype(BF16)
        k_rope = (src[:, 256:384] * cosk_ref[...] + src[:, 384:512] * sink_ref[...]).astype(BF16)
        for h in range(MLA_HEADS):
            k_s[h, :, 0:LANE] = _dot(ckv, wuk_ref[h]).astype(BF16)
            k_s[h, :, LANE:2 * LANE] = k_rope
            v_s[h] = _dot(ckv, wuv_ref[h]).astype(BF16)

    cq = cq_ref[...].astype(F32)[:, :MLA_Q_LORA]
    cq = cq * lax.rsqrt(jnp.mean(cq * cq, axis=-1, keepdims=True) + LN_EPS) * gq_ref[...]
    cq = cq.astype(BF16)
    cosq = cosq_ref[...]
    sinq = sinq_ref[...]

    def attend(nk):
        for h in range(MLA_HEADS):
            qh = _dot(cq, wq_ref[h])
            qp = _dot(cq, wqp_ref[h])
            q_rope = qh[:, LANE:] * cosq + qp * sinq
            q = (jnp.concatenate([qh[:, :LANE], q_rope], axis=-1) * scale).astype(BF16)
            s = _dot_nt(q, k_s[h, 0:nk, :])
            p = jnp.exp(s - jnp.max(s, axis=-1, keepdims=True))
            l = jnp.sum(p, axis=-1, keepdims=True)
            o = _dot(p.astype(BF16), v_s[h, 0:nk, :])
            o_ref[:, h * MLA_V:(h + 1) * MLA_V] = (o * (1.0 / l)).astype(BF16)

    @pl.when(qi * tq < n_ctx)
    def _():
        attend(n_ctx)

    @pl.when(qi * tq >= n_ctx)
    def _():
        attend(t_all)


def mla_attn(p_all, tabs, wts, n_ctx, tq):
    b, t, _ = p_all.shape
    cosk, sink = tabs
    gkv, gq, wuk, wuv, wq, wqp = wts
    full = lambda shape: pl.BlockSpec(shape, lambda bi, qi: (0,) * len(shape))
    return pl.pallas_call(
        functools.partial(_mla_body, n_ctx=n_ctx, tq=tq, scale=(MLA_NOPE + MLA_ROPE) ** -0.5),
        grid=(b, t // tq),
        in_specs=[pl.BlockSpec((None, t, GROUP), lambda bi, qi: (bi, 0, G_MLA_KV)),
                  pl.BlockSpec((None, tq, GROUP), lambda bi, qi: (bi, qi, G_MLA_CQ)),
                  full((t, LANE)), full((t, LANE)),
                  pl.BlockSpec((tq, LANE), lambda bi, qi: (qi, 0)),
                  pl.BlockSpec((tq, LANE), lambda bi, qi: (qi, 0)),
                  full(gkv.shape), full(gq.shape), full(wuk.shape), full(wuv.shape), full(wq.shape), full(wqp.shape)],
        out_specs=pl.BlockSpec((None, tq, MLA_HEADS * MLA_V), lambda bi, qi: (bi, qi, 0)),
        out_shape=jax.ShapeDtypeStruct((b, t, MLA_HEADS * MLA_V), BF16),
        scratch_shapes=[pltpu.VMEM((MLA_HEADS, t, 2 * LANE), BF16), pltpu.VMEM((MLA_HEADS, t, MLA_V), BF16)],
        compiler_params=_cparams("parallel", "arbitrary"),
        name="mla_attn",
    )(p_all, p_all, cosk, sink, cosk, sink, gkv, gq, wuk, wuv, wq, wqp)


def _diff_body(k_ref, kp_ref, v_ref, q_ref, qp_ref, cosk_ref, sink_ref, cosq_ref, sinq_ref, lam_ref, gsub_ref,
               o_ref, k_s, *, n_ctx, tq, scale, lambda_init):
    qi = pl.program_id(1)
    t_all = k_s.shape[0]
    nrep = GROUP // LANE

    @pl.when(qi == 0)
    def _():
        cosk = jnp.concatenate([cosk_ref[...]] * nrep, axis=-1)
        sink = jnp.concatenate([sink_ref[...]] * nrep, axis=-1)
        k_s[...] = (k_ref[...].astype(F32) * cosk + kp_ref[...].astype(F32) * sink).astype(BF16)

    cosq = jnp.concatenate([cosq_ref[...]] * nrep, axis=-1)
    sinq = jnp.concatenate([sinq_ref[...]] * nrep, axis=-1)
    q = (q_ref[...].astype(F32) * cosq + qp_ref[...].astype(F32) * sinq) * scale
    lp = lam_ref[...]
    lam = (jnp.exp(jnp.sum(lp[0:1] * lp[1:2], axis=-1, keepdims=True))
           - jnp.exp(jnp.sum(lp[2:3] * lp[3:4], axis=-1, keepdims=True)) + lambda_init)
    first = lax.broadcasted_iota(jnp.int32, (1, LANE), 1) < DIFF_HD

    def softmax_pv(qm, kh, vh):
        s = _dot_nt(qm, kh)
        p = jnp.exp(s - jnp.max(s, axis=-1, keepdims=True))
        l = jnp.sum(p, axis=-1, keepdims=True)
        return _dot(p.astype(BF16), vh) * (1.0 / l)

    def attend(nk):
        for h in range(DIFF_HEADS):
            qh = q[:, h * LANE:(h + 1) * LANE]
            q1 = jnp.where(first, qh, 0.0).astype(BF16)
            q2 = jnp.where(first, 0.0, qh).astype(BF16)
            kh = k_s[0:nk, h * LANE:(h + 1) * LANE]
            vh = v_ref[0:nk, h * LANE:(h + 1) * LANE]
            o = softmax_pv(q1, kh, vh) - lam * softmax_pv(q2, kh, vh)
            o = o * lax.rsqrt(jnp.mean(o * o, axis=-1, keepdims=True) + LN_EPS) * gsub_ref[...]
            o_ref[:, h * LANE:(h + 1) * LANE] = (o * (1.0 - lambda_init)).astype(BF16)

    @pl.when(qi * tq < n_ctx)
    def _():
        attend(n_ctx)

    @pl.when(qi * tq >= n_ctx)
    def _():
        attend(t_all)


def diff_attn(p_all, tabs, lam_pack, g_sub, lambda_init, n_ctx, tq):
    b, t, _ = p_all.shape
    cosd, sind = tabs
    kv_spec = lambda g: pl.BlockSpec((None, t, GROUP), lambda bi, qi: (bi, 0, g))
    q_spec = lambda g: pl.BlockSpec((None, tq, GROUP), lambda bi, qi: (bi, qi, g))
    full = lambda shape: pl.BlockSpec(shape, lambda bi, qi: (0,) * len(shape))
    return pl.pallas_call(
        functools.partial(_diff_body, n_ctx=n_ctx, tq=tq, scale=DIFF_HD ** -0.5, lambda_init=lambda_init),
        grid=(b, t // tq),
        in_specs=[kv_spec(G_DIFF_K), kv_spec(G_DIFF_KP), kv_spec(G_DIFF_V), q_spec(G_DIFF_Q), q_spec(G_DIFF_QP),
                  full((t, LANE)), full((t, LANE)),
                  pl.BlockSpec((tq, LANE), lambda bi, qi: (qi, 0)),
                  pl.BlockSpec((tq, LANE), lambda bi, qi: (qi, 0)),
                  full(lam_pack.shape), full(g_sub.shape)],
        out_specs=pl.BlockSpec((None, tq, GROUP), lambda bi, qi: (bi, qi, 0)),
        out_shape=jax.ShapeDtypeStruct((b, t, GROUP), BF16),
        scratch_shapes=[pltpu.VMEM((t, GROUP), BF16)],
        compiler_params=_cparams("parallel", "arbitrary"),
        name="diff_attn",
    )(p_all, p_all, p_all, p_all, p_all, cosd, sind, cosd, sind, lam_pack, g_sub)


def _chunk_of_step(s, ncc, nlc, reverse):
    if not reverse:
        return s
    return jnp.where(s < ncc, ncc - 1 - s, 2 * ncc + nlc - 1 - s)


def _lru_body(*refs, ncc, nlc, tc, reverse):
    if reverse:
        (xp_ref, x_ref, xn_ref, cw_ref, cb_ref, wax_ref, ba_ref, bx_ref, lam_ref, hf_ref, gate_ref,
         o_ref, xe_s, a_s, in_s, h_s) = refs
    else:
        (xp_ref, x_ref, xn_ref, cw_ref, cb_ref, wax_ref, ba_ref, bx_ref, lam_ref,
         o_ref, xe_s, a_s, in_s, h_s) = refs
    s = pl.program_id(0)
    c = _chunk_of_step(s, ncc, nlc, reverse)
    nb = x_ref.shape[1]
    bw = LRU_WIDTH // LRU_BLOCKS

    @pl.when(s == 0)
    def _():
        h_s[...] = jnp.zeros_like(h_s)

    has_prev = jnp.logical_and(c != 0, c != ncc).astype(F32)
    has_next = jnp.logical_and(c != ncc - 1, c != ncc + nlc - 1).astype(F32)
    xe_s[0:2] = xp_ref[...].astype(F32) * has_prev
    xe_s[2:tc + 2] = x_ref[...].astype(F32)
    xe_s[tc + 2:tc + 3] = xn_ref[...].astype(F32) * has_next

    for blk in range(LRU_BLOCKS):
        ls = slice(blk * bw, (blk + 1) * bw)
        xc = cb_ref[:, ls].reshape(1, 1, bw)
        for k in range(LRU_CONV):
            xc = xc + cw_ref[k:k + 1, ls].reshape(1, 1, bw) * xe_s[k:k + tc, :, ls]
        xc2 = xc.reshape(tc * nb, bw)
        gates = _dot(xc2.astype(BF16), wax_ref[blk])
        r = _sigmoid(gates[:, :bw] + ba_ref[:, ls])
        i = _sigmoid(gates[:, bw:] + bx_ref[:, ls])
        log_a = -LRU_C * r * _softplus(-lam_ref[:, ls])
        a = jnp.exp(log_a)
        a_s[:, :, ls] = a.reshape(tc, nb, bw)
        in_s[:, :, ls] = (jnp.sqrt(1.0 - a * a) * (i * xc2)).reshape(tc, nb, bw)

    def step(j, h):
        tt = tc - 1 - j if reverse else j
        h = a_s[tt] * h + in_s[tt]
        in_s[tt] = h
        return h

    h_s[...] = lax.fori_loop(0, tc, step, h_s[...], unroll=8)

    if reverse:
        o_ref[...] = ((hf_ref[...] + in_s[...]) * _gelu_tanh(gate_ref[...].astype(F32))).astype(BF16)
    else:
        o_ref[...] = in_s[...]


def lru_direction(x_t, gate_t, h_fwd, wts, n_ctx, tc, reverse):
    t, nb, w = x_t.shape
    ncc, nlc = n_ctx // tc, (t - n_ctx) // tc
    cw, cb, wax, ba, bx, lam = wts
    cidx = lambda s: _chunk_of_step(s, ncc, nlc, reverse)
    full = lambda shape: pl.BlockSpec(shape, lambda s: (0,) * len(shape))
    chunk = pl.BlockSpec((tc, nb, w), lambda s: (cidx(s), 0, 0))
    in_specs = [pl.BlockSpec((2, nb, w), lambda s: (jnp.maximum(cidx(s) * (tc // 2) - 1, 0), 0, 0)),
                chunk,
                pl.BlockSpec((1, nb, w), lambda s: (jnp.minimum((cidx(s) + 1) * tc, t - 1), 0, 0)),
                full(cw.shape), full(cb.shape), full(wax.shape), full(ba.shape), full(bx.shape), full(lam.shape)]
    args = [x_t, x_t, x_t, cw, cb, wax, ba, bx, lam]
    if reverse:
        in_specs += [chunk, chunk]
        args += [h_fwd, gate_t]
    return pl.pallas_call(
        functools.partial(_lru_body, ncc=ncc, nlc=nlc, tc=tc, reverse=reverse),
        grid=(ncc + nlc,),
        in_specs=in_specs,
        out_specs=chunk,
        out_shape=jax.ShapeDtypeStruct((t, nb, w), BF16 if reverse else F32),
        scratch_shapes=[pltpu.VMEM((tc + 3, nb, w), F32), pltpu.VMEM((tc, nb, w), F32),
                        pltpu.VMEM((tc, nb, w), F32), pltpu.VMEM((nb, w), F32)],
        compiler_params=_cparams("arbitrary"),
        name="lru_bwd" if reverse else "lru_fwd",
    )(*args)


def _ssd_body(*refs, ncc, nlc, reverse):
    if reverse:
        (xp_ref, x_ref, xn_ref, dtc_ref, dtr_ref, cw_ref, cb_ref, arow_ref, acol_ref, brow_ref, bcol_ref,
         yf_ref, z_ref, dskip_ref, gn_ref, o_ref, xe_s, st_s) = refs
    else:
        (xp_ref, x_ref, xn_ref, dtc_ref, dtr_ref, cw_ref, cb_ref, arow_ref, acol_ref, brow_ref, bcol_ref,
         o_ref, xe_s, st_s) = refs
    s = pl.program_id(1)
    c = _chunk_of_step(s, ncc, nlc, reverse)
    ln = SSD_CHUNK
    npair = SSD_HEADS // 2
    lane0 = SSD_HEADS if reverse else 0

    @pl.when(s == 0)
    def _():
        st_s[...] = jnp.zeros_like(st_s)

    has_prev = jnp.logical_and(c != 0, c != ncc).astype(F32)
    has_next = jnp.logical_and(c != ncc - 1, c != ncc + nlc - 1).astype(F32)
    xe_s[0:8] = xp_ref[...].astype(F32)[8:16] * has_prev
    xe_s[8:ln + 8] = x_ref[...].astype(F32)
    xe_s[ln + 8:ln + 16] = xn_ref[...].astype(F32)[0:8] * has_next
    xbc = cb_ref[...]
    for k in range(SSD_CONV):
        xbc = xbc + cw_ref[k:k + 1, :] * xe_s[pl.ds(6 + k, ln), :]
    xbc = _silu(xbc)

    dt_c = _softplus(dtc_ref[...].astype(F32)[:, 3 * LANE:4 * LANE] + brow_ref[...])
    da_c = dt_c * arow_ref[...]
    dt_r = _softplus(dtr_ref[...] + bcol_ref[...])
    da_r = dt_r * acol_ref[...]
    li = lax.broadcasted_iota(jnp.int32, (ln, ln), 0)
    mi = lax.broadcasted_iota(jnp.int32, (ln, ln), 1)
    keep = (li <= mi) if reverse else (li >= mi)
    keep_bf = jnp.where(keep, 1.0, 0.0).astype(BF16)
    keep_t_bf = jnp.where((li >= mi) if reverse else (li <= mi), 1.0, 0.0).astype(BF16)
    cs_c = sum(_dot(keep_bf, part) for part in _split3(da_c))
    cs_r = sum(_dot(part, keep_t_bf) for part in _split3(da_r))
    tot_c = jnp.sum(da_c, axis=0, keepdims=True)
    e_in = jnp.exp(cs_c)
    e_out = jnp.exp(tot_c - cs_c)
    e_tot = jnp.exp(tot_c)

    lane_first = lax.broadcasted_iota(jnp.int32, (1, LANE), 1) < SSD_HD
    row_first = lax.broadcasted_iota(jnp.int32, (LANE, 1), 0) < SSD_HD

    def col(v, h):
        return jnp.broadcast_to(v[:, lane0 + h:lane0 + h + 1], (v.shape[0], LANE))

    for pr in range(npair):
        g = pr // (npair // SSD_GROUPS)
        h0, h1 = 2 * pr, 2 * pr + 1
        bm = xbc[:, SSD_WIDTH + g * SSD_STATE:SSD_WIDTH + (g + 1) * SSD_STATE].astype(BF16)
        cm = xbc[:, SSD_WIDTH + (SSD_GROUPS + g) * SSD_STATE:SSD_WIDTH + (SSD_GROUPS + g + 1) * SSD_STATE].astype(BF16)
        xs = xbc[:, pr * LANE:(pr + 1) * LANE]
        xdt = xs * jnp.where(lane_first, col(dt_c, h0), col(dt_c, h1))
        cb = _dot_nt(cm, bm)
        y = None
        for h, sel in ((h0, lane_first), (h1, jnp.logical_not(lane_first))):
            dec = jnp.exp(jnp.where(keep, col(cs_c, h) - cs_r[h:h + 1, :], -jnp.inf))
            part = _dot((cb * dec).astype(BF16), jnp.where(sel, xdt, 0.0).astype(BF16))
            y = part if y is None else y + part
        st = st_s[pr]
        y_off = _dot_nt(cm, st.astype(BF16)) * jnp.where(lane_first, col(e_in, h0), col(e_in, h1))
        xw = (xdt * jnp.where(lane_first, col(e_out, h0), col(e_out, h1))).astype(BF16)
        new = _dot_tn(xw, bm)
        dec_st = jnp.where(row_first, e_tot[:, lane0 + h0:lane0 + h0 + 1], e_tot[:, lane0 + h1:lane0 + h1 + 1])
        st_s[pr] = st * dec_st + new
        y = y + y_off
        if reverse:
            y = y + yf_ref[:, pr * LANE:(pr + 1) * LANE] + xs * dskip_ref[:, pr * LANE:(pr + 1) * LANE]
            o_ref[:, pr * LANE:(pr + 1) * LANE] = y
        else:
            o_ref[:, pr * LANE:(pr + 1) * LANE] = y

    if reverse:
        yz = o_ref[...] * _silu(z_ref[...].astype(F32))
        gw = SSD_WIDTH // SSD_GROUPS
        for g in range(SSD_GROUPS):
            blk = yz[:, g * gw:(g + 1) * gw]
            blk = blk * lax.rsqrt(jnp.mean(blk * blk, axis=-1, keepdims=True) + LN_EPS) * gn_ref[:, g * gw:(g + 1) * gw]
            o_ref[:, g * gw:(g + 1) * gw] = blk


def ssd_direction(p_all, dt_t, y_fwd, wts, n_ctx, reverse):
    b, t, _ = p_all.shape
    ln = SSD_CHUNK
    ncc, nlc = n_ctx // ln, (t - n_ctx) // ln
    cw, cb, arow, acol, brow, bcol, dskip, gn = wts
    cidx = lambda s: _chunk_of_step(s, ncc, nlc, reverse)
    full = lambda a: pl.BlockSpec(a.shape, lambda bi, s: (0,) * a.ndim)
    xw = 2 * GROUP
    gx = G_SSD_XBC // 2
    hb = ln // 16
    in_specs = [pl.BlockSpec((None, 16, xw), lambda bi, s: (bi, jnp.maximum(cidx(s) * hb - 1, 0), gx)),
                pl.BlockSpec((None, ln, xw), lambda bi, s: (bi, cidx(s), gx)),
                pl.BlockSpec((None, 16, xw), lambda bi, s: (bi, jnp.minimum((cidx(s) + 1) * hb, t // 16 - 1), gx)),
                pl.BlockSpec((None, ln, GROUP), lambda bi, s: (bi, cidx(s), G_MLA_CQ)),
                pl.BlockSpec((None, SSD_HEADS, ln), lambda bi, s: (bi, 1 if reverse else 0, cidx(s))),
                full(cw), full(cb), full(arow), full(acol), full(brow), full(bcol)]
    args = [p_all, p_all, p_all, p_all, dt_t, cw, cb, arow, acol, brow, bcol]
    out_spec = pl.BlockSpec((None, ln, SSD_WIDTH), lambda bi, s: (bi, cidx(s), 0))
    if reverse:
        in_specs += [out_spec, pl.BlockSpec((None, ln, GROUP), lambda bi, s: (bi, cidx(s), G_SSD_Z)), full(dskip), full(gn)]
        args += [y_fwd, p_all, dskip, gn]
    return pl.pallas_call(
        functools.partial(_ssd_body, ncc=ncc, nlc=nlc, reverse=reverse),
        grid=(b, ncc + nlc),
        in_specs=in_specs,
        out_specs=out_spec,
        out_shape=jax.ShapeDtypeStruct((b, t, SSD_WIDTH), F32),
        scratch_shapes=[pltpu.VMEM((ln + 16, xw), F32), pltpu.VMEM((SSD_HEADS // 2, LANE, SSD_STATE), F32)],
        compiler_params=_cparams("parallel", "arbitrary"),
        name="ssd_bwd" if reverse else "ssd_fwd",
    )(*args)


MERGE_TM = 256


def _merge_body(x_ref, y0_ref, y1_ref, y2_ref, y3_ref, gt0_ref, gt1_ref, gt2_ref, gt3_ref, wb_ref, wo_ref, g1_ref, sh2_ref, sc2_ref,
                lng_ref, lnb_ref, rw_ref, rb_ref, x1_ref, v_ref, lg_ref):
    d = D_MODEL
    merged = None
    for k, (y_ref, gt_ref) in enumerate(((y0_ref, gt0_ref), (y1_ref, gt1_ref), (y2_ref, gt2_ref), (y3_ref, gt3_ref))):
        gate = _sigmoid(gt_ref[...].astype(F32))
        term = gate * _dot(y_ref[...].astype(BF16), wb_ref[k])
        merged = term if merged is None else merged + term
    m = _dot(merged.astype(BF16), wo_ref[...])
    r = DEEPNORM_ALPHA * x_ref[...] + g1_ref[...] * m
    rc = r - jnp.mean(r, axis=-1, keepdims=True)
    x1 = rc * lax.rsqrt(jnp.mean(rc * rc, axis=-1, keepdims=True) + LN_EPS) * lng_ref[...] + lnb_ref[...]
    x1_ref[...] = x1
    xc = x1 - jnp.mean(x1, axis=-1, keepdims=True)
    v = xc * lax.rsqrt(jnp.mean(xc * xc, axis=-1, keepdims=True) + LN_EPS) * (1.0 + sc2_ref[...]) + sh2_ref[...]
    v_ref[...] = v
    vh = v.astype(BF16)
    vl = (v - vh.astype(F32)).astype(BF16)
    lg_ref[...] = _dot(vh, rw_ref[0]) + _dot(vl, rw_ref[0]) + _dot(vh, rw_ref[1]) + rb_ref[...]


def merge_ln(x_all, ys, p_all, mod4, wts, n_ctx, row0):
    b, t, d = x_all.shape
    tm = MERGE_TM if (t % MERGE_TM == 0 and n_ctx % MERGE_TM == 0) else 128
    t0 = row0 // tm
    nt = t // tm - t0
    wb, wo, lng, lnb, rw, rb = wts
    nctx_t = n_ctx // tm
    row = lambda w, dt: pl.BlockSpec((None, tm, w), lambda bi, ti: (bi, ti + t0, 0))
    mod_spec = lambda k: pl.BlockSpec((None, None, 1, d),
                                      lambda bi, ti: (jnp.where(ti + t0 < nctx_t, b, bi), k, 0, 0))
    const = lambda a: pl.BlockSpec(a.shape, lambda bi, ti: (0,) * a.ndim, pipeline_mode=pl.Buffered(1))
    out_row = lambda w: pl.BlockSpec((None, tm, w), lambda bi, ti: (bi, ti, 0))
    gate_spec = lambda k: pl.BlockSpec((None, tm, d), lambda bi, ti: (bi, ti + t0, G_GATES * GROUP // d + k))
    return pl.pallas_call(
        _merge_body,
        grid=(b, nt),
        in_specs=[row(d, F32), row(GROUP, BF16), row(GROUP, BF16), row(GROUP, BF16), row(GROUP, F32),
                  gate_spec(0), gate_spec(1), gate_spec(2), gate_spec(3),
                  const(wb), const(wo), mod_spec(2), mod_spec(3), mod_spec(4),
                  const(lng), const(lnb), const(rw), const(rb)],
        out_specs=[out_row(d), out_row(d), out_row(LANE)],
        out_shape=[jax.ShapeDtypeStruct((b, nt * tm, d), F32), jax.ShapeDtypeStruct((b, nt * tm, d), F32),
                   jax.ShapeDtypeStruct((b, nt * tm, LANE), F32)],
        compiler_params=_cparams("parallel", "parallel"),
        name="merge_ln",
    )(x_all, *ys, p_all, p_all, p_all, p_all, wb, wo, mod4, mod4, mod4, lng, lnb, rw, rb)


COMBINE_TT = 128


def _combine_body(slot_ref, y_hbm, x1_ref, w_ref, g2_ref, lng_ref, lnb_ref, o_ref, buf, sem):
    tt = x1_ref.shape[0]
    n_rows = TOP_K * tt

    def issue(a, carry):
        pltpu.make_async_copy(y_hbm.at[pl.ds(slot_ref[0, 0, a], 1)], buf.at[pl.ds(a, 1)], sem).start()
        return carry

    lax.fori_loop(0, n_rows, issue, 0, unroll=8)
    pltpu.make_async_copy(y_hbm.at[pl.ds(0, n_rows)], buf, sem).wait()
    f = None
    for k in range(TOP_K):
        term = w_ref[:, k:k + 1] * buf[k * tt:(k + 1) * tt, :]
        f = term if f is None else f + term
    r = DEEPNORM_ALPHA * x1_ref[...] + g2_ref[...] * f
    rc = r - jnp.mean(r, axis=-1, keepdims=True)
    o_ref[...] = rc * lax.rsqrt(jnp.mean(rc * rc, axis=-1, keepdims=True) + LN_EPS) * lng_ref[...] + lnb_ref[...]


def combine_ln(x1, y, slots, top_w, mod4, lng, lnb, n_ctx_rows):
    b, t, d = x1.shape
    tt = COMBINE_TT
    nt = t // tt
    nctx_t = n_ctx_rows // tt
    slot_tiles = slots.reshape(b * nt, tt, TOP_K).transpose(0, 2, 1).reshape(b * nt, 1, TOP_K * tt)
    row = pl.BlockSpec((None, tt, d), lambda bi, ti: (bi, ti, 0))
    const = lambda a: pl.BlockSpec(a.shape, lambda bi, ti: (0,) * a.ndim)
    return pl.pallas_call(
        _combine_body,
        grid=(b, nt),
        in_specs=[pl.BlockSpec((1, 1, TOP_K * tt), lambda bi, ti: (bi * nt + ti, 0, 0), memory_space=pltpu.SMEM),
                  pl.BlockSpec(memory_space=pl.ANY),
                  row,
                  pl.BlockSpec((None, tt, TOP_K), lambda bi, ti: (bi, ti, 0)),
                  pl.BlockSpec((None, None, 1, d), lambda bi, ti: (jnp.where(ti < nctx_t, b, bi), 5, 0, 0)),
                  const(lng), const(lnb)],
        out_specs=row,
        out_shape=jax.ShapeDtypeStruct((b, t, d), F32),
        scratch_shapes=[pltpu.VMEM((TOP_K * tt, d), F32), pltpu.SemaphoreType.DMA(())],
        compiler_params=_cparams("parallel", "parallel"),
        name="combine_ln",
    )(slot_tiles, y, x1, top_w.reshape(b, t, TOP_K), mod4, lng, lnb)


MOE_UP_TC = 1024
MOE_DOWN_TN = 1024
SEL_W = 512
W_CHUNKS = 4


def _refresh(be_ref, i):
    return jnp.logical_or(i == 0, be_ref[i] != be_ref[jnp.maximum(i - 1, 0)])


def _moe_up_body(be_ref, nvalid_ref, x_ref, *refs):
    w1_refs = refs[:W_CHUNKS]
    sel_ref, b1_ref, h_ref, wg_s, wl_s, bg_s, bl_s = refs[W_CHUNKS:]
    i = pl.program_id(1)
    half = SEL_W // 2
    rows = wg_s.shape[0] // W_CHUNKS

    @pl.when(_refresh(be_ref, i))
    def _():
        for c in range(MOE_UP_TC // SEL_W):
            for r, w1_ref in enumerate(w1_refs):
                both = _dot(w1_ref[:, c * SEL_W:(c + 1) * SEL_W].astype(BF16), sel_ref[...])
                wg_s[r * rows:(r + 1) * rows, c * half:(c + 1) * half] = both[:, :half].astype(BF16)
                wl_s[r * rows:(r + 1) * rows, c * half:(c + 1) * half] = both[:, half:].astype(BF16)
            b_hi, b_mid, b_lo = _split3(jnp.broadcast_to(b1_ref[:, c * SEL_W:(c + 1) * SEL_W], (8, SEL_W)))
            bias = _dot(b_hi, sel_ref[...]) + _dot(b_mid, sel_ref[...]) + _dot(b_lo, sel_ref[...])
            bg_s[:, c * half:(c + 1) * half] = bias[:, :half]
            bl_s[:, c * half:(c + 1) * half] = bias[:, half:]

    @pl.when(i >= nvalid_ref[0])
    def _():
        h_ref[...] = jnp.zeros_like(h_ref)

    @pl.when(i < nvalid_ref[0])
    def _():
        x = x_ref[...]
        glu = jnp.minimum(_dot(x, wg_s[...]) + bg_s[0:1, :], SWIGLU_LIMIT)
        lin = jnp.clip(_dot(x, wl_s[...]) + bl_s[0:1, :], -SWIGLU_LIMIT, SWIGLU_LIMIT)
        h_ref[...] = (glu * _sigmoid(SWIGLU_ALPHA * glu) * (lin + 1.0)).astype(BF16)


def _moe_down_body(be_ref, nvalid_ref, h_ref, *refs):
    w2_refs = refs[:W_CHUNKS]
    b2_ref, y_ref, w_s = refs[W_CHUNKS:]
    i = pl.program_id(1)
    rows = w_s.shape[0] // W_CHUNKS

    @pl.when(_refresh(be_ref, i))
    def _():
        for r, w2_ref in enumerate(w2_refs):
            w_s[r * rows:(r + 1) * rows, :] = w2_ref[...].astype(BF16)

    @pl.when(i >= nvalid_ref[0])
    def _():
        y_ref[...] = jnp.zeros_like(y_ref)

    @pl.when(i < nvalid_ref[0])
    def _():
        y_ref[...] = _dot(h_ref[...], w_s[...]) + b2_ref[...]


def moe_expert_blocks(x_sorted, block_e, n_valid, w1, b1, w2, b2, layer):
    n_slots, d = x_sorted.shape
    nb = n_slots // MOE_TM
    dh = w2.shape[2]
    half = SEL_W // 2
    src = jnp.arange(SEL_W)[:, None]
    dst = jnp.arange(SEL_W)[None, :]
    sel = (src == jnp.where(dst < half, 2 * dst, 2 * (dst - half) + 1)).astype(BF16)
    hidden = pl.pallas_call(
        _moe_up_body,
        grid_spec=pltpu.PrefetchScalarGridSpec(
            num_scalar_prefetch=2,
            grid=(2 * dh // MOE_UP_TC, nb),
            in_specs=[
                pl.BlockSpec((MOE_TM, d), lambda j, i, be, nv: (i, 0)),
                *[pl.BlockSpec((None, None, d // W_CHUNKS, MOE_UP_TC),
                               functools.partial(lambda j, i, be, nv, r: (layer, be[i], r, j), r=r))
                  for r in range(W_CHUNKS)],
                pl.BlockSpec((SEL_W, SEL_W), lambda j, i, be, nv: (0, 0)),
                pl.BlockSpec((None, None, 1, MOE_UP_TC), lambda j, i, be, nv: (layer, be[i], 0, j)),
            ],
            out_specs=pl.BlockSpec((MOE_TM, MOE_UP_TC // 2), lambda j, i, be, nv: (i, j)),
            scratch_shapes=[pltpu.VMEM((d, MOE_UP_TC // 2), BF16), pltpu.VMEM((d, MOE_UP_TC // 2), BF16),
                            pltpu.VMEM((8, MOE_UP_TC // 2), F32), pltpu.VMEM((8, MOE_UP_TC // 2), F32)],
        ),
        out_shape=jax.ShapeDtypeStruct((n_slots, dh), BF16),
        compiler_params=_cparams("arbitrary", "arbitrary"),
        name="moe_up",
    )(block_e, n_valid, x_sorted, *([w1] * W_CHUNKS), sel, b1.reshape(b1.shape[0], b1.shape[1], 1, b1.shape[2]))
    return pl.pallas_call(
        _moe_down_body,
        grid_spec=pltpu.PrefetchScalarGridSpec(
            num_scalar_prefetch=2,
            grid=(d // MOE_DOWN_TN, nb),
            in_specs=[
                pl.BlockSpec((MOE_TM, dh), lambda j, i, be, nv: (i, 0)),
                *[pl.BlockSpec((None, None, dh // W_CHUNKS, MOE_DOWN_TN),
                               functools.partial(lambda j, i, be, nv, r: (layer, be[i], r, j), r=r))
                  for r in range(W_CHUNKS)],
                pl.BlockSpec((None, None, 1, MOE_DOWN_TN), lambda j, i, be, nv: (layer, be[i], 0, j)),
            ],
            out_specs=pl.BlockSpec((MOE_TM, MOE_DOWN_TN), lambda j, i, be, nv: (i, j)),
            scratch_shapes=[pltpu.VMEM((dh, MOE_DOWN_TN), BF16)],
        ),
        out_shape=jax.ShapeDtypeStruct((n_slots, d), F32),
        compiler_params=_cparams("arbitrary", "arbitrary"),
        name="moe_down",
    )(block_e, n_valid, hidden, *([w2] * W_CHUNKS), b2.reshape(b2.shape[0], b2.shape[1], 1, b2.shape[2]))


def _dispatch_body(nvalid_ref, tok_ref, v_hbm, o_ref, buf, sem):
    i = pl.program_id(0)
    n = o_ref.shape[0]

    @pl.when(i >= nvalid_ref[0])
    def _():
        o_ref[...] = jnp.zeros_like(o_ref)

    @pl.when(i < nvalid_ref[0])
    def _():
        def issue(a, carry):
            pltpu.make_async_copy(v_hbm.at[pl.ds(tok_ref[0, 0, a], 1)], buf.at[pl.ds(a, 1)], sem).start()
            return carry

        lax.fori_loop(0, n, issue, 0, unroll=8)
        pltpu.make_async_copy(v_hbm.at[pl.ds(0, n)], buf, sem).wait()
        o_ref[...] = buf[...].astype(BF16)


def dispatch_rows(v, slot_tok, n_valid):
    t, d = v.shape
    n_slots = slot_tok.shape[0]
    nb = n_slots // MOE_TM
    return pl.pallas_call(
        _dispatch_body,
        grid_spec=pltpu.PrefetchScalarGridSpec(
            num_scalar_prefetch=1,
            grid=(nb,),
            in_specs=[pl.BlockSpec((1, 1, MOE_TM), lambda i, nv: (i, 0, 0), memory_space=pltpu.SMEM),
                      pl.BlockSpec(memory_space=pl.ANY)],
            out_specs=pl.BlockSpec((MOE_TM, d), lambda i, nv: (i, 0)),
            scratch_shapes=[pltpu.VMEM((MOE_TM, d), F32), pltpu.SemaphoreType.DMA(())],
        ),
        out_shape=jax.ShapeDtypeStruct((n_slots, d), BF16),
        compiler_params=_cparams("arbitrary"),
        name="dispatch_rows",
    )(n_valid, slot_tok.reshape(nb, 1, MOE_TM), v)


def moe_ffn(h, logits, w1, b1, w2, b2, layer):
    t, d = h.shape
    top_logit, top_idx = lax.top_k(logits, TOP_K)
    top_w = jax.nn.softmax(top_logit, axis=-1)
    n_assign = t * TOP_K
    flat_e = top_idx.reshape(-1)
    onehot = (flat_e[:, None] == jnp.arange(N_EXPERTS, dtype=flat_e.dtype)[None, :]).astype(jnp.int32)
    csum = jnp.cumsum(onehot, axis=0)
    counts = csum[-1]
    padded = (counts + MOE_TM - 1) // MOE_TM * MOE_TM
    pend = jnp.cumsum(padded)
    pstart = pend - padded
    dest = (jnp.sum(onehot * (csum + pstart[None, :]), axis=1) - 1).astype(jnp.int32)
    n_blocks = (n_assign + N_EXPERTS * (MOE_TM - 1) + MOE_TM - 1) // MOE_TM
    n_slots = n_blocks * MOE_TM
    slot_tok = jnp.zeros((n_slots,), jnp.int32).at[dest].set(jnp.arange(n_assign, dtype=jnp.int32) // TOP_K)
    n_valid = (pend[-1] // MOE_TM).astype(jnp.int32)
    blk_row = jnp.minimum(jnp.arange(n_blocks), n_valid - 1) * MOE_TM
    block_e = jnp.minimum(jnp.sum((pend[None, :] <= blk_row[:, None]).astype(jnp.int32), axis=1), N_EXPERTS - 1)
    x_sorted = dispatch_rows(h, slot_tok, n_valid.reshape(1))
    y = moe_expert_blocks(x_sorted, block_e, n_valid.reshape(1), w1, b1, w2, b2, layer)
    return y, dest.reshape(t, TOP_K), top_w


def _rot_partner(w, width=64):
    d, n = w.shape
    g = w.reshape(d, n // width, 2, width // 2)
    return jnp.stack([-g[:, :, 1], g[:, :, 0]], axis=2).reshape(d, n)


def _axial_angles(n_tok, rot_dim):
    rows = n_tok // GRID_W
    row = jnp.repeat(jnp.arange(rows), GRID_W).astype(F32)
    col = jnp.tile(jnp.arange(GRID_W), rows).astype(F32)
    n_freq = rot_dim // 4
    inv = jnp.power(ROPE_BASE, -jnp.arange(n_freq, dtype=F32) / n_freq)
    return jnp.concatenate([row[:, None] * inv, col[:, None] * inv], axis=-1)


def _rope_tables(n_ctx, n_lat, rot_dim, n_groups):
    ang = _axial_angles(n_lat, rot_dim)
    cos = jnp.concatenate([jnp.ones((n_ctx, rot_dim // 2), F32), jnp.cos(ang)], axis=0)
    sin = jnp.concatenate([jnp.zeros((n_ctx, rot_dim // 2), F32), jnp.sin(ang)], axis=0)
    t = n_ctx + n_lat
    pad = LANE - n_groups * rot_dim
    cos = jnp.concatenate([cos] * (2 * n_groups) + [jnp.ones((t, pad), F32)], axis=1)
    sin = jnp.concatenate([sin] * (2 * n_groups) + [jnp.zeros((t, pad), F32)], axis=1)
    return cos, sin


def _layer_weights(i, a):
    d = D_MODEL
    w_in = a['w_in'][i]
    z = lambda n: jnp.zeros((d, n), F32)
    seg = lambda c0, n: w_in[:, c0:c0 + n]
    kr = seg(C_MLA_KR, MLA_ROPE)
    groups = [
        seg(C_MLA_CKV, MLA_KV_LORA), kr, z(64), _rot_partner(kr), z(64),
        seg(C_DIFF_K, 512), _rot_partner(seg(C_DIFF_K, 512)), seg(C_DIFF_V, 512),
        seg(C_SSD_XBC, SSD_XBC_W), seg(C_LRU_X, LRU_WIDTH),
        seg(C_MLA_CQ, MLA_Q_LORA), seg(C_SSD_DT, 2 * SSD_HEADS), z(LANE - 2 * SSD_HEADS),
        seg(C_DIFF_Q, 512), _rot_partner(seg(C_DIFF_Q, 512)),
        seg(C_LRU_GATE, LRU_WIDTH), seg(C_SSD_Z, SSD_WIDTH), seg(C_GATES, N_BRANCH * d),
    ]
    w_in_r = jnp.concatenate(groups, axis=1).astype(BF16)

    w_uq = a['mla_w_uq'][i].reshape(MLA_Q_LORA, MLA_HEADS, MLA_NOPE + MLA_ROPE)
    w_ukv = a['mla_w_ukv'][i].reshape(MLA_KV_LORA, MLA_HEADS, MLA_NOPE + MLA_V)
    zq = jnp.zeros((MLA_Q_LORA, 64), F32)
    wq = jnp.stack([jnp.concatenate([w_uq[:, h, :], zq], axis=1) for h in range(MLA_HEADS)]).astype(BF16)
    wqp = jnp.stack([jnp.concatenate([_rot_partner(w_uq[:, h, MLA_NOPE:]), zq], axis=1)
                     for h in range(MLA_HEADS)]).astype(BF16)
    wuk = jnp.transpose(w_ukv[:, :, :MLA_NOPE], (1, 0, 2)).astype(BF16)
    wuv = jnp.transpose(w_ukv[:, :, MLA_NOPE:], (1, 0, 2)).astype(BF16)
    mla = (a['mla_g_kv'][i].reshape(1, -1), a['mla_g_q'][i].reshape(1, -1), wuk, wuv, wq, wqp)

    lam_pack = jnp.zeros((8, LANE), F32)
    for r, name in enumerate(('diff_lq1', 'diff_lk1', 'diff_lq2', 'diff_lk2')):
        lam_pack = lam_pack.at[r, :DIFF_HD].set(a[name][i])
    g_sub = a['diff_g_sub'][i].reshape(1, -1)

    lru = []
    for dd in range(2):
        wax = jnp.concatenate([a['lru_w_a'][i, dd], a['lru_w_x'][i, dd]], axis=-1).astype(BF16)
        lru.append((a['lru_conv_w'][i], a['lru_conv_b'][i].reshape(1, -1), wax,
                    a['lru_b_a'][i, dd].reshape(1, -1), a['lru_b_x'][i, dd].reshape(1, -1),
                    a['lru_lam'][i, dd].reshape(1, -1)))

    ssd = []
    pad_row = lambda v: jnp.zeros((1, LANE), F32).at[0, :SSD_HEADS].set(v)
    for dd in range(2):
        a_neg = -jnp.exp(a['ssd_a_log'][i, dd])
        bias = a['ssd_dt_bias'][i, dd]
        arow = jnp.zeros((1, LANE), F32).at[0, dd * SSD_HEADS:(dd + 1) * SSD_HEADS].set(a_neg)
        brow = jnp.zeros((1, LANE), F32).at[0, dd * SSD_HEADS:(dd + 1) * SSD_HEADS].set(bias)
        ssd.append((a['ssd_conv_w'][i], a['ssd_conv_b'][i].reshape(1, -1), arow, a_neg.reshape(-1, 1),
                    brow, bias.reshape(-1, 1),
                    jnp.repeat(a['ssd_d'][i], SSD_HD).reshape(1, -1), a['ssd_g_norm'][i].reshape(1, -1), dd))

    rw = jnp.pad(a['router_w'][i], ((0, 0), (0, LANE - N_EXPERTS)))
    rwh = rw.astype(BF16)
    rwl = (rw - rwh.astype(F32)).astype(BF16)
    rb = jnp.pad(a['router_b'][i], (0, LANE - N_EXPERTS)).reshape(1, -1)
    merge = (a['w_branch'][i].astype(BF16), a['w_out'][i].astype(BF16),
             a['ln1_g'][i].reshape(1, -1), a['ln1_b'][i].reshape(1, -1), jnp.stack([rwh, rwl]), rb)
    return dict(w_in_r=w_in_r, mla=mla, lam_pack=lam_pack, g_sub=g_sub, lru=lru, ssd=ssd, merge=merge)


def _token_mixers(p_all, lw, tabs_mla, tabs_diff, layer_idx, n_ctx):
    b, t, _ = p_all.shape
    tq = 256 if (t % 256 == 0 and n_ctx % 256 == 0) else 128
    y_mla = mla_attn(p_all, tabs_mla, lw['mla'], n_ctx, tq)
    lambda_init = 0.8 - 0.6 * math.exp(-0.3 * layer_idx)
    y_diff = diff_attn(p_all, tabs_diff, lw['lam_pack'], lw['g_sub'], lambda_init, n_ctx, tq)

    x_t = jnp.transpose(p_all[:, :, G_LRU_X * GROUP:(G_LRU_X + 1) * GROUP], (1, 0, 2))
    g_t = jnp.transpose(p_all[:, :, G_LRU_GATE * GROUP:(G_LRU_GATE + 1) * GROUP], (1, 0, 2))
    tc = 256 if (t % 256 == 0 and n_ctx % 256 == 0) else 128
    h_f = lru_direction(x_t, None, None, lw['lru'][0], n_ctx, tc, False)
    y_lru = jnp.transpose(lru_direction(x_t, g_t, h_f, lw['lru'][1], n_ctx, tc, True), (1, 0, 2))

    dt0 = G_MLA_CQ * GROUP + MLA_Q_LORA
    dt_t = jnp.transpose(p_all[:, :, dt0:dt0 + 2 * SSD_HEADS].astype(F32), (0, 2, 1))
    ssd_f, ssd_b = lw['ssd']
    y_f = ssd_direction(p_all, dt_t, None, ssd_f[:8], n_ctx, False)
    y_ssd = ssd_direction(p_all, dt_t, y_f, ssd_b[:8], n_ctx, True)
    return y_mla, y_diff, y_lru, y_ssd


def kernel(x, c, ctx, c_ctx, w_mod, b_mod, w_in, mla_g_q, mla_g_kv, mla_w_uq, mla_w_ukv, diff_lq1, diff_lk1, diff_lq2, diff_lk2, diff_g_sub, lru_conv_w, lru_conv_b, lru_w_a, lru_b_a, lru_w_x, lru_b_x, lru_lam, ssd_conv_w, ssd_conv_b, ssd_a_log, ssd_dt_bias, ssd_d, ssd_g_norm, w_branch, w_out, ln1_g, ln1_b, ln2_g, ln2_b, router_w, router_b, w1, b1, w2, b2):
    a = dict(w_in=w_in, mla_g_q=mla_g_q, mla_g_kv=mla_g_kv, mla_w_uq=mla_w_uq, mla_w_ukv=mla_w_ukv,
             diff_lq1=diff_lq1, diff_lk1=diff_lk1, diff_lq2=diff_lq2, diff_lk2=diff_lk2, diff_g_sub=diff_g_sub,
             lru_conv_w=lru_conv_w, lru_conv_b=lru_conv_b, lru_w_a=lru_w_a, lru_b_a=lru_b_a, lru_w_x=lru_w_x,
             lru_b_x=lru_b_x, lru_lam=lru_lam, ssd_conv_w=ssd_conv_w, ssd_conv_b=ssd_conv_b, ssd_a_log=ssd_a_log,
             ssd_dt_bias=ssd_dt_bias, ssd_d=ssd_d, ssd_g_norm=ssd_g_norm, w_branch=w_branch, w_out=w_out,
             ln1_g=ln1_g, ln1_b=ln1_b, router_w=router_w, router_b=router_b)
    bsz, n_lat, d = x.shape
    n_ctx = ctx.shape[1]
    t = n_ctx + n_lat
    depth = w_in.shape[0]
    tabs_mla = _rope_tables(n_ctx, n_lat, MLA_ROPE, 1)
    tabs_diff = _rope_tables(n_ctx, n_lat, DIFF_HD, 2)
    x_all = jnp.concatenate([ctx, x], axis=1)
    cond = jnp.zeros((16, d), F32).at[:bsz].set(jax.nn.silu(c)).at[bsz].set(jax.nn.silu(c_ctx))

    for i in range(depth):
        last = i == depth - 1
        lw = _layer_weights(i, a)
        mod4 = matmul_bias(cond, w_mod[i], b_mod[i]).reshape(16, 6, 1, d)
        p_all = ln_mod_proj(x_all, mod4, lw['w_in_r'], n_ctx)
        ys = _token_mixers(p_all, lw, tabs_mla, tabs_diff, i, n_ctx)
        row0 = n_ctx if last else 0
        x1, v, logits = merge_ln(x_all, ys, p_all, mod4, lw['merge'], n_ctx, row0)
        tr = t - row0
        y, slots, top_w = moe_ffn(v.reshape(bsz * tr, d), logits.reshape(bsz * tr, LANE)[:, :N_EXPERTS],
                                  w1, b1, w2, b2, i)
        x_all = combine_ln(x1, y, slots, top_w, mod4, ln2_g[i].reshape(1, -1), ln2_b[i].reshape(1, -1),
                           n_ctx - row0)
    return x_all
```

```python
import functools
import math

import numpy as np
import jax
import jax.numpy as jnp
from jax import lax
from jax.experimental import pallas as pl
from jax.experimental.pallas import tpu as pltpu

D_MODEL = 2048
DEPTH = 2
GRID_W = 64
ROPE_BASE = 10000.0

MLA_HEADS = 4
MLA_Q_LORA = 384
MLA_KV_LORA = 256
MLA_NOPE = 128
MLA_ROPE = 64
MLA_V = 128
DIFF_HEADS = 4
DIFF_HD = 64
LRU_WIDTH = 512
LRU_BLOCKS = 4
LRU_CONV = 4
LRU_C = 8.0
SSD_HEADS = 8
SSD_HD = 64
SSD_WIDTH = SSD_HEADS * SSD_HD
SSD_GROUPS = 2
SSD_STATE = 128
SSD_CONV = 4
SSD_CHUNK = 128
SSD_XBC_W = SSD_WIDTH + 2 * SSD_GROUPS * SSD_STATE
N_BRANCH = 4
BRANCH_WIDTH = 512
N_EXPERTS = 32
TOP_K = 4
D_EXPERT = D_MODEL
SWIGLU_LIMIT = 7.0
SWIGLU_ALPHA = 1.702
DEEPNORM_ALPHA = (2 * DEPTH) ** 0.25
LN_EPS = 1e-6

C_MLA_CKV = 0
C_MLA_KR = C_MLA_CKV + MLA_KV_LORA
C_DIFF_K = C_MLA_KR + MLA_ROPE
C_DIFF_V = C_DIFF_K + 2 * DIFF_HEADS * DIFF_HD
C_LRU_X = C_DIFF_V + DIFF_HEADS * 2 * DIFF_HD
C_SSD_XBC = C_LRU_X + LRU_WIDTH
C_SSD_DT = C_SSD_XBC + SSD_XBC_W
N_STATE_COLS = C_SSD_DT + 2 * SSD_HEADS
C_MLA_CQ = N_STATE_COLS
C_DIFF_Q = C_MLA_CQ + MLA_Q_LORA
C_LRU_GATE = C_DIFF_Q + 2 * DIFF_HEADS * DIFF_HD
C_SSD_Z = C_LRU_GATE + LRU_WIDTH
C_GATES = C_SSD_Z + SSD_WIDTH

GROUP = 512
G_MLA_KV = 0
G_DIFF_K = 1
G_DIFF_KP = 2
G_DIFF_V = 3
G_SSD_XBC = 4
G_LRU_X = 6
G_MLA_CQ = 7
G_DIFF_Q = 8
G_DIFF_QP = 9
G_LRU_GATE = 10
G_SSD_Z = 11
G_GATES = 12
N_GROUPS = G_GATES + N_BRANCH * D_MODEL // GROUP
NP_COLS = N_GROUPS * GROUP

LANE = 128
V7X_VMEM_LIMIT_BYTES = 48 * 1024 * 1024

MOE_TM = 512

BF16 = jnp.bfloat16
F32 = jnp.float32


def _cparams(*sem):
    return pltpu.CompilerParams(dimension_semantics=sem, vmem_limit_bytes=V7X_VMEM_LIMIT_BYTES)


def _dot(a, b):
    return jnp.dot(a, b, preferred_element_type=F32)


def _dot_nt(a, b):
    return lax.dot_general(a, b, (((1,), (1,)), ((), ())), preferred_element_type=F32)


def _dot_tn(a, b):
    return lax.dot_general(a, b, (((0,), (0,)), ((), ())), preferred_element_type=F32)


def _sigmoid(x):
    return 1.0 / (1.0 + jnp.exp(-x))


def _silu(x):
    return x * _sigmoid(x)


def _softplus(x):
    return jnp.maximum(x, 0.0) + jnp.log1p(jnp.exp(-jnp.abs(x)))


def _gelu_tanh(x):
    return 0.5 * x * (1.0 + jnp.tanh(math.sqrt(2.0 / math.pi) * (x + 0.044715 * (x * x * x))))


def _split3(x):
    hi = x.astype(BF16)
    r1 = x - hi.astype(F32)
    mid = r1.astype(BF16)
    lo = (r1 - mid.astype(F32)).astype(BF16)
    return hi, mid, lo


def _mm_body(a_ref, b_ref, bias_ref, o_ref):
    o_ref[...] = (_dot(a_ref[...], b_ref[...]) + bias_ref[...]).astype(o_ref.dtype)


def matmul_bias(a, b, bias, tn=512):
    m, k = a.shape
    _, n = b.shape
    return pl.pallas_call(
        _mm_body,
        grid=(n // tn,),
        in_specs=[pl.BlockSpec((m, k), lambda j: (0, 0)),
                  pl.BlockSpec((k, tn), lambda j: (0, j)),
                  pl.BlockSpec((1, tn), lambda j: (0, j))],
        out_specs=pl.BlockSpec((m, tn), lambda j: (0, j)),
        out_shape=jax.ShapeDtypeStruct((m, n), F32),
        compiler_params=_cparams("parallel"),
        name="mod_matmul",
    )(a.astype(BF16), b.astype(BF16), bias.reshape(1, n))


PROJ_TM = 768
PROJ_TN = 1024


def _proj_body(x_ref, shc_ref, scc_ref, shl_ref, scl_ref, w_ref, o_ref, u_ref, *, n_ctx, tm):
    t = pl.program_id(1)

    @pl.when(pl.program_id(2) == 0)
    def _():
        x = x_ref[...]
        xc = x - jnp.mean(x, axis=-1, keepdims=True)
        y = xc * lax.rsqrt(jnp.mean(xc * xc, axis=-1, keepdims=True) + LN_EPS)
        is_ctx = lax.broadcasted_iota(jnp.int32, (tm, 1), 0) + t * tm < n_ctx
        sh = jnp.where(is_ctx, shc_ref[...], shl_ref[...])
        sc = jnp.where(is_ctx, scc_ref[...], scl_ref[...])
        u_ref[...] = (y * (1.0 + sc) + sh).astype(BF16)

    o_ref[...] = _dot(u_ref[...], w_ref[...]).astype(BF16)


def ln_mod_proj(x_all, mod4, w_in_r, n_ctx):
    b, t, d = x_all.shape
    tm = PROJ_TM if t % PROJ_TM == 0 else 128
    npc = w_in_r.shape[1]
    mod_spec = lambda row_fn, k: pl.BlockSpec((None, None, 1, d), lambda bi, ti, j: (row_fn(bi), k, 0, 0))
    return pl.pallas_call(
        functools.partial(_proj_body, n_ctx=n_ctx, tm=tm),
        grid=(b, t // tm, npc // PROJ_TN),
        in_specs=[pl.BlockSpec((None, tm, d), lambda bi, ti, j: (bi, ti, 0)),
                  mod_spec(lambda bi: b, 0), mod_spec(lambda bi: b, 1),
                  mod_spec(lambda bi: bi, 0), mod_spec(lambda bi: bi, 1),
                  pl.BlockSpec((d, PROJ_TN), lambda bi, ti, j: (0, j))],
        out_specs=pl.BlockSpec((None, tm, PROJ_TN), lambda bi, ti, j: (bi, ti, j)),
        out_shape=jax.ShapeDtypeStruct((b, t, npc), BF16),
        scratch_shapes=[pltpu.VMEM((tm, d), BF16)],
        compiler_params=_cparams("parallel", "parallel", "arbitrary"),
        name="ln_mod_proj",
    )(x_all, mod4, mod4, mod4, mod4, w_in_r)


def _mla_body(kvsrc_ref, cq_ref, cosk_ref, sink_ref, cosq_ref, sinq_ref, gkv_ref, gq_ref,
              wuk_ref, wuv_ref, wq_ref, wqp_ref, o_ref, k_s, v_s, *, n_ctx, tq, scale):
    qi = pl.program_id(1)
    t_all = k_s.shape[1]

    @pl.when(qi == 0)
    def _():
        src = kvsrc_ref[...].astype(F32)
        ckv = src[:, :MLA_KV_LORA]
        ckv = ckv * lax.rsqrt(jnp.mean(ckv * ckv, axis=-1, keepdims=True) + LN_EPS) * gkv_ref[...]
        ckv = ckv.astype(BF16)
        k_rope = (src[:, 256:384] * cosk_ref[...] + src[:, 384:512] * sink_ref[...]).astype(BF16)
        for h in range(MLA_HEADS):
            k_s[h, :, 0:LANE] = _dot(ckv, wuk_ref[h]).astype(BF16)
            k_s[h, :, LANE:2 * LANE] = k_rope
            v_s[h] = _dot(ckv, wuv_ref[h]).astype(BF16)

    cq = cq_ref[...].astype(F32)[:, :MLA_Q_LORA]
    cq = cq * lax.rsqrt(jnp.mean(cq * cq, axis=-1, keepdims=True) + LN_EPS) * gq_ref[...]
    cq = cq.astype(BF16)
    cosq = cosq_ref[...]
    sinq = sinq_ref[...]

    def attend(nk):
        for h in range(MLA_HEADS):
            qh = _dot(cq, wq_ref[h])
            qp = _dot(cq, wqp_ref[h])
            q_rope = qh[:, LANE:] * cosq + qp * sinq
            q = (jnp.concatenate([qh[:, :LANE], q_rope], axis=-1) * scale).astype(BF16)
            s = _dot_nt(q, k_s[h, 0:nk, :])
            p = jnp.exp(s - jnp.max(s, axis=-1, keepdims=True))
            l = jnp.sum(p, axis=-1, keepdims=True)
            o = _dot(p.astype(BF16), v_s[h, 0:nk, :])
            o_ref[:, h * MLA_V:(h + 1) * MLA_V] = (o * (1.0 / l)).astype(BF16)

    @pl.when(qi * tq < n_ctx)
    def _():
        attend(n_ctx)

    @pl.when(qi * tq >= n_ctx)
    def _():
        attend(t_all)


def mla_attn(p_all, tabs, wts, n_ctx, tq):
    b, t, _ = p_all.shape
    cosk, sink = tabs
    gkv, gq, wuk, wuv, wq, wqp = wts
    full = lambda shape: pl.BlockSpec(shape, lambda bi, qi: (0,) * len(shape))
    return pl.pallas_call(
        functools.partial(_mla_body, n_ctx=n_ctx, tq=tq, scale=(MLA_NOPE + MLA_ROPE) ** -0.5),
        grid=(b, t // tq),
        in_specs=[pl.BlockSpec((None, t, GROUP), lambda bi, qi: (bi, 0, G_MLA_KV)),
                  pl.BlockSpec((None, tq, GROUP), lambda bi, qi: (bi, qi, G_MLA_CQ)),
                  full((t, LANE)), full((t, LANE)),
                  pl.BlockSpec((tq, LANE), lambda bi, qi: (qi, 0)),
                  pl.BlockSpec((tq, LANE), lambda bi, qi: (qi, 0)),
                  full(gkv.shape), full(gq.shape), full(wuk.shape), full(wuv.shape), full(wq.shape), full(wqp.shape)],
        out_specs=pl.BlockSpec((None, tq, MLA_HEADS * MLA_V), lambda bi, qi: (bi, qi, 0)),
        out_shape=jax.ShapeDtypeStruct((b, t, MLA_HEADS * MLA_V), BF16),
        scratch_shapes=[pltpu.VMEM((MLA_HEADS, t, 2 * LANE), BF16), pltpu.VMEM((MLA_HEADS, t, MLA_V), BF16)],
        compiler_params=_cparams("parallel", "arbitrary"),
        name="mla_attn",
    )(p_all, p_all, cosk, sink, cosk, sink, gkv, gq, wuk, wuv, wq, wqp)


def _diff_body(k_ref, kp_ref, v_ref, q_ref, qp_ref, cosk_ref, sink_ref, cosq_ref, sinq_ref, lam_ref, gsub_ref,
               o_ref, k_s, *, n_ctx, tq, scale, lambda_init):
    qi = pl.program_id(1)
    t_all = k_s.shape[0]
    nrep = GROUP // LANE

    @pl.when(qi == 0)
    def _():
        cosk = jnp.concatenate([cosk_ref[...]] * nrep, axis=-1)
        sink = jnp.concatenate([sink_ref[...]] * nrep, axis=-1)
        k_s[...] = (k_ref[...].astype(F32) * cosk + kp_ref[...].astype(F32) * sink).astype(BF16)

    cosq = jnp.concatenate([cosq_ref[...]] * nrep, axis=-1)
    sinq = jnp.concatenate([sinq_ref[...]] * nrep, axis=-1)
    q = (q_ref[...].astype(F32) * cosq + qp_ref[...].astype(F32) * sinq) * scale
    lp = lam_ref[...]
    lam = (jnp.exp(jnp.sum(lp[0:1] * lp[1:2], axis=-1, keepdims=True))
           - jnp.exp(jnp.sum(lp[2:3] * lp[3:4], axis=-1, keepdims=True)) + lambda_init)
    first = lax.broadcasted_iota(jnp.int32, (1, LANE), 1) < DIFF_HD

    def softmax_pv(qm, kh, vh):
        s = _dot_nt(qm, kh)
        p = jnp.exp(s - jnp.max(s, axis=-1, keepdims=True))
        l = jnp.sum(p, axis=-1, keepdims=True)
        return _dot(p.astype(BF16), vh) * (1.0 / l)

    def attend(nk):
        for h in range(DIFF_HEADS):
            qh = q[:, h * LANE:(h + 1) * LANE]
            q1 = jnp.where(first, qh, 0.0).astype(BF16)
            q2 = jnp.where(first, 0.0, qh).astype(BF16)
            kh = k_s[0:nk, h * LANE:(h + 1) * LANE]
            vh = v_ref[0:nk, h * LANE:(h + 1) * LANE]
            o = softmax_pv(q1, kh, vh) - lam * softmax_pv(q2, kh, vh)
            o = o * lax.rsqrt(jnp.mean(o * o, axis=-1, keepdims=True) + LN_EPS) * gsub_ref[...]
            o_ref[:, h * LANE:(h + 1) * LANE] = (o * (1.0 - lambda_init)).astype(BF16)

    @pl.when(qi * tq < n_ctx)
    def _():
        attend(n_ctx)

    @pl.when(qi * tq >= n_ctx)
    def _():
        attend(t_all)


def diff_attn(p_all, tabs, lam_pack, g_sub, lambda_init, n_ctx, tq):
    b, t, _ = p_all.shape
    cosd, sind = tabs
    kv_spec = lambda g: pl.BlockSpec((None, t, GROUP), lambda bi, qi: (bi, 0, g))
    q_spec = lambda g: pl.BlockSpec((None, tq, GROUP), lambda bi, qi: (bi, qi, g))
    full = lambda shape: pl.BlockSpec(shape, lambda bi, qi: (0,) * len(shape))
    return pl.pallas_call(
        functools.partial(_diff_body, n_ctx=n_ctx, tq=tq, scale=DIFF_HD ** -0.5, lambda_init=lambda_init),
        grid=(b, t // tq),
        in_specs=[kv_spec(G_DIFF_K), kv_spec(G_DIFF_KP), kv_spec(G_DIFF_V), q_spec(G_DIFF_Q), q_spec(G_DIFF_QP),
                  full((t, LANE)), full((t, LANE)),
                  pl.BlockSpec((tq, LANE), lambda bi, qi: (qi, 0)),
                  pl.BlockSpec((tq, LANE), lambda bi, qi: (qi, 0)),
                  full(lam_pack.shape), full(g_sub.shape)],
        out_specs=pl.BlockSpec((None, tq, GROUP), lambda bi, qi: (bi, qi, 0)),
        out_shape=jax.ShapeDtypeStruct((b, t, GROUP), BF16),
        scratch_shapes=[pltpu.VMEM((t, GROUP), BF16)],
        compiler_params=_cparams("parallel", "arbitrary"),
        name="diff_attn",
    )(p_all, p_all, p_all, p_all, p_all, cosd, sind, cosd, sind, lam_pack, g_sub)


def _chunk_of_step(s, ncc, nlc, reverse):
    if not reverse:
        return s
    return jnp.where(s < ncc, ncc - 1 - s, 2 * ncc + nlc - 1 - s)


def _lru_body(*refs, ncc, nlc, tc, reverse):
    if reverse:
        (xp_ref, x_ref, xn_ref, cw_ref, cb_ref, wax_ref, ba_ref, bx_ref, lam_ref, hf_ref, gate_ref,
         o_ref, xe_s, a_s, in_s, h_s) = refs
    else:
        (xp_ref, x_ref, xn_ref, cw_ref, cb_ref, wax_ref, ba_ref, bx_ref, lam_ref,
         o_ref, xe_s, a_s, in_s, h_s) = refs
    s = pl.program_id(0)
    c = _chunk_of_step(s, ncc, nlc, reverse)
    nb = x_ref.shape[1]
    bw = LRU_WIDTH // LRU_BLOCKS

    @pl.when(s == 0)
    def _():
        h_s[...] = jnp.zeros_like(h_s)

    has_prev = jnp.logical_and(c != 0, c != ncc).astype(F32)
    has_next = jnp.logical_and(c != ncc - 1, c != ncc + nlc - 1).astype(F32)
    xe_s[0:2] = xp_ref[...].astype(F32) * has_prev
    xe_s[2:tc + 2] = x_ref[...].astype(F32)
    xe_s[tc + 2:tc + 3] = xn_ref[...].astype(F32) * has_next

    for blk in range(LRU_BLOCKS):
        ls = slice(blk * bw, (blk + 1) * bw)
        xc = cb_ref[:, ls].reshape(1, 1, bw)
        for k in range(LRU_CONV):
            xc = xc + cw_ref[k:k + 1, ls].reshape(1, 1, bw) * xe_s[k:k + tc, :, ls]
        xc2 = xc.reshape(tc * nb, bw)
        gates = _dot(xc2.astype(BF16), wax_ref[blk])
        r = _sigmoid(gates[:, :bw] + ba_ref[:, ls])
        i = _sigmoid(gates[:, bw:] + bx_ref[:, ls])
        log_a = -LRU_C * r * _softplus(-lam_ref[:, ls])
        a = jnp.exp(log_a)
        a_s[:, :, ls] = a.reshape(tc, nb, bw)
        in_s[:, :, ls] = (jnp.sqrt(1.0 - a * a) * (i * xc2)).reshape(tc, nb, bw)

    def step(j, h):
        tt = tc - 1 - j if reverse else j
        h = a_s[tt] * h + in_s[tt]
        in_s[tt] = h
        return h

    h_s[...] = lax.fori_loop(0, tc, step, h_s[...], unroll=8)

    if reverse:
        o_ref[...] = ((hf_ref[...] + in_s[...]) * _gelu_tanh(gate_ref[...].astype(F32))).astype(BF16)
    else:
        o_ref[...] = in_s[...]


def lru_direction(x_t, gate_t, h_fwd, wts, n_ctx, tc, reverse):
    t, nb, w = x_t.shape
    ncc, nlc = n_ctx // tc, (t - n_ctx) // tc
    cw, cb, wax, ba, bx, lam = wts
    cidx = lambda s: _chunk_of_step(s, ncc, nlc, reverse)
    full = lambda shape: pl.BlockSpec(shape, lambda s: (0,) * len(shape))
    chunk = pl.BlockSpec((tc, nb, w), lambda s: (cidx(s), 0, 0))
    in_specs = [pl.BlockSpec((2, nb, w), lambda s: (jnp.maximum(cidx(s) * (tc // 2) - 1, 0), 0, 0)),
                chunk,
                pl.BlockSpec((1, nb, w), lambda s: (jnp.minimum((cidx(s) + 1) * tc, t - 1), 0, 0)),
                full(cw.shape), full(cb.shape), full(wax.shape), full(ba.shape), full(bx.shape), full(lam.shape)]
    args = [x_t, x_t, x_t, cw, cb, wax, ba, bx, lam]
    if reverse:
        in_specs += [chunk, chunk]
        args += [h_fwd, gate_t]
    return pl.pallas_call(
        functools.partial(_lru_body, ncc=ncc, nlc=nlc, tc=tc, reverse=reverse),
        grid=(ncc + nlc,),
        in_specs=in_specs,
        out_specs=chunk,
        out_shape=jax.ShapeDtypeStruct((t, nb, w), BF16 if reverse else F32),
        scratch_shapes=[pltpu.VMEM((tc + 3, nb, w), F32), pltpu.VMEM((tc, nb, w), F32),
                        pltpu.VMEM((tc, nb, w), F32), pltpu.VMEM((nb, w), F32)],
        compiler_params=_cparams("arbitrary"),
        name="lru_bwd" if reverse else "lru_fwd",
    )(*args)


def _ssd_body(*refs, ncc, nlc, reverse):
    if reverse:
        (xp_ref, x_ref, xn_ref, dtc_ref, dtr_ref, cw_ref, cb_ref, arow_ref, acol_ref, brow_ref, bcol_ref,
         yf_ref, z_ref, dskip_ref, gn_ref, o_ref, xe_s, st_s) = refs
    else:
        (xp_ref, x_ref, xn_ref, dtc_ref, dtr_ref, cw_ref, cb_ref, arow_ref, acol_ref, brow_ref, bcol_ref,
         o_ref, xe_s, st_s) = refs
    s = pl.program_id(1)
    c = _chunk_of_step(s, ncc, nlc, reverse)
    ln = SSD_CHUNK
    npair = SSD_HEADS // 2
    lane0 = SSD_HEADS if reverse else 0

    @pl.when(s == 0)
    def _():
        st_s[...] = jnp.zeros_like(st_s)

    has_prev = jnp.logical_and(c != 0, c != ncc).astype(F32)
    has_next = jnp.logical_and(c != ncc - 1, c != ncc + nlc - 1).astype(F32)
    xe_s[0:8] = xp_ref[...].astype(F32)[8:16] * has_prev
    xe_s[8:ln + 8] = x_ref[...].astype(F32)
    xe_s[ln + 8:ln + 16] = xn_ref[...].astype(F32)[0:8] * has_next
    xbc = cb_ref[...]
    for k in range(SSD_CONV):
        xbc = xbc + cw_ref[k:k + 1, :] * xe_s[pl.ds(6 + k, ln), :]
    xbc = _silu(xbc)

    dt_c = _softplus(dtc_ref[...].astype(F32)[:, 3 * LANE:4 * LANE] + brow_ref[...])
    da_c = dt_c * arow_ref[...]
    dt_r = _softplus(dtr_ref[...] + bcol_ref[...])
    da_r = dt_r * acol_ref[...]
    li = lax.broadcasted_iota(jnp.int32, (ln, ln), 0)
    mi = lax.broadcasted_iota(jnp.int32, (ln, ln), 1)
    keep = (li <= mi) if reverse else (li >= mi)
    keep_bf = jnp.where(keep, 1.0, 0.0).astype(BF16)
    keep_t_bf = jnp.where((li >= mi) if reverse else (li <= mi), 1.0, 0.0).astype(BF16)
    cs_c = sum(_dot(keep_bf, part) for part in _split3(da_c))
    cs_r = sum(_dot(part, keep_t_bf) for part in _split3(da_r))
    tot_c = jnp.sum(da_c, axis=0, keepdims=True)
    e_in = jnp.exp(cs_c)
    e_out = jnp.exp(tot_c - cs_c)
    e_tot = jnp.exp(tot_c)

    lane_first = lax.broadcasted_iota(jnp.int32, (1, LANE), 1) < SSD_HD
    row_first = lax.broadcasted_iota(jnp.int32, (LANE, 1), 0) < SSD_HD

    def col(v, h):
        return jnp.broadcast_to(v[:, lane0 + h:lane0 + h + 1], (v.shape[0], LANE))

    for pr in range(npair):
        g = pr // (npair // SSD_GROUPS)
        h0, h1 = 2 * pr, 2 * pr + 1
        bm = xbc[:, SSD_WIDTH + g * SSD_STATE:SSD_WIDTH + (g + 1) * SSD_STATE].astype(BF16)
        cm = xbc[:, SSD_WIDTH + (SSD_GROUPS + g) * SSD_STATE:SSD_WIDTH + (SSD_GROUPS + g + 1) * SSD_STATE].astype(BF16)
        xs = xbc[:, pr * LANE:(pr + 1) * LANE]
        xdt = xs * jnp.where(lane_first, col(dt_c, h0), col(dt_c, h1))
        cb = _dot_nt(cm, bm)
        y = None
        for h, sel in ((h0, lane_first), (h1, jnp.logical_not(lane_first))):
            dec = jnp.exp(jnp.where(keep, col(cs_c, h) - cs_r[h:h + 1, :], -jnp.inf))
            part = _dot((cb * dec).astype(BF16), jnp.where(sel, xdt, 0.0).astype(BF16))
            y = part if y is None else y + part
        st = st_s[pr]
        y_off = _dot_nt(cm, st.astype(BF16)) * jnp.where(lane_first, col(e_in, h0), col(e_in, h1))
        xw = (xdt * jnp.where(lane_first, col(e_out, h0), col(e_out, h1))).astype(BF16)
        new = _dot_tn(xw, bm)
        dec_st = jnp.where(row_first, e_tot[:, lane0 + h0:lane0 + h0 + 1], e_tot[:, lane0 + h1:lane0 + h1 + 1])
        st_s[pr] = st * dec_st + new
        y = y + y_off
        if reverse:
            y = y + yf_ref[:, pr * LANE:(pr + 1) * LANE] + xs * dskip_ref[:, pr * LANE:(pr + 1) * LANE]
            o_ref[:, pr * LANE:(pr + 1) * LANE] = y
        else:
            o_ref[:, pr * LANE:(pr + 1) * LANE] = y

    if reverse:
        yz = o_ref[...] * _silu(z_ref[...].astype(F32))
        gw = SSD_WIDTH // SSD_GROUPS
        for g in range(SSD_GROUPS):
            blk = yz[:, g * gw:(g + 1) * gw]
            blk = blk * lax.rsqrt(jnp.mean(blk * blk, axis=-1, keepdims=True) + LN_EPS) * gn_ref[:, g * gw:(g + 1) * gw]
            o_ref[:, g * gw:(g + 1) * gw] = blk


def ssd_direction(p_all, dt_t, y_fwd, wts, n_ctx, reverse):
    b, t, _ = p_all.shape
    ln = SSD_CHUNK
    ncc, nlc = n_ctx // ln, (t - n_ctx) // ln
    cw, cb, arow, acol, brow, bcol, dskip, gn = wts
    cidx = lambda s: _chunk_of_step(s, ncc, nlc, reverse)
    full = lambda a: pl.BlockSpec(a.shape, lambda bi, s: (0,) * a.ndim)
    xw = 2 * GROUP
    gx = G_SSD_XBC // 2
    hb = ln // 16
    in_specs = [pl.BlockSpec((None, 16, xw), lambda bi, s: (bi, jnp.maximum(cidx(s) * hb - 1, 0), gx)),
                pl.BlockSpec((None, ln, xw), lambda bi, s: (bi, cidx(s), gx)),
                pl.BlockSpec((None, 16, xw), lambda bi, s: (bi, jnp.minimum((cidx(s) + 1) * hb, t // 16 - 1), gx)),
                pl.BlockSpec((None, ln, GROUP), lambda bi, s: (bi, cidx(s), G_MLA_CQ)),
                pl.BlockSpec((None, SSD_HEADS, ln), lambda bi, s: (bi, 1 if reverse else 0, cidx(s))),
                full(cw), full(cb), full(arow), full(acol), full(brow), full(bcol)]
    args = [p_all, p_all, p_all, p_all, dt_t, cw, cb, arow, acol, brow, bcol]
    out_spec = pl.BlockSpec((None, ln, SSD_WIDTH), lambda bi, s: (bi, cidx(s), 0))
    if reverse:
        in_specs += [out_spec, pl.BlockSpec((None, ln, GROUP), lambda bi, s: (bi, cidx(s), G_SSD_Z)), full(dskip), full(gn)]
        args += [y_fwd, p_all, dskip, gn]
    return pl.pallas_call(
        functools.partial(_ssd_body, ncc=ncc, nlc=nlc, reverse=reverse),
        grid=(b, ncc + nlc),
        in_specs=in_specs,
        out_specs=out_spec,
        out_shape=jax.ShapeDtypeStruct((b, t, SSD_WIDTH), F32),
        scratch_shapes=[pltpu.VMEM((ln + 16, xw), F32), pltpu.VMEM((SSD_HEADS // 2, LANE, SSD_STATE), F32)],
        compiler_params=_cparams("parallel", "arbitrary"),
        name="ssd_bwd" if reverse else "ssd_fwd",
    )(*args)


MERGE_TM = 256


def _merge_body(x_ref, y0_ref, y1_ref, y2_ref, y3_ref, gt0_ref, gt1_ref, gt2_ref, gt3_ref, wb_ref, wo_ref, g1_ref, sh2_ref, sc2_ref,
                lng_ref, lnb_ref, rw_ref, rb_ref, x1_ref, v_ref, lg_ref):
    d = D_MODEL
    merged = None
    for k, (y_ref, gt_ref) in enumerate(((y0_ref, gt0_ref), (y1_ref, gt1_ref), (y2_ref, gt2_ref), (y3_ref, gt3_ref))):
        gate = _sigmoid(gt_ref[...].astype(F32))
        term = gate * _dot(y_ref[...].astype(BF16), wb_ref[k])
        merged = term if merged is None else merged + term
    m = _dot(merged.astype(BF16), wo_ref[...])
    r = DEEPNORM_ALPHA * x_ref[...] + g1_ref[...] * m
    rc = r - jnp.mean(r, axis=-1, keepdims=True)
    x1 = rc * lax.rsqrt(jnp.mean(rc * rc, axis=-1, keepdims=True) + LN_EPS) * lng_ref[...] + lnb_ref[...]
    x1_ref[...] = x1
    xc = x1 - jnp.mean(x1, axis=-1, keepdims=True)
    v = xc * lax.rsqrt(jnp.mean(xc * xc, axis=-1, keepdims=True) + LN_EPS) * (1.0 + sc2_ref[...]) + sh2_ref[...]
    v_ref[...] = v
    vh = v.astype(BF16)
    vl = (v - vh.astype(F32)).astype(BF16)
    lg_ref[...] = _dot(vh, rw_ref[0]) + _dot(vl, rw_ref[0]) + _dot(vh, rw_ref[1]) + rb_ref[...]


def merge_ln(x_all, ys, p_all, mod4, wts, n_ctx, row0):
    b, t, d = x_all.shape
    tm = MERGE_TM if (t % MERGE_TM == 0 and n_ctx % MERGE_TM == 0) else 128
    t0 = row0 // tm
    nt = t // tm - t0
    wb, wo, lng, lnb, rw, rb = wts
    nctx_t = n_ctx // tm
    row = lambda w, dt: pl.BlockSpec((None, tm, w), lambda bi, ti: (bi, ti + t0, 0))
    mod_spec = lambda k: pl.BlockSpec((None, None, 1, d),
                                      lambda bi, ti: (jnp.where(ti + t0 < nctx_t, b, bi), k, 0, 0))
    const = lambda a: pl.BlockSpec(a.shape, lambda bi, ti: (0,) * a.ndim, pipeline_mode=pl.Buffered(1))
    out_row = lambda w: pl.BlockSpec((None, tm, w), lambda bi, ti: (bi, ti, 0))
    gate_spec = lambda k: pl.BlockSpec((None, tm, d), lambda bi, ti: (bi, ti + t0, G_GATES * GROUP // d + k))
    return pl.pallas_call(
        _merge_body,
        grid=(b, nt),
        in_specs=[row(d, F32), row(GROUP, BF16), row(GROUP, BF16), row(GROUP, BF16), row(GROUP, F32),
                  gate_spec(0), gate_spec(1), gate_spec(2), gate_spec(3),
                  const(wb), const(wo), mod_spec(2), mod_spec(3), mod_spec(4),
                  const(lng), const(lnb), const(rw), const(rb)],
        out_specs=[out_row(d), out_row(d), out_row(LANE)],
        out_shape=[jax.ShapeDtypeStruct((b, nt * tm, d), F32), jax.ShapeDtypeStruct((b, nt * tm, d), F32),
                   jax.ShapeDtypeStruct((b, nt * tm, LANE), F32)],
        compiler_params=_cparams("parallel", "parallel"),
        name="merge_ln",
    )(x_all, *ys, p_all, p_all, p_all, p_all, wb, wo, mod4, mod4, mod4, lng, lnb, rw, rb)


COMBINE_TT = 128
GATHER_CHUNKS = 4


def _combine_body(slot_ref, y_hbm, x1_ref, w_ref, g2_ref, lng_ref, lnb_ref, o_ref, buf, sem):
    tt = x1_ref.shape[0]
    tc = tt // GATHER_CHUNKS
    rows = TOP_K * tc

    for c in range(GATHER_CHUNKS):
        def issue(a, carry, c=c):
            pltpu.make_async_copy(y_hbm.at[pl.ds(slot_ref[0, 0, a], 1)], buf.at[pl.ds(a, 1)], sem.at[c]).start()
            return carry

        lax.fori_loop(c * rows, (c + 1) * rows, issue, 0, unroll=8)
    for c in range(GATHER_CHUNKS):
        pltpu.make_async_copy(y_hbm.at[pl.ds(0, rows)], buf.at[pl.ds(c * rows, rows)], sem.at[c]).wait()
        tok = slice(c * tc, (c + 1) * tc)
        f = None
        for k in range(TOP_K):
            term = w_ref[tok, k:k + 1] * buf[c * rows + k * tc:c * rows + (k + 1) * tc, :]
            f = term if f is None else f + term
        r = DEEPNORM_ALPHA * x1_ref[tok, :] + g2_ref[...] * f
        rc = r - jnp.mean(r, axis=-1, keepdims=True)
        o_ref[tok, :] = (rc * lax.rsqrt(jnp.mean(rc * rc, axis=-1, keepdims=True) + LN_EPS) * lng_ref[...]
                         + lnb_ref[...])


def combine_ln(x1, y, slots, top_w, mod4, lng, lnb, n_ctx_rows):
    b, t, d = x1.shape
    tt = COMBINE_TT
    nt = t // tt
    nctx_t = n_ctx_rows // tt
    slot_tiles = (slots.reshape(b * nt, GATHER_CHUNKS, tt // GATHER_CHUNKS, TOP_K).transpose(0, 1, 3, 2)
                  .reshape(b * nt, 1, TOP_K * tt))
    row = pl.BlockSpec((None, tt, d), lambda bi, ti: (bi, ti, 0))
    const = lambda a: pl.BlockSpec(a.shape, lambda bi, ti: (0,) * a.ndim)
    return pl.pallas_call(
        _combine_body,
        grid=(b, nt),
        in_specs=[pl.BlockSpec((1, 1, TOP_K * tt), lambda bi, ti: (bi * nt + ti, 0, 0), memory_space=pltpu.SMEM),
                  pl.BlockSpec(memory_space=pl.ANY),
                  row,
                  pl.BlockSpec((None, tt, TOP_K), lambda bi, ti: (bi, ti, 0)),
                  pl.BlockSpec((None, None, 1, d), lambda bi, ti: (jnp.where(ti < nctx_t, b, bi), 5, 0, 0)),
                  const(lng), const(lnb)],
        out_specs=row,
        out_shape=jax.ShapeDtypeStruct((b, t, d), F32),
        scratch_shapes=[pltpu.VMEM((TOP_K * tt, d), F32), pltpu.SemaphoreType.DMA((GATHER_CHUNKS,))],
        compiler_params=_cparams("parallel", "parallel"),
        name="combine_ln",
    )(slot_tiles, y, x1, top_w.reshape(b, t, TOP_K), mod4, lng, lnb)


MOE_UP_TC = 1024
MOE_DOWN_TN = 1024
SEL_W = 512
W_CHUNKS = 4


def _refresh(be_ref, i):
    return jnp.logical_or(i == 0, be_ref[i] != be_ref[jnp.maximum(i - 1, 0)])


def _moe_up_body(be_ref, nvalid_ref, x_ref, *refs):
    w1_refs = refs[:W_CHUNKS]
    sel_ref, b1_ref, h_ref, wg_s, wl_s, bg_s, bl_s = refs[W_CHUNKS:]
    i = pl.program_id(1)
    half = SEL_W // 2
    rows = wg_s.shape[0] // W_CHUNKS

    @pl.when(_refresh(be_ref, i))
    def _():
        for c in range(MOE_UP_TC // SEL_W):
            for r, w1_ref in enumerate(w1_refs):
                both = _dot(w1_ref[:, c * SEL_W:(c + 1) * SEL_W].astype(BF16), sel_ref[...])
                wg_s[r * rows:(r + 1) * rows, c * half:(c + 1) * half] = both[:, :half].astype(BF16)
                wl_s[r * rows:(r + 1) * rows, c * half:(c + 1) * half] = both[:, half:].astype(BF16)
            b_hi, b_mid, b_lo = _split3(jnp.broadcast_to(b1_ref[:, c * SEL_W:(c + 1) * SEL_W], (8, SEL_W)))
            bias = _dot(b_hi, sel_ref[...]) + _dot(b_mid, sel_ref[...]) + _dot(b_lo, sel_ref[...])
            bg_s[:, c * half:(c + 1) * half] = bias[:, :half]
            bl_s[:, c * half:(c + 1) * half] = bias[:, half:]

    @pl.when(i >= nvalid_ref[0])
    def _():
        h_ref[...] = jnp.zeros_like(h_ref)

    @pl.when(i < nvalid_ref[0])
    def _():
        x = x_ref[...]
        glu = jnp.minimum(_dot(x, wg_s[...]) + bg_s[0:1, :], SWIGLU_LIMIT)
        lin = jnp.clip(_dot(x, wl_s[...]) + bl_s[0:1, :], -SWIGLU_LIMIT, SWIGLU_LIMIT)
        h_ref[...] = (glu * _sigmoid(SWIGLU_ALPHA * glu) * (lin + 1.0)).astype(BF16)


def _moe_down_body(be_ref, nvalid_ref, h_ref, *refs):
    w2_refs = refs[:W_CHUNKS]
    b2_ref, y_ref, w_s = refs[W_CHUNKS:]
    i = pl.program_id(1)
    rows = w_s.shape[0] // W_CHUNKS

    @pl.when(_refresh(be_ref, i))
    def _():
        for r, w2_ref in enumerate(w2_refs):
            w_s[r * rows:(r + 1) * rows, :] = w2_ref[...].astype(BF16)

    @pl.when(i >= nvalid_ref[0])
    def _():
        y_ref[...] = jnp.zeros_like(y_ref)

    @pl.when(i < nvalid_ref[0])
    def _():
        y_ref[...] = _dot(h_ref[...], w_s[...]) + b2_ref[...]


def moe_expert_blocks(x_sorted, block_e, n_valid, w1, b1, w2, b2, layer):
    n_slots, d = x_sorted.shape
    nb = n_slots // MOE_TM
    dh = w2.shape[2]
    half = SEL_W // 2
    src = jnp.arange(SEL_W)[:, None]
    dst = jnp.arange(SEL_W)[None, :]
    sel = (src == jnp.where(dst < half, 2 * dst, 2 * (dst - half) + 1)).astype(BF16)
    hidden = pl.pallas_call(
        _moe_up_body,
        grid_spec=pltpu.PrefetchScalarGridSpec(
            num_scalar_prefetch=2,
            grid=(2 * dh // MOE_UP_TC, nb),
            in_specs=[
                pl.BlockSpec((MOE_TM, d), lambda j, i, be, nv: (i, 0)),
                *[pl.BlockSpec((None, None, d // W_CHUNKS, MOE_UP_TC),
                               functools.partial(lambda j, i, be, nv, r: (layer, be[i], r, j), r=r))
                  for r in range(W_CHUNKS)],
                pl.BlockSpec((SEL_W, SEL_W), lambda j, i, be, nv: (0, 0)),
                pl.BlockSpec((None, None, 1, MOE_UP_TC), lambda j, i, be, nv: (layer, be[i], 0, j)),
            ],
            out_specs=pl.BlockSpec((MOE_TM, MOE_UP_TC // 2), lambda j, i, be, nv: (i, j)),
            scratch_shapes=[pltpu.VMEM((d, MOE_UP_TC // 2), BF16), pltpu.VMEM((d, MOE_UP_TC // 2), BF16),
                            pltpu.VMEM((8, MOE_UP_TC // 2), F32), pltpu.VMEM((8, MOE_UP_TC // 2), F32)],
        ),
        out_shape=jax.ShapeDtypeStruct((n_slots, dh), BF16),
        compiler_params=_cparams("arbitrary", "arbitrary"),
        name="moe_up",
    )(block_e, n_valid, x_sorted, *([w1] * W_CHUNKS), sel, b1.reshape(b1.shape[0], b1.shape[1], 1, b1.shape[2]))
    return pl.pallas_call(
        _moe_down_body,
        grid_spec=pltpu.PrefetchScalarGridSpec(
            num_scalar_prefetch=2,
            grid=(d // MOE_DOWN_TN, nb),
            in_specs=[
                pl.BlockSpec((MOE_TM, dh), lambda j, i, be, nv: (i, 0)),
                *[pl.BlockSpec((None, None, dh // W_CHUNKS, MOE_DOWN_TN),
                               functools.partial(lambda j, i, be, nv, r: (layer, be[i], r, j), r=r))
                  for r in range(W_CHUNKS)],
                pl.BlockSpec((None, None, 1, MOE_DOWN_TN), lambda j, i, be, nv: (layer, be[i], 0, j)),
            ],
            out_specs=pl.BlockSpec((MOE_TM, MOE_DOWN_TN), lambda j, i, be, nv: (i, j)),
            scratch_shapes=[pltpu.VMEM((dh, MOE_DOWN_TN), BF16)],
        ),
        out_shape=jax.ShapeDtypeStruct((n_slots, d), F32),
        compiler_params=_cparams("arbitrary", "arbitrary"),
        name="moe_down",
    )(block_e, n_valid, hidden, *([w2] * W_CHUNKS), b2.reshape(b2.shape[0], b2.shape[1], 1, b2.shape[2]))


def _dispatch_body(nvalid_ref, tok_ref, v_hbm, o_ref, buf, sem):
    i = pl.program_id(0)
    n = o_ref.shape[0]

    @pl.when(i >= nvalid_ref[0])
    def _():
        o_ref[...] = jnp.zeros_like(o_ref)

    @pl.when(i < nvalid_ref[0])
    def _():
        rows = n // GATHER_CHUNKS
        for c in range(GATHER_CHUNKS):
            def issue(a, carry, c=c):
                pltpu.make_async_copy(v_hbm.at[pl.ds(tok_ref[0, 0, a], 1)], buf.at[pl.ds(a, 1)], sem.at[c]).start()
                return carry

            lax.fori_loop(c * rows, (c + 1) * rows, issue, 0, unroll=8)
        for c in range(GATHER_CHUNKS):
            pltpu.make_async_copy(v_hbm.at[pl.ds(0, rows)], buf.at[pl.ds(c * rows, rows)], sem.at[c]).wait()
            o_ref[c * rows:(c + 1) * rows, :] = buf[c * rows:(c + 1) * rows, :].astype(BF16)


def dispatch_rows(v, slot_tok, n_valid):
    t, d = v.shape
    n_slots = slot_tok.shape[0]
    nb = n_slots // MOE_TM
    return pl.pallas_call(
        _dispatch_body,
        grid_spec=pltpu.PrefetchScalarGridSpec(
            num_scalar_prefetch=1,
            grid=(nb,),
            in_specs=[pl.BlockSpec((1, 1, MOE_TM), lambda i, nv: (i, 0, 0), memory_space=pltpu.SMEM),
                      pl.BlockSpec(memory_space=pl.ANY)],
            out_specs=pl.BlockSpec((MOE_TM, d), lambda i, nv: (i, 0)),
            scratch_shapes=[pltpu.VMEM((MOE_TM, d), F32), pltpu.SemaphoreType.DMA((GATHER_CHUNKS,))],
        ),
        out_shape=jax.ShapeDtypeStruct((n_slots, d), BF16),
        compiler_params=_cparams("arbitrary"),
        name="dispatch_rows",
    )(n_valid, slot_tok.reshape(nb, 1, MOE_TM), v)


def moe_ffn(h, logits, w1, b1, w2, b2, layer):
    t, d = h.shape
    top_logit, top_idx = lax.top_k(logits, TOP_K)
    top_w = jax.nn.softmax(top_logit, axis=-1)
    n_assign = t * TOP_K
    flat_e = top_idx.reshape(-1)
    onehot = (flat_e[:, None] == jnp.arange(N_EXPERTS, dtype=flat_e.dtype)[None, :]).astype(jnp.int32)
    csum = jnp.cumsum(onehot, axis=0)
    counts = csum[-1]
    padded = (counts + MOE_TM - 1) // MOE_TM * MOE_TM
    pend = jnp.cumsum(padded)
    pstart = pend - padded
    dest = (jnp.sum(onehot * (csum + pstart[None, :]), axis=1) - 1).astype(jnp.int32)
    n_blocks = (n_assign + N_EXPERTS * (MOE_TM - 1) + MOE_TM - 1) // MOE_TM
    n_slots = n_blocks * MOE_TM
    slot_tok = jnp.zeros((n_slots,), jnp.int32).at[dest].set(jnp.arange(n_assign, dtype=jnp.int32) // TOP_K)
    n_valid = (pend[-1] // MOE_TM).astype(jnp.int32)
    blk_row = jnp.minimum(jnp.arange(n_blocks), n_valid - 1) * MOE_TM
    block_e = jnp.minimum(jnp.sum((pend[None, :] <= blk_row[:, None]).astype(jnp.int32), axis=1), N_EXPERTS - 1)
    x_sorted = dispatch_rows(h, slot_tok, n_valid.reshape(1))
    y = moe_expert_blocks(x_sorted, block_e, n_valid.reshape(1), w1, b1, w2, b2, layer)
    return y, dest.reshape(t, TOP_K), top_w


def _rot_partner(w, width=64):
    d, n = w.shape
    g = w.reshape(d, n // width, 2, width // 2)
    return jnp.stack([-g[:, :, 1], g[:, :, 0]], axis=2).reshape(d, n)


def _axial_angles(n_tok, rot_dim):
    rows = n_tok // GRID_W
    row = jnp.repeat(jnp.arange(rows), GRID_W).astype(F32)
    col = jnp.tile(jnp.arange(GRID_W), rows).astype(F32)
    n_freq = rot_dim // 4
    inv = jnp.power(ROPE_BASE, -jnp.arange(n_freq, dtype=F32) / n_freq)
    return jnp.concatenate([row[:, None] * inv, col[:, None] * inv], axis=-1)


def _rope_tables(n_ctx, n_lat, rot_dim, n_groups):
    ang = _axial_angles(n_lat, rot_dim)
    cos = jnp.concatenate([jnp.ones((n_ctx, rot_dim // 2), F32), jnp.cos(ang)], axis=0)
    sin = jnp.concatenate([jnp.zeros((n_ctx, rot_dim // 2), F32), jnp.sin(ang)], axis=0)
    t = n_ctx + n_lat
    pad = LANE - n_groups * rot_dim
    cos = jnp.concatenate([cos] * (2 * n_groups) + [jnp.ones((t, pad), F32)], axis=1)
    sin = jnp.concatenate([sin] * (2 * n_groups) + [jnp.zeros((t, pad), F32)], axis=1)
    return cos, sin


def _layer_weights(i, a):
    d = D_MODEL
    w_in = a['w_in'][i]
    z = lambda n: jnp.zeros((d, n), F32)
    seg = lambda c0, n: w_in[:, c0:c0 + n]
    kr = seg(C_MLA_KR, MLA_ROPE)
    groups = [
        seg(C_MLA_CKV, MLA_KV_LORA), kr, z(64), _rot_partner(kr), z(64),
        seg(C_DIFF_K, 512), _rot_partner(seg(C_DIFF_K, 512)), seg(C_DIFF_V, 512),
        seg(C_SSD_XBC, SSD_XBC_W), seg(C_LRU_X, LRU_WIDTH),
        seg(C_MLA_CQ, MLA_Q_LORA), seg(C_SSD_DT, 2 * SSD_HEADS), z(LANE - 2 * SSD_HEADS),
        seg(C_DIFF_Q, 512), _rot_partner(seg(C_DIFF_Q, 512)),
        seg(C_LRU_GATE, LRU_WIDTH), seg(C_SSD_Z, SSD_WIDTH), seg(C_GATES, N_BRANCH * d),
    ]
    w_in_r = jnp.concatenate(groups, axis=1).astype(BF16)

    w_uq = a['mla_w_uq'][i].reshape(MLA_Q_LORA, MLA_HEADS, MLA_NOPE + MLA_ROPE)
    w_ukv = a['mla_w_ukv'][i].reshape(MLA_KV_LORA, MLA_HEADS, MLA_NOPE + MLA_V)
    zq = jnp.zeros((MLA_Q_LORA, 64), F32)
    wq = jnp.stack([jnp.concatenate([w_uq[:, h, :], zq], axis=1) for h in range(MLA_HEADS)]).astype(BF16)
    wqp = jnp.stack([jnp.concatenate([_rot_partner(w_uq[:, h, MLA_NOPE:]), zq], axis=1)
                     for h in range(MLA_HEADS)]).astype(BF16)
    wuk = jnp.transpose(w_ukv[:, :, :MLA_NOPE], (1, 0, 2)).astype(BF16)
    wuv = jnp.transpose(w_ukv[:, :, MLA_NOPE:], (1, 0, 2)).astype(BF16)
    mla = (a['mla_g_kv'][i].reshape(1, -1), a['mla_g_q'][i].reshape(1, -1), wuk, wuv, wq, wqp)

    lam_pack = jnp.zeros((8, LANE), F32)
    for r, name in enumerate(('diff_lq1', 'diff_lk1', 'diff_lq2', 'diff_lk2')):
        lam_pack = lam_pack.at[r, :DIFF_HD].set(a[name][i])
    g_sub = a['diff_g_sub'][i].reshape(1, -1)

    lru = []
    for dd in range(2):
        wax = jnp.concatenate([a['lru_w_a'][i, dd], a['lru_w_x'][i, dd]], axis=-1).astype(BF16)
        lru.append((a['lru_conv_w'][i], a['lru_conv_b'][i].reshape(1, -1), wax,
                    a['lru_b_a'][i, dd].reshape(1, -1), a['lru_b_x'][i, dd].reshape(1, -1),
                    a['lru_lam'][i, dd].reshape(1, -1)))

    ssd = []
    pad_row = lambda v: jnp.zeros((1, LANE), F32).at[0, :SSD_HEADS].set(v)
    for dd in range(2):
        a_neg = -jnp.exp(a['ssd_a_log'][i, dd])
        bias = a['ssd_dt_bias'][i, dd]
        arow = jnp.zeros((1, LANE), F32).at[0, dd * SSD_HEADS:(dd + 1) * SSD_HEADS].set(a_neg)
        brow = jnp.zeros((1, LANE), F32).at[0, dd * SSD_HEADS:(dd + 1) * SSD_HEADS].set(bias)
        ssd.append((a['ssd_conv_w'][i], a['ssd_conv_b'][i].reshape(1, -1), arow, a_neg.reshape(-1, 1),
                    brow, bias.reshape(-1, 1),
                    jnp.repeat(a['ssd_d'][i], SSD_HD).reshape(1, -1), a['ssd_g_norm'][i].reshape(1, -1), dd))

    rw = jnp.pad(a['router_w'][i], ((0, 0), (0, LANE - N_EXPERTS)))
    rwh = rw.astype(BF16)
    rwl = (rw - rwh.astype(F32)).astype(BF16)
    rb = jnp.pad(a['router_b'][i], (0, LANE - N_EXPERTS)).reshape(1, -1)
    merge = (a['w_branch'][i].astype(BF16), a['w_out'][i].astype(BF16),
             a['ln1_g'][i].reshape(1, -1), a['ln1_b'][i].reshape(1, -1), jnp.stack([rwh, rwl]), rb)
    return dict(w_in_r=w_in_r, mla=mla, lam_pack=lam_pack, g_sub=g_sub, lru=lru, ssd=ssd, merge=merge)


def _token_mixers(p_all, lw, tabs_mla, tabs_diff, layer_idx, n_ctx):
    b, t, _ = p_all.shape
    tq = 256 if (t % 256 == 0 and n_ctx % 256 == 0) else 128
    y_mla = mla_attn(p_all, tabs_mla, lw['mla'], n_ctx, tq)
    lambda_init = 0.8 - 0.6 * math.exp(-0.3 * layer_idx)
    y_diff = diff_attn(p_all, tabs_diff, lw['lam_pack'], lw['g_sub'], lambda_init, n_ctx, tq)

    x_t = jnp.transpose(p_all[:, :, G_LRU_X * GROUP:(G_LRU_X + 1) * GROUP], (1, 0, 2))
    g_t = jnp.transpose(p_all[:, :, G_LRU_GATE * GROUP:(G_LRU_GATE + 1) * GROUP], (1, 0, 2))
    tc = 256 if (t % 256 == 0 and n_ctx % 256 == 0) else 128
    h_f = lru_direction(x_t, None, None, lw['lru'][0], n_ctx, tc, False)
    y_lru = jnp.transpose(lru_direction(x_t, g_t, h_f, lw['lru'][1], n_ctx, tc, True), (1, 0, 2))

    dt0 = G_MLA_CQ * GROUP + MLA_Q_LORA
    dt_t = jnp.transpose(p_all[:, :, dt0:dt0 + 2 * SSD_HEADS].astype(F32), (0, 2, 1))
    ssd_f, ssd_b = lw['ssd']
    y_f = ssd_direction(p_all, dt_t, None, ssd_f[:8], n_ctx, False)
    y_ssd = ssd_direction(p_all, dt_t, y_f, ssd_b[:8], n_ctx, True)
    return y_mla, y_diff, y_lru, y_ssd


def kernel(x, c, ctx, c_ctx, w_mod, b_mod, w_in, mla_g_q, mla_g_kv, mla_w_uq, mla_w_ukv, diff_lq1, diff_lk1, diff_lq2, diff_lk2, diff_g_sub, lru_conv_w, lru_conv_b, lru_w_a, lru_b_a, lru_w_x, lru_b_x, lru_lam, ssd_conv_w, ssd_conv_b, ssd_a_log, ssd_dt_bias, ssd_d, ssd_g_norm, w_branch, w_out, ln1_g, ln1_b, ln2_g, ln2_b, router_w, router_b, w1, b1, w2, b2):
    a = dict(w_in=w_in, mla_g_q=mla_g_q, mla_g_kv=mla_g_kv, mla_w_uq=mla_w_uq, mla_w_ukv=mla_w_ukv,
             diff_lq1=diff_lq1, diff_lk1=diff_lk1, diff_lq2=diff_lq2, diff_lk2=diff_lk2, diff_g_sub=diff_g_sub,
             lru_conv_w=lru_conv_w, lru_conv_b=lru_conv_b, lru_w_a=lru_w_a, lru_b_a=lru_b_a, lru_w_x=lru_w_x,
             lru_b_x=lru_b_x, lru_lam=lru_lam, ssd_conv_w=ssd_conv_w, ssd_conv_b=ssd_conv_b, ssd_a_log=ssd_a_log,
             ssd_dt_bias=ssd_dt_bias, ssd_d=ssd_d, ssd_g_norm=ssd_g_norm, w_branch=w_branch, w_out=w_out,
             ln1_g=ln1_g, ln1_b=ln1_b, router_w=router_w, router_b=router_b)
    bsz, n_lat, d = x.shape
    n_ctx = ctx.shape[1]
    t = n_ctx + n_lat
    depth = w_in.shape[0]
    tabs_mla = _rope_tables(n_ctx, n_lat, MLA_ROPE, 1)
    tabs_diff = _rope_tables(n_ctx, n_lat, DIFF_HD, 2)
    x_all = jnp.concatenate([ctx, x], axis=1)
    cond = jnp.zeros((16, d), F32).at[:bsz].set(jax.nn.silu(c)).at[bsz].set(jax.nn.silu(c_ctx))

    for i in range(depth):
        last = i == depth - 1
        lw = _layer_weights(i, a)
        mod4 = matmul_bias(cond, w_mod[i], b_mod[i]).reshape(16, 6, 1, d)
        p_all = ln_mod_proj(x_all, mod4, lw['w_in_r'], n_ctx)
        ys = _token_mixers(p_all, lw, tabs_mla, tabs_diff, i, n_ctx)
        row0 = n_ctx if last else 0
        x1, v, logits = merge_ln(x_all, ys, p_all, mod4, lw['merge'], n_ctx, row0)
        tr = t - row0
        y, slots, top_w = moe_ffn(v.reshape(bsz * tr, d), logits.reshape(bsz * tr, LANE)[:, :N_EXPERTS],
                                  w1, b1, w2, b2, i)
        x_all = combine_ln(x1, y, slots, top_w, mod4, ln2_g[i].reshape(1, -1), ln2_b[i].reshape(1, -1),
                           n_ctx - row0)
    return x_all
```
